```python
import math
import jax, jax.numpy as jnp
from jax import lax
import numpy as np

D_MODEL = 1024
BATCH = 2
SEQ = 8192
DEPTH = 4
DEC_BATCH = 32
DEC_SEQ = 4
PAST_LEN = 8192
PAGE_SIZE = 128

N_MIXERS = 3
MIXER_POOL = 0
MIXER_SSM = 1
MIXER_ATTN = 2
LAYER_KINDS = tuple(i % N_MIXERS for i in range(DEPTH))
N_POOL_LAYERS = sum(1 for kd in LAYER_KINDS if kd == MIXER_POOL)
N_SSM_LAYERS = sum(1 for kd in LAYER_KINDS if kd == MIXER_SSM)
N_ATTN_LAYERS = sum(1 for kd in LAYER_KINDS if kd == MIXER_ATTN)

POOL_WINDOWS = (2, 4, 8, 16)
POOL_GROUPS = len(POOL_WINDOWS)
POOL_GW = D_MODEL // POOL_GROUPS
POOL_HIST = max(POOL_WINDOWS) - 1

SSM_GROUP_CH = 16
SSM_GROUPS = D_MODEL // SSM_GROUP_CH
SSM_STATE = 64
DT_MIN = 1e-3
DT_MAX = 1e-1

N_HEADS = 8
HEAD_DIM = D_MODEL // (2 * N_HEADS)
ROPE_THETA = 10000.0
Q_BLOCK = 128

N_EXPERTS = 16
N_EXPERT_GROUPS = 4
EXPERTS_PER_GROUP = N_EXPERTS // N_EXPERT_GROUPS
TOP_K = 2
D_EXPERT = 512
MOE_BLOCK = 128

NORM_EPS = 1e-6
NEG_INF = -1e30

kernel_name = 'hybrid_pool_s5_diffattn_groupmoe_adaln_step'


def _rms(x, g):
    xf = x.astype(jnp.float32)
    return xf * lax.rsqrt(jnp.mean(xf * xf, axis=-1, keepdims=True) + NORM_EPS) * g.astype(jnp.float32)


def _rope(x, pos):
    half = HEAD_DIM // 2
    inv = jnp.power(ROPE_THETA, -jnp.arange(half, dtype=jnp.float32) * (2.0 / HEAD_DIM))
    ang = pos.astype(jnp.float32)[:, None] * inv[None, :]
    cos = jnp.cos(ang)[None, :, None, None, :]
    sin = jnp.sin(ang)[None, :, None, None, :]
    x1, x2 = x[..., :half], x[..., half:]
    return jnp.concatenate([x1 * cos - x2 * sin, x2 * cos + x1 * sin], axis=-1)


def _pool_mixer(h, past, pos0, w, b, ls):
    Bsz, L, _ = h.shape
    ext = jnp.concatenate([past.astype(jnp.float32), h], axis=1)
    cs = jnp.concatenate([jnp.zeros_like(ext[:, :1]), jnp.cumsum(ext, axis=1)], axis=1)
    pos = pos0 + jnp.arange(L)
    hi = cs[:, POOL_HIST + 1:]
    outs = []
    for g, wdw in enumerate(POOL_WINDOWS):
        sl = slice(g * POOL_GW, (g + 1) * POOL_GW)
        lo = cs[:, POOL_HIST + 1 - wdw:POOL_HIST + 1 - wdw + L, sl]
        cnt = jnp.minimum(pos + 1, wdw).astype(jnp.float32)[None, :, None]
        outs.append((hi[..., sl] - lo) / cnt - h[..., sl])
    pooled = jnp.stack(outs, axis=2)
    y = jnp.einsum('blgc,gce->blge', pooled, w).reshape(Bsz, L, D_MODEL) + b
    return y * ls, ext[:, -POOL_HIST:]


def _cmul(ar, ai, br, bi):
    return ar * br - ai * bi, ar * bi + ai * br


def _s5_mixer(u, h0r, h0i, lam_re, lam_im, log_dt, b_re, b_im, c_re, c_im, d_skip, w_glu, b_glu):
    Bsz, L, _ = u.shape
    ug = u.reshape(Bsz, L, SSM_GROUPS, SSM_GROUP_CH)
    dt = jnp.exp(log_dt.astype(jnp.float32))[:, None]
    lr = lam_re.astype(jnp.float32)
    li = lam_im.astype(jnp.float32)
    mag = jnp.exp(lr * dt)
    a_r, a_i = mag * jnp.cos(li * dt), mag * jnp.sin(li * dt)
    den = lr * lr + li * li
    nr, ni = a_r - 1.0, a_i
    k_r = (nr * lr + ni * li) / den
    k_i = (ni * lr - nr * li) / den
    bu_r = jnp.einsum('blgc,gpc->blgp', ug, b_re)
    bu_i = jnp.einsum('blgc,gpc->blgp', ug, b_im)
    x_r, x_i = _cmul(k_r, k_i, bu_r, bu_i)
    ar_b = jnp.broadcast_to(a_r, x_r.shape)
    ai_b = jnp.broadcast_to(a_i, x_i.shape)

    def combine(e1, e2):
        a1r, a1i, b1r, b1i = e1
        a2r, a2i, b2r, b2i = e2
        ar, ai = _cmul(a2r, a2i, a1r, a1i)
        tr, ti = _cmul(a2r, a2i, b1r, b1i)
        return ar, ai, tr + b2r, ti + b2i

    pr, pi, sr, si = lax.associative_scan(combine, (ar_b, ai_b, x_r, x_i), axis=1)
    hr, hi = _cmul(pr, pi, h0r.astype(jnp.float32)[:, None], h0i.astype(jnp.float32)[:, None])
    sr = sr + hr
    si = si + hi
    y = jnp.einsum('blgp,gcp->blgc', sr, c_re) - jnp.einsum('blgp,gcp->blgc', si, c_im)
    y = y.reshape(Bsz, L, D_MODEL) + d_skip * u
    z = jax.nn.gelu(y)
    gu = z @ w_glu + b_glu
    out = gu[..., :D_MODEL] * jax.nn.sigmoid(gu[..., D_MODEL:])
    return out, sr[:, -1], si[:, -1]


def _diff_core(q, k, v, q_pos, k_pos, lam):
    s = jnp.einsum('bqhcd,bkhcd->bchqk', q, k) * (HEAD_DIM ** -0.5)
    mask = k_pos[None, :] <= q_pos[:, None]
    s = jnp.where(mask, s, NEG_INF)
    p = jax.nn.softmax(s, axis=-1)
    a = p[:, 0] - lam * p[:, 1]
    return jnp.einsum('bhqk,bkhe->bqhe', a, v)


def _diff_prompt(q, k, v, lam):
    Bsz, L = q.shape[:2]
    nb = L // Q_BLOCK
    qb = jnp.moveaxis(q.reshape(Bsz, nb, Q_BLOCK, N_HEADS, 2, HEAD_DIM), 1, 0)
    k_pos = jnp.arange(L)

    def block(args):
        qi, s0 = args
        return _diff_core(qi, k, v, s0 + jnp.arange(Q_BLOCK), k_pos, lam)

    o = lax.map(block, (qb, jnp.arange(nb) * Q_BLOCK))
    return jnp.moveaxis(o, 0, 1).reshape(Bsz, L, N_HEADS, 2 * HEAD_DIM)


def _diff_attn_mixer(h, pos0, layer_idx, w_qkv, q_norm, k_norm, lq1, lk1, lq2, lk2, subln, w_o, k_past, v_past):
    Bsz, L, _ = h.shape
    qkv = h @ w_qkv
    q = qkv[..., :D_MODEL].reshape(Bsz, L, N_HEADS, 2, HEAD_DIM)
    k = qkv[..., D_MODEL:2 * D_MODEL].reshape(Bsz, L, N_HEADS, 2, HEAD_DIM)
    v = qkv[..., 2 * D_MODEL:].reshape(Bsz, L, N_HEADS, 2 * HEAD_DIM).astype(jnp.float32)
    pos = pos0 + jnp.arange(L)
    q = _rope(_rms(q, q_norm), pos)
    k = _rope(_rms(k, k_norm), pos)
    lam_init = 0.8 - 0.6 * math.exp(-0.3 * layer_idx)
    lam = (jnp.exp(jnp.sum(lq1.astype(jnp.float32) * lk1.astype(jnp.float32)))
           - jnp.exp(jnp.sum(lq2.astype(jnp.float32) * lk2.astype(jnp.float32))) + lam_init)
    if k_past is None:
        o = _diff_prompt(q, k, v, lam)
    else:
        P = k_past.shape[1]
        kk = jnp.concatenate([k_past.reshape(Bsz, P, N_HEADS, 2, HEAD_DIM).astype(jnp.float32), k], axis=1)
        vv = jnp.concatenate([v_past.astype(jnp.float32), v], axis=1)
        o = _diff_core(q, kk, vv, pos, jnp.arange(P + L), lam)
    o = _rms(o, subln) * (1.0 - lam_init)
    y = o.reshape(Bsz, L, D_MODEL) @ w_o
    return y, k.reshape(Bsz, L, N_HEADS, 2 * HEAD_DIM), v


def _moe(h, w_router, b_router, w_gu, w_down):
    n = h.shape[0]
    logits = h.astype(jnp.float32) @ w_router.astype(jnp.float32) + b_router.astype(jnp.float32)
    probs = jax.nn.softmax(logits, axis=-1).reshape(n, N_EXPERT_GROUPS, EXPERTS_PER_GROUP)
    top_v, top_i = lax.top_k(probs, TOP_K)
    gsel = jnp.argmax(jnp.sum(top_v, axis=-1), axis=-1)
    sel_v = jnp.take_along_axis(top_v, gsel[:, None, None], axis=1)[:, 0]
    sel_i = jnp.take_along_axis(top_i, gsel[:, None, None], axis=1)[:, 0]
    gates = sel_v / jnp.sum(sel_v, axis=-1, keepdims=True)
    eid = gsel[:, None] * EXPERTS_PER_GROUP + sel_i
    cw = jnp.sum(gates[..., None] * jax.nn.one_hot(eid, N_EXPERTS, dtype=jnp.float32), axis=1)
    gu = jnp.einsum('nd,edf->nef', h, w_gu)
    act = jax.nn.silu(gu[..., :D_EXPERT]) * gu[..., D_EXPERT:]
    return jnp.einsum('nef,efd->nd', act * cw[..., None], w_down)


def _trunk(x, c, pos0, pool_past, ssm_re0, ssm_im0, kv_src, W, blocked):
    cond = jax.nn.silu(c.astype(jnp.float32))
    pools, ssm_re, ssm_im, k_rows, v_rows = [], [], [], [], []
    ip = iq = ia = 0
    for l in range(DEPTH):
        ada = cond @ W['w_ada'][l] + W['b_ada'][l]
        sh1, sc1, g1, sh2, sc2, g2 = [t[:, None, :] for t in jnp.split(ada, 6, axis=-1)]
        h = _rms(x, W['norm_mix'][l]) * (1.0 + sc1) + sh1
        kind = LAYER_KINDS[l]
        if kind == MIXER_POOL:
            y, st = _pool_mixer(h, pool_past[ip], pos0, W['w_pool'][ip], W['b_pool'][ip], W['ls_pool'][ip])
            pools.append(st)
            ip += 1
        elif kind == MIXER_SSM:
            y, sr, si = _s5_mixer(h, ssm_re0[iq], ssm_im0[iq], W['ssm_lambda_re'][iq], W['ssm_lambda_im'][iq],
                                  W['ssm_log_dt'][iq], W['ssm_b_re'][iq], W['ssm_b_im'][iq],
                                  W['ssm_c_re'][iq], W['ssm_c_im'][iq], W['ssm_d'][iq],
                                  W['w_glu'][iq], W['b_glu'][iq])
            ssm_re.append(sr)
            ssm_im.append(si)
            iq += 1
        else:
            if kv_src is None:
                k_past = None
                v_past = None
            else:
                ck, cv, pt = kv_src
                nseq, npg = pt.shape
                k_past = ck[ia, pt].reshape(nseq, npg * PAGE_SIZE, N_HEADS, 2 * HEAD_DIM)
                v_past = cv[ia, pt].reshape(nseq, npg * PAGE_SIZE, N_HEADS, 2 * HEAD_DIM)
            y, kr, vr = _diff_attn_mixer(h, pos0, l, W['w_qkv'][ia], W['q_norm'][ia], W['k_norm'][ia],
                                         W['lambda_q1'][ia], W['lambda_k1'][ia], W['lambda_q2'][ia],
                                         W['lambda_k2'][ia], W['subln'][ia], W['w_o'][ia], k_past, v_past)
            k_rows.append(kr)
            v_rows.append(vr)
            ia += 1
        x = x + (g1 * y).astype(x.dtype)
        h = _rms(x, W['norm_ff'][l]) * (1.0 + sc2) + sh2
        wr, br, wgu, wdn = W['w_router'], W['b_router'], W['w_gate_up'][l], W['w_down'][l]
        if blocked:
            y = lax.map(lambda t: _moe(t, wr, br, wgu, wdn), h.reshape(-1, MOE_BLOCK, D_MODEL)).reshape(h.shape)
        else:
            y = _moe(h.reshape(-1, D_MODEL), wr, br, wgu, wdn).reshape(h.shape)
        x = x + (g2 * y).astype(x.dtype)
    return x, jnp.stack(k_rows), jnp.stack(v_rows), jnp.stack(pools), jnp.stack(ssm_re), jnp.stack(ssm_im)


def setup_inputs(seed: int = 0) -> dict:
    key = jax.random.key(seed)
    ks = iter(jax.random.split(key, 64))
    f32 = jnp.float32

    def nrm(shape, s=1.0):
        return jax.random.normal(next(ks), shape, f32) * s

    n_pages = PAST_LEN // PAGE_SIZE
    n_used = DEC_BATCH * n_pages
    n_phys = n_used + max(1, n_used // 4)
    NP, NS, NA = N_POOL_LAYERS, N_SSM_LAYERS, N_ATTN_LAYERS
    G, P, C = SSM_GROUPS, SSM_STATE, SSM_GROUP_CH
    D = D_MODEL
    inp = {}
    inp['x_prompt'] = nrm((BATCH, SEQ, D))
    inp['x_sample'] = nrm((DEC_BATCH, DEC_SEQ, D))
    inp['cache_k'] = nrm((NA, n_phys, PAGE_SIZE, N_HEADS, 2 * HEAD_DIM))
    inp['cache_v'] = nrm((NA, n_phys, PAGE_SIZE, N_HEADS, 2 * HEAD_DIM))
    inp['state_pool'] = nrm((NP, DEC_BATCH, POOL_HIST, D))
    inp['state_ssm_re'] = nrm((NS, DEC_BATCH, G, P), 0.1)
    inp['state_ssm_im'] = nrm((NS, DEC_BATCH, G, P), 0.1)
    inp['page_table'] = jax.random.permutation(next(ks), n_phys)[:n_used].reshape(DEC_BATCH, n_pages).astype(jnp.int32)
    inp['c_prompt'] = nrm((BATCH, D))
    inp['c_sample'] = nrm((DEC_BATCH, D))
    inp['w_ada'] = nrm((DEPTH, D, 6 * D), 0.5 * D ** -0.5)
    inp['b_ada'] = nrm((DEPTH, 6 * D), 0.02)
    inp['norm_mix'] = 1.0 + nrm((DEPTH, D), 0.05)
    inp['norm_ff'] = 1.0 + nrm((DEPTH, D), 0.05)
    inp['w_pool'] = nrm((NP, POOL_GROUPS, POOL_GW, POOL_GW), POOL_GW ** -0.5)
    inp['b_pool'] = nrm((NP, D), 0.02)
    inp['ls_pool'] = 1.0 + nrm((NP, D), 0.05)
    inp['ssm_lambda_re'] = -0.5 + nrm((NS, G, P), 0.01)
    inp['ssm_lambda_im'] = jnp.pi * jnp.arange(P, dtype=f32)[None, None, :] + nrm((NS, G, P), 0.01)
    inp['ssm_log_dt'] = jax.random.uniform(next(ks), (NS, G), f32, math.log(DT_MIN), math.log(DT_MAX))
    inp['ssm_b_re'] = nrm((NS, G, P, C), (2.0 * C) ** -0.5)
    inp['ssm_b_im'] = nrm((NS, G, P, C), (2.0 * C) ** -0.5)
    inp['ssm_c_re'] = nrm((NS, G, C, P), P ** -0.5)
    inp['ssm_c_im'] = nrm((NS, G, C, P), P ** -0.5)
    inp['ssm_d'] = nrm((NS, D))
    inp['w_glu'] = nrm((NS, D, 2 * D), D ** -0.5)
    inp['b_glu'] = nrm((NS, 2 * D), 0.02)
    inp['w_qkv'] = nrm((NA, D, 3 * D), D ** -0.5)
    inp['q_norm'] = 1.0 + nrm((NA, HEAD_DIM), 0.05)
    inp['k_norm'] = 1.0 + nrm((NA, HEAD_DIM), 0.05)
    inp['lambda_q1'] = nrm((NA, HEAD_DIM), 0.1)
    inp['lambda_k1'] = nrm((NA, HEAD_DIM), 0.1)
    inp['lambda_q2'] = nrm((NA, HEAD_DIM), 0.1)
    inp['lambda_k2'] = nrm((NA, HEAD_DIM), 0.1)
    inp['subln'] = 1.0 + nrm((NA, 2 * HEAD_DIM), 0.05)
    inp['w_o'] = nrm((NA, D, D), D ** -0.5)
    inp['w_router'] = nrm((D, N_EXPERTS), D ** -0.5)
    inp['b_router'] = nrm((N_EXPERTS,), 0.01)
    inp['w_gate_up'] = nrm((DEPTH, N_EXPERTS, D, 2 * D_EXPERT), D ** -0.5)
    inp['w_down'] = nrm((DEPTH, N_EXPERTS, D_EXPERT, D), D_EXPERT ** -0.5)
    return inp


def reference(x_prompt, x_sample, cache_k, cache_v, state_pool, state_ssm_re, state_ssm_im, page_table,
              c_prompt, c_sample, w_ada, b_ada, norm_mix, norm_ff, w_pool, b_pool, ls_pool,
              ssm_lambda_re, ssm_lambda_im, ssm_log_dt, ssm_b_re, ssm_b_im, ssm_c_re, ssm_c_im, ssm_d,
              w_glu, b_glu, w_qkv, q_norm, k_norm, lambda_q1, lambda_k1, lambda_q2, lambda_k2, subln, w_o,
              w_router, b_router, w_gate_up, w_down):
    W = dict(w_ada=w_ada, b_ada=b_ada, norm_mix=norm_mix, norm_ff=norm_ff, w_pool=w_pool, b_pool=b_pool,
             ls_pool=ls_pool, ssm_lambda_re=ssm_lambda_re, ssm_lambda_im=ssm_lambda_im, ssm_log_dt=ssm_log_dt,
             ssm_b_re=ssm_b_re, ssm_b_im=ssm_b_im, ssm_c_re=ssm_c_re, ssm_c_im=ssm_c_im, ssm_d=ssm_d,
             w_glu=w_glu, b_glu=b_glu, w_qkv=w_qkv, q_norm=q_norm, k_norm=k_norm, lambda_q1=lambda_q1,
             lambda_k1=lambda_k1, lambda_q2=lambda_q2, lambda_k2=lambda_k2, subln=subln, w_o=w_o,
             w_router=w_router, b_router=b_router, w_gate_up=w_gate_up, w_down=w_down)
    nb_p = x_prompt.shape[0]
    pool0 = jnp.zeros((N_POOL_LAYERS, nb_p, POOL_HIST, D_MODEL), jnp.float32)
    ssm0 = jnp.zeros((N_SSM_LAYERS, nb_p, SSM_GROUPS, SSM_STATE), jnp.float32)
    y_prompt, k_prompt, v_prompt, pool_prompt, ssm_re_prompt, ssm_im_prompt = _trunk(
        x_prompt, c_prompt, 0, pool0, ssm0, ssm0, None, W, True)
    past_len = page_table.shape[1] * PAGE_SIZE
    y_sample, k_sample, v_sample, pool_sample, ssm_re_sample, ssm_im_sample = _trunk(
        x_sample, c_sample, past_len, state_pool, state_ssm_re, state_ssm_im,
        (cache_k, cache_v, page_table), W, False)
    return (y_prompt, y_sample, k_prompt, v_prompt, k_sample, v_sample, pool_prompt, pool_sample,
            ssm_re_prompt, ssm_im_prompt, ssm_re_sample, ssm_im_sample)
```

```python
import functools
import math

import jax
import jax.numpy as jnp
from jax import lax
from jax.experimental import pallas as pl
from jax.experimental.pallas import tpu as pltpu

f32 = jnp.float32
bf16 = jnp.bfloat16
i32 = jnp.int32

N_MIXERS = 3
POOL_WINDOWS = (2, 4, 8, 16)
POOL_HIST = 15
HALO = 16
SSM_GROUP_CH = 16
SSM_STATE = 64
SLAB_GROUPS = 8
N_HEADS = 8
HEAD_DIM = 64
ROPE_THETA = 10000.0
N_EXPERTS = 16
EXPERTS_PER_GROUP = 4
NORM_EPS = 1e-6
NEG_INF = -1e30
LANES = 128
VMEM_LIMIT = 48 * 1024 * 1024

TB_PROMPT = 512
TB_POOL = 256
T_SSM = 256
TQ_ATTN = 256
TM_MOE = 256
PAGES_PER_STEP = 8


def _cparams(*sem):
    return pltpu.CompilerParams(dimension_semantics=sem, vmem_limit_bytes=VMEM_LIMIT)


def _mm(a, b):
    return jnp.dot(a, b, preferred_element_type=f32)


def _nt(a, b):
    return lax.dot_general(a, b, (((1,), (1,)), ((), ())), preferred_element_type=f32)


def _split2(a):
    hi = a.astype(bf16)
    lo = (a - hi.astype(f32)).astype(bf16)
    return hi, lo


def _split3(a):
    hi = a.astype(bf16)
    r = a - hi.astype(f32)
    mid = r.astype(bf16)
    lo = (r - mid.astype(f32)).astype(bf16)
    return hi, mid, lo


def _norm_mod(x, g, sc, sh):
    ms = jnp.mean(x * x, axis=-1, keepdims=True)
    return x * lax.rsqrt(ms + NORM_EPS) * g * (1.0 + sc) + sh


def _gelu_tanh(y):
    return 0.5 * y * (1.0 + jnp.tanh(0.7978845608028654 * (y + 0.044715 * y * y * y)))


def _sigmoid(x):
    return 1.0 / (1.0 + jnp.exp(-x))


def _full_spec(a):
    nd = a.ndim
    return pl.BlockSpec(a.shape, lambda i, _n=nd: (0,) * _n)


def _rows_call(body, *, tb, nblk, off, bps, row_ins, mod_ins, seq_ins, consts, outs, prev=None, name):
    in_specs, args = [], []
    for a in row_ins:
        in_specs.append(pl.BlockSpec((tb, a.shape[1]), lambda i: (i + off, 0)))
        args.append(a)
    for a, k in mod_ins:
        r, d = a.shape[2], a.shape[3]
        in_specs.append(pl.BlockSpec((None, None, r, d), lambda i, _k=k: (_k, i // bps, 0, 0)))
        args.append(a)
    for a in seq_ins:
        in_specs.append(pl.BlockSpec((tb, a.shape[1]), lambda i: (i % bps, 0)))
        args.append(a)
    for a in consts:
        in_specs.append(_full_spec(a))
        args.append(a)
    n_real = len(args)
    aliases = {}
    if prev is not None:
        for j, a in enumerate(prev):
            in_specs.append(pl.BlockSpec(memory_space=pl.ANY))
            args.append(a)
            aliases[n_real + j] = j
    out_specs, out_shapes = [], []
    for shape, dtype, kind in outs:
        if kind == 'row':
            out_specs.append(pl.BlockSpec((tb, shape[1]), lambda i: (i + off, 0)))
        else:
            out_specs.append(pl.BlockSpec((shape[0], tb), lambda i: (0, i + off)))
        out_shapes.append(jax.ShapeDtypeStruct(shape, dtype))
    n_prev = 0 if prev is None else len(prev)

    def wrapped(*refs):
        body(*refs[:n_real], *refs[n_real + n_prev:])

    res = pl.pallas_call(
        wrapped, grid=(nblk,), in_specs=in_specs, out_specs=out_specs, out_shape=out_shapes,
        input_output_aliases=aliases, compiler_params=_cparams("arbitrary"), name=name)(*args)
    return list(res)


def _both_groups(body, dims, *, row_ins, mod_ks, consts, outs, name, seq_ins_p=(), seq_ins_s=()):
    modp, mods = dims['modp'], dims['mods']
    n_p, n_s, seq = dims['n_p'], dims['n_s'], dims['seq']
    res = _rows_call(body, tb=TB_PROMPT, nblk=n_p // TB_PROMPT, off=0, bps=seq // TB_PROMPT,
                     row_ins=row_ins, mod_ins=[(modp, k) for k in mod_ks], seq_ins=list(seq_ins_p),
                     consts=consts, outs=outs, name=name + "_p")
    res = _rows_call(body, tb=n_s, nblk=1, off=n_p // n_s, bps=1,
                     row_ins=row_ins, mod_ins=[(mods, k) for k in mod_ks], seq_ins=list(seq_ins_s),
                     consts=consts, outs=outs, prev=res, name=name + "_s")
    return res


def _ada_body(c_ref, w_ref, b_ref, o_ref):
    c = c_ref[...]
    cond = c * _sigmoid(c)
    c_hi, c_lo = _split2(cond)
    w_hi, w_lo = _split2(w_ref[...])
    o_ref[...] = _mm(c_hi, w_hi) + _mm(c_hi, w_lo) + _mm(c_lo, w_hi) + b_ref[...]


def _ada_all(c_all, w_ada, b_ada):
    depth, d, d6 = w_ada.shape
    n = c_all.shape[0]
    nj = d6 // d
    return pl.pallas_call(
        _ada_body, grid=(depth, nj),
        in_specs=[pl.BlockSpec((n, d), lambda l, j: (0, 0)),
                  pl.BlockSpec((None, d, d), lambda l, j: (l, 0, j)),
                  pl.BlockSpec((None, 1, d), lambda l, j: (l, 0, j))],
        out_specs=pl.BlockSpec((None, n, d), lambda l, j: (l, 0, j)),
        out_shape=jax.ShapeDtypeStruct((depth, n, d6), f32),
        compiler_params=_cparams("arbitrary", "arbitrary"), name="ada")(
            c_all, w_ada, b_ada.reshape(depth, 1, d6))


def _router_body(x_ref, sh_ref, sc_ref, nw_ref, wrt_ref, br_ref, h_ref, ids_ref, gates_ref):
    h = _norm_mod(x_ref[...], nw_ref[...], sc_ref[...], sh_ref[...])
    h_ref[...] = h.astype(bf16)
    h_hi, h_lo = _split2(h)
    w_hi, w_lo = _split2(wrt_ref[...])
    lg = _nt(w_hi, h_hi) + _nt(w_hi, h_lo) + _nt(w_lo, h_hi) + br_ref[...]
    e = jnp.exp(lg - jnp.max(lg, axis=0, keepdims=True))
    best = None
    for g in range(N_EXPERTS // EXPERTS_PER_GROUP):
        v = [e[EXPERTS_PER_GROUP * g + j:EXPERTS_PER_GROUP * g + j + 1, :] for j in range(EXPERTS_PER_GROUP)]
        t1 = jnp.maximum(jnp.maximum(v[0], v[1]), jnp.maximum(v[2], v[3]))
        i1 = jnp.where(v[0] == t1, 0, jnp.where(v[1] == t1, 1, jnp.where(v[2] == t1, 2, 3)))
        w = [jnp.where(i1 == j, -1.0, v[j]) for j in range(EXPERTS_PER_GROUP)]
        t2 = jnp.maximum(jnp.maximum(w[0], w[1]), jnp.maximum(w[2], w[3]))
        i2 = jnp.where(w[0] == t2, 0, jnp.where(w[1] == t2, 1, jnp.where(w[2] == t2, 2, 3)))
        cand = (t1 + t2, t1, t2, i1 + EXPERTS_PER_GROUP * g, i2 + EXPERTS_PER_GROUP * g)
        if best is None:
            best = cand
        else:
            upd = cand[0] > best[0]
            best = tuple(jnp.where(upd, c, b) for c, b in zip(cand, best))
    s, t1, t2, i1, i2 = best
    ids_ref[0:1, :] = i1.astype(i32)
    ids_ref[1:2, :] = i2.astype(i32)
    gates_ref[0:1, :] = t1 / s
    gates_ref[1:2, :] = t2 / s


def _moe_body(te_ref, nu_ref, xs_ref, wgu_ref, wdn_ref, o_ref):
    i = pl.program_id(0)
    f = wdn_ref.shape[0]

    @pl.when(i < nu_ref[0])
    def _():
        gu = _mm(xs_ref[...], wgu_ref[...])
        g = gu[:, :f]
        act = g * _sigmoid(g) * gu[:, f:]
        o_ref[...] = _mm(act.astype(bf16), wdn_ref[...]).astype(bf16)

    @pl.when(i >= nu_ref[0])
    def _():
        o_ref[...] = jnp.zeros_like(o_ref)


def _moe_grouped(xs, wgu, wdn, tile_expert, n_used):
    p, d = xs.shape
    ne, _, f2 = wgu.shape
    f = wdn.shape[1]
    n_tiles = p // TM_MOE
    gs = pltpu.PrefetchScalarGridSpec(
        num_scalar_prefetch=2, grid=(n_tiles,),
        in_specs=[pl.BlockSpec((TM_MOE, d), lambda i, te, nu: (i, 0)),
                  pl.BlockSpec((None, d, f2), lambda i, te, nu: (te[i], 0, 0)),
                  pl.BlockSpec((None, f, d), lambda i, te, nu: (te[i], 0, 0))],
        out_specs=pl.BlockSpec((TM_MOE, d), lambda i, te, nu: (i, 0)))
    return pl.pallas_call(_moe_body, grid_spec=gs, out_shape=jax.ShapeDtypeStruct((p, d), bf16),
                          compiler_params=_cparams("arbitrary"), name="moe_experts")(
                              tile_expert, n_used, xs, wgu, wdn)


def _combine_body(x_ref, oa_ref, ob_ref, gt_ref, g2_ref, o_ref):
    gt = gt_ref[...]
    y = gt[:, 0:1] * oa_ref[...].astype(f32) + gt[:, 1:2] * ob_ref[...].astype(f32)
    o_ref[...] = x_ref[...] + g2_ref[...] * y


def _dispatch(ids, n_tiles):
    n_tok = ids.shape[1]
    eflat = ids.reshape(-1)
    n_slot = eflat.shape[0]
    order = jnp.argsort(eflat, stable=True).astype(i32)
    onehot = (eflat[:, None] == jnp.arange(N_EXPERTS, dtype=i32)[None, :]).astype(i32)
    counts = jnp.sum(onehot, axis=0)
    rank = jnp.sum(onehot * jnp.cumsum(onehot, axis=0), axis=1) - 1
    padded = (counts + TM_MOE - 1) // TM_MOE * TM_MOE
    pend = jnp.cumsum(padded)
    pstart = pend - padded
    cstart = jnp.cumsum(counts) - counts
    n_used = (pend[-1] // TM_MOE).astype(i32).reshape(1)
    tile_expert = jnp.minimum(
        jnp.searchsorted(pend, jnp.arange(n_tiles, dtype=i32) * TM_MOE, side='right'), N_EXPERTS - 1).astype(i32)
    r = jnp.arange(n_tiles * TM_MOE, dtype=i32)
    e_r = jnp.repeat(tile_expert, TM_MOE)
    j = r - pstart[e_r]
    valid = j < counts[e_r]
    slot = order[jnp.clip(cstart[e_r] + j, 0, n_slot - 1)]
    row_src = jnp.where(valid, slot % n_tok, 0).astype(i32)
    dest = (pstart[eflat] + rank).astype(i32).reshape(2, n_tok)
    return row_src, dest, tile_expert, n_used


def _moe_layer(x_all, dims, l, W):
    n_all, d = x_all.shape
    h2, ids, gates = _both_groups(
        _router_body, dims, row_ins=[x_all], mod_ks=(3, 4),
        consts=[W['norm_ff'][l].reshape(1, d), W['w_router_t'], W['b_router_c']],
        outs=[((n_all, d), bf16, 'row'), ((2, n_all), i32, 'col'), ((2, n_all), f32, 'col')],
        name="router")
    n_tiles = (2 * n_all + N_EXPERTS * (TM_MOE - 1)) // TM_MOE + 1
    row_src, dest, tile_expert, n_used = _dispatch(ids, n_tiles)
    xs = jnp.take(h2, row_src, axis=0)
    osort = _moe_grouped(xs, W['w_gate_up_b'][l], W['w_down_b'][l], tile_expert, n_used)
    oa = jnp.take(osort, dest[0], axis=0)
    ob = jnp.take(osort, dest[1], axis=0)
    (x_new,) = _both_groups(
        _combine_body, dims, row_ins=[x_all, oa, ob, gates.T], mod_ks=(5,), consts=[],
        outs=[((n_all, d), f32, 'row')], name="combine")
    return x_new


def _pool_tail(pooled_slabs, w_ref, b_ref, ls_ref):
    ys = [_mm(p.astype(bf16), w_ref[g]) for g, p in enumerate(pooled_slabs)]
    return (jnp.concatenate(ys, axis=1) + b_ref[...]) * ls_ref[...]


def _pool_prompt_body(bps, x_ref, xh_ref, sh_ref, sc_ref, g1_ref, nw_ref, w_ref, b_ref, ls_ref,
                      o_ref, st_ref, ext_ref):
    tb = x_ref.shape[0]
    gw = w_ref.shape[1]
    blk = pl.program_id(0) % bps
    x = x_ref[...]
    nw, sc, sh = nw_ref[...], sc_ref[...], sh_ref[...]
    h = _norm_mod(x, nw, sc, sh)
    hh = _norm_mod(xh_ref[...], nw, sc, sh)
    ext_ref[0:HALO, :] = jnp.where(blk == 0, 0.0, hh)
    ext_ref[HALO:, :] = h
    pos = blk * tb + lax.broadcasted_iota(i32, (tb, 1), 0)
    slabs = []
    for g, wdw in enumerate(POOL_WINDOWS):
        lo, hi = g * gw, (g + 1) * gw
        acc = h[:, lo:hi]
        for j in range(1, wdw):
            acc = acc + ext_ref[HALO - j:HALO - j + tb, lo:hi]
        cnt = jnp.minimum(pos + 1, wdw).astype(f32)
        slabs.append(acc / cnt - h[:, lo:hi])
    y = _pool_tail(slabs, w_ref, b_ref, ls_ref)
    o_ref[...] = x + g1_ref[...] * y
    st_ref[...] = h[tb - HALO:, :]


def _pool_sample_body(n_t, pos0, x_ref, past_ref, sh_ref, sc_ref, g1_ref, nw_ref, w_ref, b_ref, ls_ref,
                      o_ref, h_ref):
    nb = past_ref.shape[1]
    gw = w_ref.shape[1]
    x = x_ref[...]
    h = _norm_mod(x, nw_ref[...], sc_ref[...], sh_ref[...])
    h_ref[...] = h

    def ext(r, lo, hi):
        if r < POOL_HIST:
            return past_ref[r][:, lo:hi]
        t = r - POOL_HIST
        return h[t * nb:(t + 1) * nb, lo:hi]

    slabs = []
    for g, wdw in enumerate(POOL_WINDOWS):
        lo, hi = g * gw, (g + 1) * gw
        rows = []
        for t in range(n_t):
            acc = ext(POOL_HIST + t, lo, hi)
            for j in range(1, wdw):
                acc = acc + ext(POOL_HIST + t - j, lo, hi)
            cnt = float(min(pos0 + t + 1, wdw))
            rows.append(acc / cnt - ext(POOL_HIST + t, lo, hi))
        slabs.append(jnp.concatenate(rows, axis=0))
    y = _pool_tail(slabs, w_ref, b_ref, ls_ref)
    o_ref[...] = x + g1_ref[...] * y


def _pool_layer(x_all, dims, l, ip, W, state_pool):
    n_all, d = x_all.shape
    n_p, n_s, seq, nb, n_t, batch = (dims[k] for k in ('n_p', 'n_s', 'seq', 'nb', 'n_t', 'batch'))
    modp, mods = dims['modp'], dims['mods']
    bps = seq // TB_POOL
    consts = [W['norm_mix'][l].reshape(1, d), W['w_pool'][ip].astype(bf16),
              W['b_pool'][ip].reshape(1, d), W['ls_pool'][ip].reshape(1, d)]

    def mspec(k):
        return pl.BlockSpec((None, None, 1, d), lambda i, _k=k: (_k, i // bps, 0, 0))

    ratio = TB_POOL // HALO
    x_new, st = pl.pallas_call(
        functools.partial(_pool_prompt_body, bps), grid=(n_p // TB_POOL,),
        in_specs=[pl.BlockSpec((TB_POOL, d), lambda i: (i, 0)),
                  pl.BlockSpec((HALO, d), lambda i: (jnp.maximum(i * ratio - 1, 0), 0)),
                  mspec(0), mspec(1), mspec(2)] + [_full_spec(c) for c in consts],
        out_specs=[pl.BlockSpec((TB_POOL, d), lambda i: (i, 0)),
                   pl.BlockSpec((None, HALO, d), lambda i: (i // bps, 0, 0))],
        out_shape=[jax.ShapeDtypeStruct((n_all, d), f32), jax.ShapeDtypeStruct((batch, HALO, d), f32)],
        scratch_shapes=[pltpu.VMEM((TB_POOL + HALO, d), f32)],
        compiler_params=_cparams("arbitrary"), name="pool_p")(
            x_all, x_all, modp, modp, modp, *consts)
    pool_prompt = st[:, HALO - POOL_HIST:, :]

    past = jnp.transpose(state_pool[ip], (1, 0, 2))
    off = n_p // n_s

    def sspec(k):
        return pl.BlockSpec((None, None, n_s, d), lambda i, _k=k: (_k, 0, 0, 0))

    def body(x_ref, past_ref, sh, sc, g1, nw, w, b, ls, prev_ref, o_ref, h_ref):
        _pool_sample_body(n_t, dims['past_len'], x_ref, past_ref, sh, sc, g1, nw, w, b, ls, o_ref, h_ref)

    x_new, h_s = pl.pallas_call(
        body, grid=(1,),
        in_specs=[pl.BlockSpec((n_s, d), lambda i: (off, 0)), _full_spec(past),
                  sspec(0), sspec(1), sspec(2)] + [_full_spec(c) for c in consts]
                 + [pl.BlockSpec(memory_space=pl.ANY)],
        out_specs=[pl.BlockSpec((n_s, d), lambda i: (off, 0)), pl.BlockSpec((n_s, d), lambda i: (0, 0))],
        out_shape=[jax.ShapeDtypeStruct((n_all, d), f32), jax.ShapeDtypeStruct((n_s, d), f32)],
        input_output_aliases={5 + len(consts): 0},
        compiler_params=_cparams("arbitrary"), name="pool_s")(
            x_all, past, mods, mods, mods, *consts, x_new)
    ext = jnp.concatenate([past, h_s.reshape(n_t, nb, d)], axis=0)
    pool_sample = jnp.transpose(ext[-POOL_HIST:], (1, 0, 2))
    return x_new, pool_prompt, pool_sample


def _ssm_param_body(lr_ref, li_ref, ldt_ref, apr_ref, api_ref, kr_ref, ki_ref):
    lr, li = lr_ref[...], li_ref[...]
    dt = jnp.exp(ldt_ref[...])
    mag = jnp.exp(lr * dt)
    a_r, a_i = mag * jnp.cos(li * dt), mag * jnp.sin(li * dt)
    den = lr * lr + li * li
    nr, ni = a_r - 1.0, a_i
    kr_ref[...] = (nr * lr + ni * li) / den
    ki_ref[...] = (ni * lr - nr * li) / den
    pr, pi_ = a_r, a_i
    for k in range(8):
        apr_ref[k] = pr
        api_ref[k] = pi_
        pr, pi_ = pr * a_r - pi_ * a_i, pr * a_i + pi_ * a_r


def _ssm_tables(W, iq):
    g, p = W['ssm_lambda_re'][iq].shape
    c = SSM_GROUP_CH
    ns = g // SLAB_GROUPS
    sl = SLAB_GROUPS
    shp = [jax.ShapeDtypeStruct((8, g, p), f32)] * 2 + [jax.ShapeDtypeStruct((g, p), f32)] * 2
    apr, api, k_r, k_i = pl.pallas_call(_ssm_param_body, out_shape=shp, name="ssm_params")(
        W['ssm_lambda_re'][iq], W['ssm_lambda_im'][iq], W['ssm_log_dt'][iq].reshape(g, 1))
    b_re, b_im = W['ssm_b_re'][iq], W['ssm_b_im'][iq]
    bb_re = k_r[..., None] * b_re - k_i[..., None] * b_im
    bb_im = k_r[..., None] * b_im + k_i[..., None] * b_re
    eye = jnp.eye(sl, dtype=f32)

    def b_slab(bb):
        b4 = bb.reshape(ns, sl, p, c)
        return jnp.einsum('sgpc,gh->sgchp', b4, eye).reshape(ns, sl * c, sl * p)

    def c_slab(cc):
        c4 = cc.reshape(ns, sl, c, p)
        return jnp.einsum('sgcp,gh->shpgc', c4, eye).reshape(ns, sl * p, sl * c)

    bmat = jnp.concatenate([b_slab(bb_re), b_slab(bb_im)], axis=2).astype(bf16)
    cmat = jnp.concatenate([c_slab(W['ssm_c_re'][iq]), -c_slab(W['ssm_c_im'][iq])], axis=1).astype(bf16)

    def lay(a):
        return jnp.transpose(a.reshape(8, ns, sl * p), (1, 0, 2))

    apw = jnp.stack([lay(apr), lay(api)], axis=1)
    rows = jnp.arange(8)[:, None]
    t1 = jnp.stack([jnp.where(rows >= dd, apw[:, :, dd - 1:dd, :], 0.0) for dd in (1, 2, 4)], axis=1)
    return bmat, cmat, t1, apw


def _ssm_out(x, h, y, g1_ref, dsk_ref, wglu_ref, bglu_ref):
    d = x.shape[1]
    z = _gelu_tanh(y + dsk_ref[...] * h)
    gu = _mm(z.astype(bf16), wglu_ref[...]) + bglu_ref[...]
    return x + g1_ref[...] * (gu[:, :d] * _sigmoid(gu[:, d:]))


def _ssm_prompt_body(x_ref, sh_ref, sc_ref, g1_ref, nw_ref, b_ref, c_ref, t1_ref, t2_ref, dsk_ref,
                     wglu_ref, bglu_ref, o_ref, st_ref, xs_ref, hs_ref, carry_ref):
    t = x_ref.shape[0]
    ns = b_ref.shape[0]
    sw = b_ref.shape[1]
    half = b_ref.shape[2] // 2

    @pl.when(pl.program_id(1) == 0)
    def _():
        carry_ref[...] = jnp.zeros_like(carry_ref)

    x = x_ref[...]
    h = _norm_mod(x, nw_ref[...], sc_ref[...], sh_ref[...])
    hb = h.astype(bf16)
    ys = []
    for s in range(ns):
        xs_ref[...] = _mm(hb[:, s * sw:(s + 1) * sw], b_ref[s])

        def grp(j, car, s=s):
            cr, ci = car
            r0 = pl.multiple_of(j * 8, 8)
            xr = xs_ref[pl.ds(r0, 8), 0:half]
            xi = xs_ref[pl.ds(r0, 8), half:]
            for di, dd in enumerate((1, 2, 4)):
                sr = pltpu.roll(xr, dd, 0)
                si = pltpu.roll(xi, dd, 0)
                ar = t1_ref[s, di, 0]
                ai = t1_ref[s, di, 1]
                xr, xi = xr + ar * sr - ai * si, xi + ar * si + ai * sr
            pr = t2_ref[s, 0]
            pi_ = t2_ref[s, 1]
            hr = xr + pr * cr - pi_ * ci
            hi = xi + pr * ci + pi_ * cr
            hs_ref[pl.ds(r0, 8), 0:half] = hr
            hs_ref[pl.ds(r0, 8), half:] = hi
            return (jnp.broadcast_to(hr[7:8, :], hr.shape), jnp.broadcast_to(hi[7:8, :], hi.shape))

        cr, ci = lax.fori_loop(0, t // 8, grp, (carry_ref[s, :, 0:half], carry_ref[s, :, half:]))
        carry_ref[s, :, 0:half] = cr
        carry_ref[s, :, half:] = ci
        ys.append(_mm(hs_ref[...].astype(bf16), c_ref[s]))
        st_ref[s:s + 1, :] = jnp.concatenate([cr[0:1, :], ci[0:1, :]], axis=1)
    y = jnp.concatenate(ys, axis=1)
    o_ref[...] = _ssm_out(x, h, y, g1_ref, dsk_ref, wglu_ref, bglu_ref)


def _ssm_sample_body(n_t, x_ref, s0_ref, sh_ref, sc_ref, g1_ref, nw_ref, b_ref, c_ref, t2_ref, dsk_ref,
                     wglu_ref, bglu_ref, prev_ref, o_ref, so_ref, y_ref):
    ns = b_ref.shape[0]
    sw = b_ref.shape[1]
    half = b_ref.shape[2] // 2
    nb = s0_ref.shape[1]
    x = x_ref[...]
    h = _norm_mod(x, nw_ref[...], sc_ref[...], sh_ref[...])
    hb = h.astype(bf16)
    for s in range(ns):
        s_r = s0_ref[s, :, 0:half]
        s_i = s0_ref[s, :, half:]
        ar = t2_ref[s, 0, 0:1, :]
        ai = t2_ref[s, 1, 0:1, :]
        for t in range(n_t):
            xx = _mm(hb[t * nb:(t + 1) * nb, s * sw:(s + 1) * sw], b_ref[s])
            s_r, s_i = ar * s_r - ai * s_i + xx[:, 0:half], ar * s_i + ai * s_r + xx[:, half:]
            st = jnp.concatenate([s_r, s_i], axis=1)
            y_ref[t * nb:(t + 1) * nb, s * sw:(s + 1) * sw] = _mm(st.astype(bf16), c_ref[s])
        so_ref[s] = jnp.concatenate([s_r, s_i], axis=1)
    o_ref[...] = _ssm_out(x, h, y_ref[...], g1_ref, dsk_ref, wglu_ref, bglu_ref)


def _ssm_layer(x_all, dims, l, iq, W, st_re, st_im):
    n_all, d = x_all.shape
    n_p, n_s, seq, nb, n_t, batch = (dims[k] for k in ('n_p', 'n_s', 'seq', 'nb', 'n_t', 'batch'))
    modp, mods = dims['modp'], dims['mods']
    bmat, cmat, t1, t2 = _ssm_tables(W, iq)
    ns = bmat.shape[0]
    sp = bmat.shape[2]
    g, p = W['ssm_lambda_re'][iq].shape
    sl = SLAB_GROUPS
    tail = [W['ssm_d'][iq].reshape(1, d), W['w_glu'][iq].astype(bf16), W['b_glu'][iq].reshape(1, 2 * d)]
    nw = W['norm_mix'][l].reshape(1, d)
    nchunk = seq // T_SSM

    def mspec(k):
        return pl.BlockSpec((None, None, 1, d), lambda b, c, _k=k: (_k, b, 0, 0))

    def cspec(a):
        return pl.BlockSpec(a.shape, lambda b, c, _n=a.ndim: (0,) * _n)

    consts = [nw, bmat, cmat, t1, t2] + tail
    x_new, st = pl.pallas_call(
        _ssm_prompt_body, grid=(batch, nchunk),
        in_specs=[pl.BlockSpec((T_SSM, d), lambda b, c: (b * nchunk + c, 0)), mspec(0), mspec(1), mspec(2)]
                 + [cspec(a) for a in consts],
        out_specs=[pl.BlockSpec((T_SSM, d), lambda b, c: (b * nchunk + c, 0)),
                   pl.BlockSpec((None, ns, sp), lambda b, c: (b, 0, 0))],
        out_shape=[jax.ShapeDtypeStruct((n_all, d), f32), jax.ShapeDtypeStruct((batch, ns, sp), f32)],
        scratch_shapes=[pltpu.VMEM((T_SSM, sp), f32), pltpu.VMEM((T_SSM, sp), f32),
                        pltpu.VMEM((ns, 8, sp), f32)],
        compiler_params=_cparams("arbitrary", "arbitrary"), name="ssm_p")(
            x_all, modp, modp, modp, *consts)

    def unslab(a):
        a5 = a.reshape(a.shape[0], ns, 2, sl, p)
        return a5[:, :, 0].reshape(-1, g, p), a5[:, :, 1].reshape(-1, g, p)

    re_p, im_p = unslab(st)

    def slab(a):
        return a.reshape(a.shape[0], ns, sl * p)

    s0 = jnp.transpose(jnp.concatenate([slab(st_re), slab(st_im)], axis=2), (1, 0, 2))
    off = n_p // n_s

    def sspec(k):
        return pl.BlockSpec((None, None, n_s, d), lambda i, _k=k: (_k, 0, 0, 0))

    consts_s = [nw, bmat, cmat, t2] + tail
    x_new, so = pl.pallas_call(
        functools.partial(_ssm_sample_body, n_t), grid=(1,),
        in_specs=[pl.BlockSpec((n_s, d), lambda i: (off, 0)), _full_spec(s0), sspec(0), sspec(1), sspec(2)]
                 + [_full_spec(a) for a in consts_s] + [pl.BlockSpec(memory_space=pl.ANY)],
        out_specs=[pl.BlockSpec((n_s, d), lambda i: (off, 0)), _full_spec(s0)],
        out_shape=[jax.ShapeDtypeStruct((n_all, d), f32), jax.ShapeDtypeStruct(s0.shape, f32)],
        scratch_shapes=[pltpu.VMEM((n_s, d), f32)],
        input_output_aliases={5 + len(consts_s): 0},
        compiler_params=_cparams("arbitrary"), name="ssm_s")(
            x_all, s0, mods, mods, mods, *consts_s, x_new)
    re_s, im_s = unslab(jnp.transpose(so, (1, 0, 2)))
    return x_new, re_p, im_p, re_s, im_s


def _qkv_body(x_ref, sh_ref, sc_ref, cos_ref, sin_ref, nw_ref, w_ref, qn_ref, kn_ref, seg_ref, segt_ref,
              q_ref, k_ref, v_ref, kb_ref, vb_ref):
    d = x_ref.shape[1]
    h = _norm_mod(x_ref[...], nw_ref[...], sc_ref[...], sh_ref[...])
    qkv = _mm(h.astype(bf16), w_ref[...])
    rep = d // LANES
    cos = jnp.concatenate([cos_ref[...]] * rep, axis=1)
    sin = jnp.concatenate([sin_ref[...]] * rep, axis=1)
    lane_lo = (lax.broadcasted_iota(i32, (1, d), 1) % HEAD_DIM) < (HEAD_DIM // 2)
    seg, segt = seg_ref[...], segt_ref[...]

    def norm_rope(t, g):
        s_hi, s_lo = _split2(t * t)
        ms = (_mm(s_hi, seg) + _mm(s_lo, seg)) * (1.0 / HEAD_DIM)
        r_hi, r_mid, r_lo = _split3(lax.rsqrt(ms + NORM_EPS))
        rf = _mm(r_hi, segt) + _mm(r_mid, segt) + _mm(r_lo, segt)
        tn = t * rf * g
        sw = jnp.where(lane_lo, pltpu.roll(tn, d - HEAD_DIM // 2, 1), pltpu.roll(tn, HEAD_DIM // 2, 1))
        return tn * cos + sw * sin

    q = norm_rope(qkv[:, :d], qn_ref[...])
    k = norm_rope(qkv[:, d:2 * d], kn_ref[...])
    v = qkv[:, 2 * d:]
    q_ref[...] = (q * (HEAD_DIM ** -0.5)).astype(bf16)
    k_ref[...] = k
    v_ref[...] = v
    kb_ref[...] = k.astype(bf16)
    vb_ref[...] = v.astype(bf16)


def _lam(lam_ref):
    l4 = lam_ref[...]
    a = jnp.sum(l4[0:1, :] * l4[1:2, :], axis=1, keepdims=True)
    b = jnp.sum(l4[2:3, :] * l4[3:4, :], axis=1, keepdims=True)
    return jnp.exp(a) - jnp.exp(b)


def _attn_finish(acc, l, lam, lam_init, sub_ref):
    r = acc.shape[0] // 2
    o = acc[:r] / l[:r] - lam * (acc[r:] / l[r:])
    o = o * lax.rsqrt(jnp.mean(o * o, axis=-1, keepdims=True) + NORM_EPS) * sub_ref[...]
    return o * (1.0 - lam_init)


def _flash_body(lam_init, q_ref, k_ref, v_ref, lam_ref, sub_ref, o_ref):
    tq = q_ref.shape[0]
    qi = pl.program_id(2)
    q = q_ref[...]
    lo = lax.broadcasted_iota(i32, (1, q.shape[1]), 1) < HEAD_DIM
    zero = jnp.zeros_like(q)
    qq = jnp.concatenate([jnp.where(lo, q, zero), jnp.where(lo, zero, q)], axis=0)

    def step(j, carry, masked):
        m, l, acc = carry
        r0 = pl.multiple_of(j * tq, tq)
        s = _nt(qq, k_ref[pl.ds(r0, tq), :])
        if masked:
            row = lax.broadcasted_iota(i32, s.shape, 0) % tq
            col = lax.broadcasted_iota(i32, s.shape, 1)
            s = jnp.where(col <= row, s, NEG_INF)
        m_new = jnp.maximum(m, jnp.max(s, axis=1, keepdims=True))
        alpha = jnp.exp(m - m_new)
        p = jnp.exp(s - m_new)
        l = alpha * l + jnp.sum(p, axis=1, keepdims=True)
        acc = alpha * acc + _mm(p.astype(bf16), v_ref[pl.ds(r0, tq), :])
        return m_new, l, acc

    init = (jnp.full((2 * tq, 1), NEG_INF, f32), jnp.zeros((2 * tq, 1), f32),
            jnp.zeros((2 * tq, q.shape[1]), f32))
    carry = lax.fori_loop(0, qi, lambda j, c: step(j, c, False), init)
    m, l, acc = step(qi, carry, True)
    lam = _lam(lam_ref) + lam_init
    o_ref[...] = _attn_finish(acc, l, lam, lam_init, sub_ref).astype(o_ref.dtype)


def _attn_sample_body(lam_init, pp, pt_ref, q_ref, kn_ref, vn_ref, mp_ref, mn_ref, lam_ref, sub_ref, *rest):
    k_refs, v_refs = rest[:pp], rest[pp:2 * pp]
    o_ref = rest[2 * pp]
    m_ref, l_ref, acc_ref = rest[2 * pp + 1:]
    j = pl.program_id(1)
    q = q_ref[...]

    @pl.when(j == 0)
    def _():
        m_ref[...] = jnp.full_like(m_ref, NEG_INF)
        l_ref[...] = jnp.zeros_like(l_ref)
        acc_ref[...] = jnp.zeros_like(acc_ref)

    def update(kf, vf, mask):
        s = _nt(q, kf) + mask
        m = m_ref[...]
        m_new = jnp.maximum(m, jnp.max(s, axis=1, keepdims=True))
        alpha = jnp.exp(m - m_new)
        p = jnp.exp(s - m_new)
        l_ref[...] = alpha * l_ref[...] + jnp.sum(p, axis=1, keepdims=True)
        acc_ref[...] = alpha * acc_ref[...] + _mm(p.astype(bf16), vf)
        m_ref[...] = m_new

    for u in range(pp):
        update(k_refs[u][...].astype(bf16), v_refs[u][...].astype(bf16), mp_ref[...])

    @pl.when(j == pl.num_programs(1) - 1)
    def _():
        update(kn_ref[...], vn_ref[...], mn_ref[...])
        lam = _lam(lam_ref) + lam_init
        o_ref[...] = _attn_finish(acc_ref[...], l_ref[...], lam, lam_init, sub_ref)


def _oproj_body(x_ref, o_ref_in, g1_ref, w_ref, o_ref):
    o_ref[...] = x_ref[...] + g1_ref[...] * _mm(o_ref_in[...], w_ref[...])


def _attn_layer(x_all, dims, l, ia, W, cache_k, cache_v, page_table):
    n_all, d = x_all.shape
    n_p, n_s, seq, nb, n_t, batch = (dims[k] for k in ('n_p', 'n_s', 'seq', 'nb', 'n_t', 'batch'))
    past_len = dims['past_len']
    hd2 = 2 * HEAD_DIM
    lam_init = 0.8 - 0.6 * math.exp(-0.3 * l)

    half = HEAD_DIM // 2
    inv = jnp.power(ROPE_THETA, -jnp.arange(half, dtype=f32) * (2.0 / HEAD_DIM))

    def tables(pos):
        ang = pos.astype(f32)[:, None] * inv[None, :]
        cos, sin = jnp.cos(ang), jnp.sin(ang)
        return (jnp.concatenate([cos] * 4, axis=1), jnp.concatenate([-sin, sin, -sin, sin], axis=1))

    cos_p, sin_p = tables(jnp.arange(seq))
    cos_s, sin_s = tables(past_len + jnp.repeat(jnp.arange(n_t), nb))
    seg = (jnp.arange(d)[:, None] // HEAD_DIM == jnp.arange(d // HEAD_DIM)[None, :]).astype(bf16)
    consts = [W['norm_mix'][l].reshape(1, d), W['w_qkv'][ia].astype(bf16),
              jnp.tile(W['q_norm'][ia], d // HEAD_DIM).reshape(1, d),
              jnp.tile(W['k_norm'][ia], d // HEAD_DIM).reshape(1, d), seg, seg.T]
    qs, k_all, v_all, kb, vb = _both_groups(
        _qkv_body, dims, row_ins=[x_all], mod_ks=(0, 1), consts=consts,
        seq_ins_p=(cos_p, sin_p), seq_ins_s=(cos_s, sin_s),
        outs=[((n_all, d), bf16, 'row'), ((n_all, d), f32, 'row'), ((n_all, d), f32, 'row'),
              ((n_all, d), bf16, 'row'), ((n_all, d), bf16, 'row')], name="qkv")

    lam4 = jnp.stack([W['lambda_q1'][ia], W['lambda_k1'][ia], W['lambda_q2'][ia], W['lambda_k2'][ia]])
    sub = W['subln'][ia].reshape(1, hd2)
    nq = seq // TQ_ATTN
    o_all = pl.pallas_call(
        functools.partial(_flash_body, lam_init), grid=(batch, N_HEADS, nq),
        in_specs=[pl.BlockSpec((TQ_ATTN, hd2), lambda b, h, i: (b * nq + i, h)),
                  pl.BlockSpec((seq, hd2), lambda b, h, i: (b, h)),
                  pl.BlockSpec((seq, hd2), lambda b, h, i: (b, h)),
                  pl.BlockSpec(lam4.shape, lambda b, h, i: (0, 0)),
                  pl.BlockSpec(sub.shape, lambda b, h, i: (0, 0))],
        out_specs=pl.BlockSpec((TQ_ATTN, hd2), lambda b, h, i: (b * nq + i, h)),
        out_shape=jax.ShapeDtypeStruct((n_all, d), bf16),
        compiler_params=_cparams("arbitrary", "arbitrary", "arbitrary"), name="flash_p")(
            qs, kb, vb, lam4, sub)

    na, n_phys, page, _, _ = cache_k.shape
    flat = page * N_HEADS
    ck = cache_k.reshape(na * n_phys, flat, hd2)
    cv = cache_v.reshape(na * n_phys, flat, hd2)
    n_pages = page_table.shape[1]
    pp = PAGES_PER_STEP if n_pages % PAGES_PER_STEP == 0 else 1
    rows = 2 * N_HEADS * n_t
    q5 = jnp.transpose(qs[n_p:].reshape(n_t, nb, N_HEADS, 2, HEAD_DIM), (1, 3, 2, 0, 4))
    z = jnp.zeros_like(q5[:, 0])
    qm = jnp.stack([jnp.concatenate([q5[:, 0], z], -1), jnp.concatenate([z, q5[:, 1]], -1)], axis=1)
    qm = qm.reshape(nb, rows, hd2)

    def new_rows(a):
        return jnp.transpose(a[n_p:].reshape(n_t, nb, N_HEADS, hd2), (1, 0, 2, 3)).reshape(nb, n_t * N_HEADS, hd2)

    kn, vn = new_rows(kb), new_rows(vb)
    r_h = (jnp.arange(rows) % (N_HEADS * n_t)) // n_t
    r_t = jnp.arange(rows) % n_t
    mask_p = jnp.where(r_h[:, None] == (jnp.arange(flat) % N_HEADS)[None, :], 0.0, NEG_INF).astype(f32)
    cn = jnp.arange(n_t * N_HEADS)
    mask_n = jnp.where((r_h[:, None] == (cn % N_HEADS)[None, :]) & ((cn // N_HEADS)[None, :] <= r_t[:, None]),
                       0.0, NEG_INF).astype(f32)
    base = ia * n_phys

    def seq_spec(shape):
        return pl.BlockSpec((None,) + shape, lambda s, j, pt: (s, 0, 0))

    def cst_spec(a):
        return pl.BlockSpec(a.shape, lambda s, j, pt: (0, 0))

    def page_spec(u):
        return pl.BlockSpec((None, flat, hd2), lambda s, j, pt, _u=u: (base + pt[s, j * pp + _u], 0, 0))

    gs = pltpu.PrefetchScalarGridSpec(
        num_scalar_prefetch=1, grid=(nb, n_pages // pp),
        in_specs=[seq_spec((rows, hd2)), seq_spec((n_t * N_HEADS, hd2)), seq_spec((n_t * N_HEADS, hd2)),
                  cst_spec(mask_p), cst_spec(mask_n), cst_spec(lam4), cst_spec(sub)]
                 + [page_spec(u) for u in range(pp)] + [page_spec(u) for u in range(pp)],
        out_specs=pl.BlockSpec((None, rows // 2, hd2), lambda s, j, pt: (s, 0, 0)),
        scratch_shapes=[pltpu.VMEM((rows, 1), f32), pltpu.VMEM((rows, 1), f32), pltpu.VMEM((rows, hd2), f32)])
    o_s = pl.pallas_call(
        functools.partial(_attn_sample_body, lam_init, pp), grid_spec=gs,
        out_shape=jax.ShapeDtypeStruct((nb, rows // 2, hd2), f32),
        compiler_params=_cparams("arbitrary", "arbitrary"), name="attn_s")(
            page_table, qm, kn, vn, mask_p, mask_n, lam4, sub, *([ck] * pp), *([cv] * pp))
    o_s = jnp.transpose(o_s.reshape(nb, N_HEADS, n_t, hd2), (2, 0, 1, 3)).reshape(n_s, d)
    o_all = lax.dynamic_update_slice(o_all, o_s.astype(bf16), (n_p, 0))

    (x_new,) = _both_groups(
        _oproj_body, dims, row_ins=[x_all, o_all], mod_ks=(2,), consts=[W['w_o'][ia].astype(bf16)],
        outs=[((n_all, d), f32, 'row')], name="oproj")

    def split(a):
        ap = a[:n_p].reshape(batch, seq, N_HEADS, hd2)
        as_ = jnp.transpose(a[n_p:].reshape(n_t, nb, N_HEADS, hd2), (1, 0, 2, 3))
        return ap, as_

    k_p, k_s = split(k_all)
    v_p, v_s = split(v_all)
    return x_new, k_p, v_p, k_s, v_s


def kernel(x_prompt, x_sample, cache_k, cache_v, state_pool, state_ssm_re, state_ssm_im, page_table, c_prompt, c_sample, w_ada, b_ada, norm_mix, norm_ff, w_pool, b_pool, ls_pool, ssm_lambda_re, ssm_lambda_im, ssm_log_dt, ssm_b_re, ssm_b_im, ssm_c_re, ssm_c_im, ssm_d, w_glu, b_glu, w_qkv, q_norm, k_norm, lambda_q1, lambda_k1, lambda_q2, lambda_k2, subln, w_o, w_router, b_router, w_gate_up, w_down):
    batch, seq, d = x_prompt.shape
    nb, n_t, _ = x_sample.shape
    depth = w_ada.shape[0]
    n_p, n_s = batch * seq, nb * n_t
    past_len = page_table.shape[1] * cache_k.shape[2]
    W = dict(norm_mix=norm_mix, norm_ff=norm_ff, w_pool=w_pool, b_pool=b_pool, ls_pool=ls_pool,
             ssm_lambda_re=ssm_lambda_re, ssm_lambda_im=ssm_lambda_im, ssm_log_dt=ssm_log_dt,
             ssm_b_re=ssm_b_re, ssm_b_im=ssm_b_im, ssm_c_re=ssm_c_re, ssm_c_im=ssm_c_im, ssm_d=ssm_d,
             w_glu=w_glu, b_glu=b_glu, w_qkv=w_qkv, q_norm=q_norm, k_norm=k_norm, lambda_q1=lambda_q1,
             lambda_k1=lambda_k1, lambda_q2=lambda_q2, lambda_k2=lambda_k2, subln=subln, w_o=w_o,
             w_router_t=w_router.T, b_router_c=b_router.reshape(-1, 1),
             w_gate_up_b=w_gate_up.astype(bf16), w_down_b=w_down.astype(bf16))

    ada = _ada_all(jnp.concatenate([c_prompt, c_sample], axis=0), w_ada, b_ada)
    x_all = jnp.concatenate([x_prompt.reshape(n_p, d),
                             jnp.transpose(x_sample, (1, 0, 2)).reshape(n_s, d)], axis=0)
    dims = dict(n_p=n_p, n_s=n_s, seq=seq, nb=nb, n_t=n_t, batch=batch, past_len=past_len)

    pools_p, pools_s, k_ps, v_ps, k_ss, v_ss = [], [], [], [], [], []
    re_ps, im_ps, re_ss, im_ss = [], [], [], []
    ip = iq = ia = 0
    for l in range(depth):
        chunks = jnp.transpose(ada[l].reshape(batch + nb, 6, d), (1, 0, 2))
        dims['modp'] = chunks[:, :batch, None, :]
        dims['mods'] = jnp.tile(chunks[:, batch:], (1, n_t, 1))[:, None]
        kind = l % N_MIXERS
        if kind == 0:
            x_all, pp_, ps_ = _pool_layer(x_all, dims, l, ip, W, state_pool)
            pools_p.append(pp_)
            pools_s.append(ps_)
            ip += 1
        elif kind == 1:
            x_all, rp, imp, rs, ims = _ssm_layer(x_all, dims, l, iq, W, state_ssm_re[iq], state_ssm_im[iq])
            re_ps.append(rp)
            im_ps.append(imp)
            re_ss.append(rs)
            im_ss.append(ims)
            iq += 1
        else:
            x_all, kp, vp, ks, vs = _attn_layer(x_all, dims, l, ia, W, cache_k, cache_v, page_table)
            k_ps.append(kp)
            v_ps.append(vp)
            k_ss.append(ks)
            v_ss.append(vs)
            ia += 1
        x_all = _moe_layer(x_all, dims, l, W)

    y_prompt = x_all[:n_p].reshape(batch, seq, d)
    y_sample = jnp.transpose(x_all[n_p:].reshape(n_t, nb, d), (1, 0, 2))
    return (y_prompt, y_sample, jnp.stack(k_ps), jnp.stack(v_ps), jnp.stack(k_ss), jnp.stack(v_ss),
            jnp.stack(pools_p), jnp.stack(pools_s), jnp.stack(re_ps), jnp.stack(im_ps),
            jnp.stack(re_ss), jnp.stack(im_ss))
```

```python
import functools
import math

import jax
import jax.numpy as jnp
from jax import lax
from jax.experimental import pallas as pl
from jax.experimental.pallas import tpu as pltpu

f32 = jnp.float32
bf16 = jnp.bfloat16
i32 = jnp.int32

N_MIXERS = 3
POOL_WINDOWS = (2, 4, 8, 16)
POOL_HIST = 15
HALO = 16
SSM_GROUP_CH = 16
SSM_STATE = 64
SLAB_GROUPS = 8
N_HEADS = 8
HEAD_DIM = 64
ROPE_THETA = 10000.0
N_EXPERTS = 16
EXPERTS_PER_GROUP = 4
NORM_EPS = 1e-6
NEG_INF = -1e30
LOG2E = 1.4426950408889634
LANES = 128
VMEM_LIMIT = 48 * 1024 * 1024

TB_PROMPT = 512
TB_POOL = 256
T_SSM = 256
TQ_ATTN = 256
HEADS_PER_STEP = 2
TM_MOE = 256
PAGES_PER_STEP = 8


def _cparams(*sem):
    return pltpu.CompilerParams(dimension_semantics=sem, vmem_limit_bytes=VMEM_LIMIT)


def _mm(a, b):
    return jnp.dot(a, b, preferred_element_type=f32)


def _nt(a, b):
    return lax.dot_general(a, b, (((1,), (1,)), ((), ())), preferred_element_type=f32)


def _split2(a):
    hi = a.astype(bf16)
    lo = (a - hi.astype(f32)).astype(bf16)
    return hi, lo


def _split3(a):
    hi = a.astype(bf16)
    r = a - hi.astype(f32)
    mid = r.astype(bf16)
    lo = (r - mid.astype(f32)).astype(bf16)
    return hi, mid, lo


def _norm_mod(x, g, sc, sh):
    ms = jnp.mean(x * x, axis=-1, keepdims=True)
    return x * lax.rsqrt(ms + NORM_EPS) * g * (1.0 + sc) + sh


def _gelu_tanh(y):
    return 0.5 * y * (1.0 + jnp.tanh(0.7978845608028654 * (y + 0.044715 * y * y * y)))


def _sigmoid(x):
    return 1.0 / (1.0 + jnp.exp(-x))


def _full_spec(a):
    nd = a.ndim
    return pl.BlockSpec(a.shape, lambda i, _n=nd: (0,) * _n)


def _rows_call(body, *, tb, nblk, off, bps, row_ins, mod_ins, seq_ins, consts, outs, prev=None,
               scratch=(), name):
    in_specs, args = [], []
    for a in row_ins:
        a, o = a if isinstance(a, tuple) else (a, off)
        in_specs.append(pl.BlockSpec((tb, a.shape[1]), lambda i, _o=o: (i + _o, 0)))
        args.append(a)
    for a, k in mod_ins:
        r, d = a.shape[2], a.shape[3]
        in_specs.append(pl.BlockSpec((None, None, r, d), lambda i, _k=k: (_k, i // bps, 0, 0)))
        args.append(a)
    for a in seq_ins:
        in_specs.append(pl.BlockSpec((tb, a.shape[1]), lambda i: (i % bps, 0)))
        args.append(a)
    for a in consts:
        in_specs.append(_full_spec(a))
        args.append(a)
    n_real = len(args)
    aliases = {}
    if prev is not None:
        for j, a in enumerate(prev):
            in_specs.append(pl.BlockSpec(memory_space=pl.ANY))
            args.append(a)
            aliases[n_real + j] = j
    out_specs, out_shapes = [], []
    for shape, dtype, kind in outs:
        if kind == 'row':
            out_specs.append(pl.BlockSpec((tb, shape[1]), lambda i: (i + off, 0)))
        elif kind == 'own':
            out_specs.append(pl.BlockSpec((tb, shape[1]), lambda i: (i, 0)))
        elif kind == 'col':
            out_specs.append(pl.BlockSpec((shape[0], tb), lambda i: (0, i + off)))
        elif kind == 'blk3':
            out_specs.append(pl.BlockSpec((tb // shape[2], shape[1], shape[2]), lambda i: (i, 0, 0)))
        else:
            out_specs.append(pl.BlockSpec(shape, lambda i, _n=len(shape): (0,) * _n))
        out_shapes.append(jax.ShapeDtypeStruct(shape, dtype))
    n_prev = 0 if prev is None else len(prev)

    def wrapped(*refs):
        body(*refs[:n_real], *refs[n_real + n_prev:])

    res = pl.pallas_call(
        wrapped, grid=(nblk,), in_specs=in_specs, out_specs=out_specs, out_shape=out_shapes,
        scratch_shapes=list(scratch), input_output_aliases=aliases,
        compiler_params=_cparams("arbitrary"), name=name)(*args)
    return list(res)


def _both_groups(body, dims, *, row_ins, mod_ks, consts, outs, name, seq_ins_p=(), seq_ins_s=()):
    modp, mods = dims['modp'], dims['mods']
    n_p, n_s, seq = dims['n_p'], dims['n_s'], dims['seq']
    res = _rows_call(body, tb=TB_PROMPT, nblk=n_p // TB_PROMPT, off=0, bps=seq // TB_PROMPT,
                     row_ins=row_ins, mod_ins=[(modp, k) for k in mod_ks], seq_ins=list(seq_ins_p),
                     consts=consts, outs=outs, name=name + "_p")
    res = _rows_call(body, tb=n_s, nblk=1, off=n_p // n_s, bps=1,
                     row_ins=row_ins, mod_ins=[(mods, k) for k in mod_ks], seq_ins=list(seq_ins_s),
                     consts=consts, outs=outs, prev=res, name=name + "_s")
    return res


def _ada_body(c_ref, w_ref, b_ref, o_ref):
    c = c_ref[...]
    cond = c * _sigmoid(c)
    c_hi, c_lo = _split2(cond)
    w_hi, w_lo = _split2(w_ref[...])
    o_ref[...] = _mm(c_hi, w_hi) + _mm(c_hi, w_lo) + _mm(c_lo, w_hi) + b_ref[...]


def _ada_all(c_all, w_ada, b_ada):
    depth, d, d6 = w_ada.shape
    n = c_all.shape[0]
    nj = d6 // d
    return pl.pallas_call(
        _ada_body, grid=(depth, nj),
        in_specs=[pl.BlockSpec((n, d), lambda l, j: (0, 0)),
                  pl.BlockSpec((None, d, d), lambda l, j: (l, 0, j)),
                  pl.BlockSpec((None, 1, d), lambda l, j: (l, 0, j))],
        out_specs=pl.BlockSpec((None, n, d), lambda l, j: (l, 0, j)),
        out_shape=jax.ShapeDtypeStruct((depth, n, d6), f32),
        compiler_params=_cparams("arbitrary", "arbitrary"), name="ada")(
            c_all, w_ada, b_ada.reshape(depth, 1, d6))


def _router_body(x_ref, sh_ref, sc_ref, nw_ref, wrt_ref, br_ref, tri_ref, cnt0_ref,
                 h_ref, ids_ref, gates_ref, rank_ref, cnt_ref, carry_ref):
    @pl.when(pl.program_id(0) == 0)
    def _():
        carry_ref[...] = cnt0_ref[...]

    h = _norm_mod(x_ref[...], nw_ref[...], sc_ref[...], sh_ref[...])
    h_ref[...] = h.astype(bf16)
    h_hi, h_lo = _split2(h)
    w_hi, w_lo = _split2(wrt_ref[...])
    lg = _nt(w_hi, h_hi) + _nt(w_hi, h_lo) + _nt(w_lo, h_hi) + br_ref[...]
    e = jnp.exp(lg - jnp.max(lg, axis=0, keepdims=True))
    best = None
    for g in range(N_EXPERTS // EXPERTS_PER_GROUP):
        v = [e[EXPERTS_PER_GROUP * g + j:EXPERTS_PER_GROUP * g + j + 1, :] for j in range(EXPERTS_PER_GROUP)]
        t1 = jnp.maximum(jnp.maximum(v[0], v[1]), jnp.maximum(v[2], v[3]))
        i1 = jnp.where(v[0] == t1, 0, jnp.where(v[1] == t1, 1, jnp.where(v[2] == t1, 2, 3)))
        w = [jnp.where(i1 == j, -1.0, v[j]) for j in range(EXPERTS_PER_GROUP)]
        t2 = jnp.maximum(jnp.maximum(w[0], w[1]), jnp.maximum(w[2], w[3]))
        i2 = jnp.where(w[0] == t2, 0, jnp.where(w[1] == t2, 1, jnp.where(w[2] == t2, 2, 3)))
        cand = (t1 + t2, t1, t2, i1 + EXPERTS_PER_GROUP * g, i2 + EXPERTS_PER_GROUP * g)
        if best is None:
            best = cand
        else:
            upd = cand[0] > best[0]
            best = tuple(jnp.where(upd, c, b) for c, b in zip(cand, best))
    s, t1, t2, i1, i2 = best
    ids_ref[0:1, :] = i1.astype(i32)
    ids_ref[1:2, :] = i2.astype(i32)
    gates_ref[0:1, :] = t1 / s
    gates_ref[1:2, :] = t2 / s
    eidx = lax.broadcasted_iota(i32, lg.shape, 0)
    oh1 = (eidx == i1).astype(f32)
    oh2 = (eidx == i2).astype(f32)
    tri = tri_ref[...]
    ex1 = _mm(oh1.astype(bf16), tri)
    ex2 = _mm(oh2.astype(bf16), tri)
    carry = carry_ref[...]
    tot1 = jnp.sum(oh1, axis=1, keepdims=True)
    tot2 = jnp.sum(oh2, axis=1, keepdims=True)
    rank_ref[0:1, :] = jnp.sum(oh1 * (carry + ex1), axis=0, keepdims=True).astype(i32)
    rank_ref[1:2, :] = jnp.sum(oh2 * (carry + tot1 + ex2), axis=0, keepdims=True).astype(i32)
    carry_ref[...] = carry + tot1 + tot2
    cnt_ref[...] = carry_ref[...]


def _invperm_body(ids_ref, rank_ref, pstart_ref, src_ref, dest_ref):
    n_slot = ids_ref.shape[0]
    n_tok = n_slot // 2

    def zero(i, c):
        src_ref[i] = 0
        return c

    lax.fori_loop(0, src_ref.shape[0], zero, 0)

    def place(s, c):
        dst = rank_ref[s] + pstart_ref[ids_ref[s]]
        dest_ref[s] = dst
        src_ref[dst] = jnp.where(s >= n_tok, s - n_tok, s)
        return c

    lax.fori_loop(0, n_slot, place, 0)


def _moe_body(te_ref, nu_ref, xs_ref, wgu_ref, wdn_ref, o_ref, wgu_b, wdn_b):
    i = pl.program_id(0)
    f = wdn_ref.shape[0]

    @pl.when(jnp.logical_or(i == 0, te_ref[i] != te_ref[jnp.maximum(i - 1, 0)]))
    def _():
        wgu_b[...] = wgu_ref[...].astype(bf16)
        wdn_b[...] = wdn_ref[...].astype(bf16)

    @pl.when(i < nu_ref[0])
    def _():
        gu = _mm(xs_ref[...], wgu_b[...])
        g = gu[:, :f]
        act = g * _sigmoid(g) * gu[:, f:]
        o_ref[...] = _mm(act.astype(bf16), wdn_b[...]).astype(bf16)

    @pl.when(i >= nu_ref[0])
    def _():
        o_ref[...] = jnp.zeros_like(o_ref)


def _moe_grouped(xs, wgu, wdn, tile_expert, n_used):
    p, d = xs.shape
    ne, _, f2 = wgu.shape
    f = wdn.shape[1]
    n_tiles = p // TM_MOE
    gs = pltpu.PrefetchScalarGridSpec(
        num_scalar_prefetch=2, grid=(n_tiles,),
        in_specs=[pl.BlockSpec((TM_MOE, d), lambda i, te, nu: (i, 0)),
                  pl.BlockSpec((None, d, f2), lambda i, te, nu: (te[i], 0, 0)),
                  pl.BlockSpec((None, f, d), lambda i, te, nu: (te[i], 0, 0))],
        out_specs=pl.BlockSpec((TM_MOE, d), lambda i, te, nu: (i, 0)),
        scratch_shapes=[pltpu.VMEM((d, f2), bf16), pltpu.VMEM((f, d), bf16)])
    return pl.pallas_call(_moe_body, grid_spec=gs, out_shape=jax.ShapeDtypeStruct((p, d), bf16),
                          compiler_params=_cparams("arbitrary"), name="moe_experts")(
                              tile_expert, n_used, xs, wgu, wdn)


def _combine_body(x_ref, oa_ref, ob_ref, gt_ref, g2_ref, o_ref):
    gt = gt_ref[...]
    y = gt[:, 0:1] * oa_ref[...].astype(f32) + gt[:, 1:2] * ob_ref[...].astype(f32)
    o_ref[...] = x_ref[...] + g2_ref[...] * y


def _tri(n):
    return (jnp.arange(n)[:, None] < jnp.arange(n)[None, :]).astype(bf16)


def _moe_layer(x_all, dims, l, W, last):
    n_all, d = x_all.shape
    n_p, n_s, seq = dims['n_p'], dims['n_s'], dims['seq']
    modp, mods = dims['modp'], dims['mods']
    consts = [W['norm_ff'][l].reshape(1, d), W['w_router_t'], W['b_router_c']]
    outs = [((n_all, d), bf16, 'row'), ((2, n_all), i32, 'col'), ((2, n_all), f32, 'col'),
            ((2, n_all), i32, 'col'), ((N_EXPERTS, 1), f32, 'const')]
    scratch = [pltpu.VMEM((N_EXPERTS, 1), f32)]
    res = _rows_call(_router_body, tb=TB_PROMPT, nblk=n_p // TB_PROMPT, off=0, bps=seq // TB_PROMPT,
                     row_ins=[x_all], mod_ins=[(modp, 3), (modp, 4)], seq_ins=[],
                     consts=consts + [_tri(TB_PROMPT), jnp.zeros((N_EXPERTS, 1), f32)],
                     outs=outs, scratch=scratch, name="router_p")
    h2, ids, gates, rank, cnt = _rows_call(
        _router_body, tb=n_s, nblk=1, off=n_p // n_s, bps=1,
        row_ins=[x_all], mod_ins=[(mods, 3), (mods, 4)], seq_ins=[],
        consts=consts + [_tri(n_s), res[4]], outs=outs, prev=res[:4], scratch=scratch, name="router_s")

    n_tiles = (2 * n_all + N_EXPERTS * (TM_MOE - 1)) // TM_MOE + 1
    counts = cnt[:, 0].astype(i32)
    padded = (counts + TM_MOE - 1) // TM_MOE * TM_MOE
    pend = jnp.cumsum(padded)
    n_used = (pend[-1:] // TM_MOE).astype(i32)
    tile_start = jnp.arange(n_tiles, dtype=i32) * TM_MOE
    tile_expert = jnp.minimum(jnp.sum((pend[None, :] <= tile_start[:, None]).astype(i32), axis=1),
                              N_EXPERTS - 1)
    smem = pl.BlockSpec(memory_space=pltpu.SMEM)
    row_src, dest = pl.pallas_call(
        _invperm_body, in_specs=[smem, smem, smem], out_specs=[smem, smem],
        out_shape=[jax.ShapeDtypeStruct((n_tiles * TM_MOE,), i32), jax.ShapeDtypeStruct((2 * n_all,), i32)],
        name="moe_invperm")(ids.reshape(-1), rank.reshape(-1), (pend - padded).astype(i32))
    xs = jnp.take(h2, row_src, axis=0)
    osort = _moe_grouped(xs, W['w_gate_up'][l], W['w_down'][l], tile_expert, n_used)
    oa = jnp.take(osort, dest[:n_all], axis=0)
    ob = jnp.take(osort, dest[n_all:], axis=0)
    gt = gates.T
    if not last:
        (x_new,) = _both_groups(
            _combine_body, dims, row_ins=[x_all, oa, ob, gt], mod_ks=(5,), consts=[],
            outs=[((n_all, d), f32, 'row')], name="combine")
        return x_new
    (y_p,) = _rows_call(_combine_body, tb=TB_PROMPT, nblk=n_p // TB_PROMPT, off=0, bps=seq // TB_PROMPT,
                        row_ins=[x_all, oa, ob, gt], mod_ins=[(modp, 5)], seq_ins=[], consts=[],
                        outs=[((n_p, d), f32, 'own')], name="combine_last_p")
    (y_s,) = _rows_call(_combine_body, tb=n_s, nblk=1, off=n_p // n_s, bps=1,
                        row_ins=[x_all, oa, ob, gt], mod_ins=[(mods, 5)], seq_ins=[], consts=[],
                        outs=[((n_s, d), f32, 'own')], name="combine_last_s")
    return y_p, y_s


def _pool_tail(pooled_slabs, w_ref, b_ref, ls_ref):
    ys = [_mm(p.astype(bf16), w_ref[g]) for g, p in enumerate(pooled_slabs)]
    return (jnp.concatenate(ys, axis=1) + b_ref[...]) * ls_ref[...]


def _pool_prompt_body(bps, x_ref, xh_ref, sh_ref, sc_ref, g1_ref, nw_ref, w_ref, b_ref, ls_ref,
                      o_ref, st_ref, ext_ref):
    tb = x_ref.shape[0]
    gw = w_ref.shape[1]
    blk = pl.program_id(0) % bps
    x = x_ref[...]
    nw, sc, sh = nw_ref[...], sc_ref[...], sh_ref[...]
    h = _norm_mod(x, nw, sc, sh)
    hh = _norm_mod(xh_ref[...], nw, sc, sh)
    ext_ref[0:HALO, :] = jnp.where(blk == 0, 0.0, hh)
    ext_ref[HALO:, :] = h
    pos = blk * tb + lax.broadcasted_iota(i32, (tb, 1), 0)
    slabs = []
    for g, wdw in enumerate(POOL_WINDOWS):
        lo, hi = g * gw, (g + 1) * gw
        acc = h[:, lo:hi]
        for j in range(1, wdw):
            acc = acc + ext_ref[HALO - j:HALO - j + tb, lo:hi]
        cnt = jnp.minimum(pos + 1, wdw).astype(f32)
        slabs.append(acc / cnt - h[:, lo:hi])
    y = _pool_tail(slabs, w_ref, b_ref, ls_ref)
    o_ref[...] = x + g1_ref[...] * y
    st_ref[...] = h[tb - HALO:, :]


def _pool_sample_body(n_t, pos0, x_ref, past_ref, sh_ref, sc_ref, g1_ref, nw_ref, w_ref, b_ref, ls_ref,
                      o_ref, h_ref):
    nb = past_ref.shape[1]
    gw = w_ref.shape[1]
    x = x_ref[...]
    h = _norm_mod(x, nw_ref[...], sc_ref[...], sh_ref[...])
    h_ref[...] = h

    def ext(r, lo, hi):
        if r < POOL_HIST:
            return past_ref[r][:, lo:hi]
        t = r - POOL_HIST
        return h[t * nb:(t + 1) * nb, lo:hi]

    slabs = []
    for g, wdw in enumerate(POOL_WINDOWS):
        lo, hi = g * gw, (g + 1) * gw
        rows = []
        for t in range(n_t):
            acc = ext(POOL_HIST + t, lo, hi)
            for j in range(1, wdw):
                acc = acc + ext(POOL_HIST + t - j, lo, hi)
            cnt = float(min(pos0 + t + 1, wdw))
            rows.append(acc / cnt - ext(POOL_HIST + t, lo, hi))
        slabs.append(jnp.concatenate(rows, axis=0))
    y = _pool_tail(slabs, w_ref, b_ref, ls_ref)
    o_ref[...] = x + g1_ref[...] * y


def _pool_layer(x_all, x_smp, smp_off, dims, l, ip, W, state_pool):
    d = x_all.shape[1]
    n_p, n_s, seq, nb, n_t, batch = (dims[k] for k in ('n_p', 'n_s', 'seq', 'nb', 'n_t', 'batch'))
    n_all = n_p + n_s
    modp, mods = dims['modp'], dims['mods']
    bps = seq // TB_POOL
    consts = [W['norm_mix'][l].reshape(1, d), W['w_pool'][ip].astype(bf16),
              W['b_pool'][ip].reshape(1, d), W['ls_pool'][ip].reshape(1, d)]

    def mspec(k):
        return pl.BlockSpec((None, None, 1, d), lambda i, _k=k: (_k, i // bps, 0, 0))

    ratio = TB_POOL // HALO
    x_new, st = pl.pallas_call(
        functools.partial(_pool_prompt_body, bps), grid=(n_p // TB_POOL,),
        in_specs=[pl.BlockSpec((TB_POOL, d), lambda i: (i, 0)),
                  pl.BlockSpec((HALO, d), lambda i: (jnp.maximum(i * ratio - 1, 0), 0)),
                  mspec(0), mspec(1), mspec(2)] + [_full_spec(c) for c in consts],
        out_specs=[pl.BlockSpec((TB_POOL, d), lambda i: (i, 0)),
                   pl.BlockSpec((None, HALO, d), lambda i: (i // bps, 0, 0))],
        out_shape=[jax.ShapeDtypeStruct((n_all, d), f32), jax.ShapeDtypeStruct((batch, HALO, d), f32)],
        scratch_shapes=[pltpu.VMEM((TB_POOL + HALO, d), f32)],
        compiler_params=_cparams("arbitrary"), name="pool_p")(
            x_all, x_all, modp, modp, modp, *consts)
    pool_prompt = st[:, HALO - POOL_HIST:, :]

    past = jnp.transpose(state_pool[ip], (1, 0, 2))
    off = n_p // n_s

    def sspec(k):
        return pl.BlockSpec((None, None, n_s, d), lambda i, _k=k: (_k, 0, 0, 0))

    def body(x_ref, past_ref, sh, sc, g1, nw, w, b, ls, prev_ref, o_ref, h_ref):
        _pool_sample_body(n_t, dims['past_len'], x_ref, past_ref, sh, sc, g1, nw, w, b, ls, o_ref, h_ref)

    x_new, h_s = pl.pallas_call(
        body, grid=(1,),
        in_specs=[pl.BlockSpec((n_s, d), lambda i: (smp_off, 0)), _full_spec(past),
                  sspec(0), sspec(1), sspec(2)] + [_full_spec(c) for c in consts]
                 + [pl.BlockSpec(memory_space=pl.ANY)],
        out_specs=[pl.BlockSpec((n_s, d), lambda i: (off, 0)), pl.BlockSpec((n_s, d), lambda i: (0, 0))],
        out_shape=[jax.ShapeDtypeStruct((n_all, d), f32), jax.ShapeDtypeStruct((n_s, d), f32)],
        input_output_aliases={5 + len(consts): 0},
        compiler_params=_cparams("arbitrary"), name="pool_s")(
            x_smp, past, mods, mods, mods, *consts, x_new)
    ext = jnp.concatenate([past, h_s.reshape(n_t, nb, d)], axis=0)
    pool_sample = jnp.transpose(ext[-POOL_HIST:], (1, 0, 2))
    return x_new, pool_prompt, pool_sample


def _ssm_param_body(lr_ref, li_ref, ldt_ref, apr_ref, api_ref, kr_ref, ki_ref):
    lr, li = lr_ref[...], li_ref[...]
    dt = jnp.exp(ldt_ref[...])
    mag = jnp.exp(lr * dt)
    a_r, a_i = mag * jnp.cos(li * dt), mag * jnp.sin(li * dt)
    den = lr * lr + li * li
    nr, ni = a_r - 1.0, a_i
    kr_ref[...] = (nr * lr + ni * li) / den
    ki_ref[...] = (ni * lr - nr * li) / den
    pr, pi_ = a_r, a_i
    for k in range(8):
        apr_ref[k] = pr
        api_ref[k] = pi_
        pr, pi_ = pr * a_r - pi_ * a_i, pr * a_i + pi_ * a_r


def _ssm_tables(W, iq):
    g, p = W['ssm_lambda_re'][iq].shape
    c = SSM_GROUP_CH
    ns = g // SLAB_GROUPS
    sl = SLAB_GROUPS
    shp = [jax.ShapeDtypeStruct((8, g, p), f32)] * 2 + [jax.ShapeDtypeStruct((g, p), f32)] * 2
    apr, api, k_r, k_i = pl.pallas_call(_ssm_param_body, out_shape=shp, name="ssm_params")(
        W['ssm_lambda_re'][iq], W['ssm_lambda_im'][iq], W['ssm_log_dt'][iq].reshape(g, 1))
    b_re, b_im = W['ssm_b_re'][iq], W['ssm_b_im'][iq]
    bb_re = k_r[..., None] * b_re - k_i[..., None] * b_im
    bb_im = k_r[..., None] * b_im + k_i[..., None] * b_re
    eye = jnp.eye(sl, dtype=f32)

    def b_slab(bb):
        b4 = bb.reshape(ns, sl, p, c)
        return jnp.einsum('sgpc,gh->sgchp', b4, eye).reshape(ns, sl * c, sl * p)

    def c_slab(cc):
        c4 = cc.reshape(ns, sl, c, p)
        return jnp.einsum('sgcp,gh->shpgc', c4, eye).reshape(ns, sl * p, sl * c)

    bmat = jnp.concatenate([b_slab(bb_re), b_slab(bb_im)], axis=2).astype(bf16)
    cmat = jnp.concatenate([c_slab(W['ssm_c_re'][iq]), -c_slab(W['ssm_c_im'][iq])], axis=1).astype(bf16)

    def lay(a):
        return jnp.transpose(a.reshape(8, ns, sl * p), (1, 0, 2))

    apw = jnp.stack([lay(apr), lay(api)], axis=1)
    return bmat, cmat, apw


def _ssm_out(x, h, y, g1_ref, dsk_ref, wglu_ref, bglu_ref):
    d = x.shape[1]
    z = _gelu_tanh(y + dsk_ref[...] * h)
    gu = _mm(z.astype(bf16), wglu_ref[...]) + bglu_ref[...]
    return x + g1_ref[...] * (gu[:, :d] * _sigmoid(gu[:, d:]))


def _ssm_prompt_body(x_ref, sh_ref, sc_ref, g1_ref, nw_ref, b_ref, c_ref, t2_ref, dsk_ref,
                     wglu_ref, bglu_ref, o_ref, st_ref, xs_ref, hs_ref, carry_ref):
    t = x_ref.shape[0]
    ns = b_ref.shape[0]
    sw = b_ref.shape[1]
    half = b_ref.shape[2] // 2
    hc = half // LANES

    @pl.when(pl.program_id(1) == 0)
    def _():
        carry_ref[...] = jnp.zeros_like(carry_ref)

    x = x_ref[...]
    h = _norm_mod(x, nw_ref[...], sc_ref[...], sh_ref[...])
    hb = h.astype(bf16)
    ys = []
    for s in range(ns):
        xx = _mm(hb[:, s * sw:(s + 1) * sw], b_ref[s])
        for c in range(2 * hc):
            xs_ref[c] = xx[:, c * LANES:(c + 1) * LANES]

        for c in range(hc):
            ar = jnp.broadcast_to(t2_ref[s, 0, 0:1, c * LANES:(c + 1) * LANES], (8, LANES))
            ai = jnp.broadcast_to(t2_ref[s, 1, 0:1, c * LANES:(c + 1) * LANES], (8, LANES))
            for base in range(0, t, 64):
                hr = xs_ref[c, pl.ds(base, 8, stride=8), :]
                hi = xs_ref[hc + c, pl.ds(base, 8, stride=8), :]
                for i in range(1, 8):
                    rows = pl.ds(base + i, 8, stride=8)
                    hr, hi = (ar * hr - ai * hi + xs_ref[c, rows, :], ar * hi + ai * hr + xs_ref[hc + c, rows, :])
                    xs_ref[c, rows, :] = hr
                    xs_ref[hc + c, rows, :] = hi

        def grp(j, car, s=s):
            cr, ci = car
            r0 = pl.multiple_of(j * 8, 8)
            xr = jnp.concatenate([xs_ref[c, pl.ds(r0, 8), :] for c in range(hc)], axis=1)
            xi = jnp.concatenate([xs_ref[hc + c, pl.ds(r0, 8), :] for c in range(hc)], axis=1)
            pr = t2_ref[s, 0]
            pi_ = t2_ref[s, 1]
            hr = xr + pr * cr - pi_ * ci
            hi = xi + pr * ci + pi_ * cr
            hs_ref[pl.ds(r0, 8), 0:half] = hr
            hs_ref[pl.ds(r0, 8), half:] = hi
            return (jnp.broadcast_to(hr[7:8, :], hr.shape), jnp.broadcast_to(hi[7:8, :], hi.shape))

        cr, ci = lax.fori_loop(0, t // 8, grp, (carry_ref[s, :, 0:half], carry_ref[s, :, half:]))
        carry_ref[s, :, 0:half] = cr
        carry_ref[s, :, half:] = ci
        ys.append(_mm(hs_ref[...].astype(bf16), c_ref[s]))
        st_ref[s:s + 1, :] = jnp.concatenate([cr[0:1, :], ci[0:1, :]], axis=1)
    y = jnp.concatenate(ys, axis=1)
    o_ref[...] = _ssm_out(x, h, y, g1_ref, dsk_ref, wglu_ref, bglu_ref)


def _ssm_sample_body(n_t, x_ref, s0_ref, sh_ref, sc_ref, g1_ref, nw_ref, b_ref, c_ref, t2_ref, dsk_ref,
                     wglu_ref, bglu_ref, prev_ref, o_ref, so_ref, y_ref):
    ns = b_ref.shape[0]
    sw = b_ref.shape[1]
    half = b_ref.shape[2] // 2
    nb = s0_ref.shape[1]
    x = x_ref[...]
    h = _norm_mod(x, nw_ref[...], sc_ref[...], sh_ref[...])
    hb = h.astype(bf16)
    for s in range(ns):
        s_r = s0_ref[s, :, 0:half]
        s_i = s0_ref[s, :, half:]
        ar = t2_ref[s, 0, 0:1, :]
        ai = t2_ref[s, 1, 0:1, :]
        for t in range(n_t):
            xx = _mm(hb[t * nb:(t + 1) * nb, s * sw:(s + 1) * sw], b_ref[s])
            s_r, s_i = ar * s_r - ai * s_i + xx[:, 0:half], ar * s_i + ai * s_r + xx[:, half:]
            st = jnp.concatenate([s_r, s_i], axis=1)
            y_ref[t * nb:(t + 1) * nb, s * sw:(s + 1) * sw] = _mm(st.astype(bf16), c_ref[s])
        so_ref[s] = jnp.concatenate([s_r, s_i], axis=1)
    o_ref[...] = _ssm_out(x, h, y_ref[...], g1_ref, dsk_ref, wglu_ref, bglu_ref)


def _ssm_layer(x_all, dims, l, iq, W, st_re, st_im):
    n_all, d = x_all.shape
    n_p, n_s, seq, nb, n_t, batch = (dims[k] for k in ('n_p', 'n_s', 'seq', 'nb', 'n_t', 'batch'))
    modp, mods = dims['modp'], dims['mods']
    bmat, cmat, t2 = _ssm_tables(W, iq)
    ns = bmat.shape[0]
    sp = bmat.shape[2]
    g, p = W['ssm_lambda_re'][iq].shape
    sl = SLAB_GROUPS
    tail = [W['ssm_d'][iq].reshape(1, d), W['w_glu'][iq].astype(bf16), W['b_glu'][iq].reshape(1, 2 * d)]
    nw = W['norm_mix'][l].reshape(1, d)
    nchunk = seq // T_SSM

    def mspec(k):
        return pl.BlockSpec((None, None, 1, d), lambda b, c, _k=k: (_k, b, 0, 0))

    def cspec(a):
        return pl.BlockSpec(a.shape, lambda b, c, _n=a.ndim: (0,) * _n)

    consts = [nw, bmat, cmat, t2] + tail
    x_new, st = pl.pallas_call(
        _ssm_prompt_body, grid=(batch, nchunk),
        in_specs=[pl.BlockSpec((T_SSM, d), lambda b, c: (b * nchunk + c, 0)), mspec(0), mspec(1), mspec(2)]
                 + [cspec(a) for a in consts],
        out_specs=[pl.BlockSpec((T_SSM, d), lambda b, c: (b * nchunk + c, 0)),
                   pl.BlockSpec((None, ns, sp), lambda b, c: (b, 0, 0))],
        out_shape=[jax.ShapeDtypeStruct((n_all, d), f32), jax.ShapeDtypeStruct((batch, ns, sp), f32)],
        scratch_shapes=[pltpu.VMEM((sp // LANES, T_SSM, LANES), f32), pltpu.VMEM((T_SSM, sp), f32),
                        pltpu.VMEM((ns, 8, sp), f32)],
        compiler_params=_cparams("arbitrary", "arbitrary"), name="ssm_p")(
            x_all, modp, modp, modp, *consts)

    def unslab(a):
        a5 = a.reshape(a.shape[0], ns, 2, sl, p)
        return a5[:, :, 0].reshape(-1, g, p), a5[:, :, 1].reshape(-1, g, p)

    re_p, im_p = unslab(st)

    def slab(a):
        return a.reshape(a.shape[0], ns, sl * p)

    s0 = jnp.transpose(jnp.concatenate([slab(st_re), slab(st_im)], axis=2), (1, 0, 2))
    off = n_p // n_s

    def sspec(k):
        return pl.BlockSpec((None, None, n_s, d), lambda i, _k=k: (_k, 0, 0, 0))

    consts_s = [nw, bmat, cmat, t2] + tail
    x_new, so = pl.pallas_call(
        functools.partial(_ssm_sample_body, n_t), grid=(1,),
        in_specs=[pl.BlockSpec((n_s, d), lambda i: (off, 0)), _full_spec(s0), sspec(0), sspec(1), sspec(2)]
                 + [_full_spec(a) for a in consts_s] + [pl.BlockSpec(memory_space=pl.ANY)],
        out_specs=[pl.BlockSpec((n_s, d), lambda i: (off, 0)), _full_spec(s0)],
        out_shape=[jax.ShapeDtypeStruct((n_all, d), f32), jax.ShapeDtypeStruct(s0.shape, f32)],
        scratch_shapes=[pltpu.VMEM((n_s, d), f32)],
        input_output_aliases={5 + len(consts_s): 0},
        compiler_params=_cparams("arbitrary"), name="ssm_s")(
            x_all, s0, mods, mods, mods, *consts_s, x_new)
    re_s, im_s = unslab(jnp.transpose(so, (1, 0, 2)))
    return x_new, re_p, im_p, re_s, im_s


def _qkv_body(prompt, x_ref, sh_ref, sc_ref, cos_ref, sin_ref, nw_ref, w_ref, qn_ref, kn_ref, seg_ref, segt_ref,
              kb_ref, k_ref, v_ref, a_ref, b_ref):
    d = x_ref.shape[1]
    h = _norm_mod(x_ref[...], nw_ref[...], sc_ref[...], sh_ref[...])
    qkv = _mm(h.astype(bf16), w_ref[...])
    rep = d // LANES
    cos = jnp.concatenate([cos_ref[...]] * rep, axis=1)
    sin = jnp.concatenate([sin_ref[...]] * rep, axis=1)
    lane_lo = (lax.broadcasted_iota(i32, (1, d), 1) % HEAD_DIM) < (HEAD_DIM // 2)
    seg, segt = seg_ref[...], segt_ref[...]

    def norm_rope(t, g):
        s_hi, s_lo = _split2(t * t)
        ms = (_mm(s_hi, seg) + _mm(s_lo, seg)) * (1.0 / HEAD_DIM)
        r_hi, r_mid, r_lo = _split3(lax.rsqrt(ms + NORM_EPS))
        rf = _mm(r_hi, segt) + _mm(r_mid, segt) + _mm(r_lo, segt)
        tn = t * rf * g
        sw = jnp.where(lane_lo, pltpu.roll(tn, d - HEAD_DIM // 2, 1), pltpu.roll(tn, HEAD_DIM // 2, 1))
        return tn * cos + sw * sin

    q = norm_rope(qkv[:, :d], qn_ref[...])
    k = norm_rope(qkv[:, d:2 * d], kn_ref[...])
    v = qkv[:, 2 * d:]
    k_ref[...] = k
    v_ref[...] = v
    kb_ref[...] = k.astype(bf16)
    if prompt:
        a_ref[...] = (q * (HEAD_DIM ** -0.5 * LOG2E)).T.astype(bf16)
        tk = b_ref.shape[2]
        for c in range(b_ref.shape[0]):
            b_ref[c] = v[c * tk:(c + 1) * tk, :].T.astype(bf16)
    else:
        a_ref[...] = (q * (HEAD_DIM ** -0.5)).astype(bf16)
        b_ref[...] = v.astype(bf16)


def _lam(lam_ref):
    l4 = lam_ref[...]
    a = jnp.sum(l4[0:1, :] * l4[1:2, :], axis=1, keepdims=True)
    b = jnp.sum(l4[2:3, :] * l4[3:4, :], axis=1, keepdims=True)
    return jnp.exp(a) - jnp.exp(b)


def _attn_finish(acc, l, lam, lam_init, sub_ref):
    r = acc.shape[0] // 2
    o = acc[:r] / l[:r] - lam * (acc[r:] / l[r:])
    o = o * lax.rsqrt(jnp.mean(o * o, axis=-1, keepdims=True) + NORM_EPS) * sub_ref[...]
    return o * (1.0 - lam_init)


def _flash_body(lam_init, qt_ref, k_ref, vt_ref, lam_ref, sub_ref, o_ref, acc_ref, s_ref, p_ref):
    tq = qt_ref.shape[1]
    tk = vt_ref.shape[2]
    hd2 = 2 * HEAD_DIM
    nh = qt_ref.shape[0] // hd2
    qi = pl.program_id(2)
    top = lax.broadcasted_iota(i32, (hd2, 1), 0) < HEAD_DIM
    qqs = []
    for h in range(nh):
        qt = qt_ref[h * hd2:(h + 1) * hd2, :]
        zero = jnp.zeros_like(qt)
        qqs.append(jnp.concatenate([jnp.where(top, qt, zero), jnp.where(top, zero, qt)], axis=1))
    acc_ref[...] = jnp.zeros_like(acc_ref)
    p_ref[...] = jnp.zeros_like(p_ref)

    def scores(j):
        r0 = pl.multiple_of(j * tk, tk)
        return [_mm(k_ref[pl.ds(r0, tk), h * hd2:(h + 1) * hd2], qqs[h]) for h in range(nh)]

    def keep(ss, slot):
        for h in range(nh):
            s_ref[slot, h] = ss[h]

    def values(j):
        return [_mm(vt_ref[j, h * hd2:(h + 1) * hd2, :], p_ref[h]) for h in range(nh)]

    def softmax(slot, carry, masked):
        out = []
        for h in range(nh):
            m, l, _ = carry[h]
            s = s_ref[slot, h]
            if masked:
                key = lax.broadcasted_iota(i32, s.shape, 0)
                qry = lax.broadcasted_iota(i32, s.shape, 1) % tq
                s = jnp.where(key <= qry, s, NEG_INF)
            m_new = jnp.maximum(m, jnp.max(s, axis=0, keepdims=True))
            alpha = jnp.exp2(m - m_new)
            p = jnp.exp2(s - m_new)
            p_ref[h] = p.astype(bf16)
            out.append((m_new, alpha * l + jnp.sum(p, axis=0, keepdims=True), alpha))
        return tuple(out)

    def accumulate(pvs, carry):
        for h in range(nh):
            acc_ref[h] = carry[h][2] * acc_ref[h] + pvs[h]

    def body(j, carry):
        pvs = values(jnp.maximum(j - 1, 0))
        ss = scores(j + 1)
        new = softmax(j % 2, carry, False)
        keep(ss, (j + 1) % 2)
        accumulate(pvs, carry)
        return new

    keep(scores(0), 0)
    init = tuple((jnp.full((1, 2 * tq), NEG_INF, f32), jnp.zeros((1, 2 * tq), f32),
                  jnp.ones((1, 2 * tq), f32)) for _ in range(nh))
    carry = lax.fori_loop(0, qi, body, init)
    pvs = values(jnp.maximum(qi - 1, 0))
    last = softmax(qi % 2, carry, True)
    accumulate(pvs, carry)
    accumulate(values(qi), last)
    carry = last
    lam = _lam(lam_ref) + lam_init
    for h in range(nh):
        acc = acc_ref[h]
        l = carry[h][1]
        o = acc[:, :tq] / l[:, :tq] - lam * (acc[:, tq:] / l[:, tq:])
        o = o * lax.rsqrt(jnp.mean(o * o, axis=0, keepdims=True) + NORM_EPS) * sub_ref[...]
        o_ref[:, h * hd2:(h + 1) * hd2] = (o * (1.0 - lam_init)).T.astype(o_ref.dtype)


def _attn_sample_body(lam_init, pp, pt_ref, q_ref, kn_ref, vn_ref, mp_ref, mn_ref, lam_ref, sub_ref, *rest):
    k_refs, v_refs = rest[:pp], rest[pp:2 * pp]
    o_ref = rest[2 * pp]
    m_ref, l_ref, acc_ref = rest[2 * pp + 1:]
    j = pl.program_id(1)
    q = q_ref[...]

    @pl.when(j == 0)
    def _():
        m_ref[...] = jnp.full_like(m_ref, NEG_INF)
        l_ref[...] = jnp.zeros_like(l_ref)
        acc_ref[...] = jnp.zeros_like(acc_ref)

    def lane_fold(op, a):
        parts = [a[:, c:c + LANES] for c in range(0, a.shape[1], LANES)]
        while len(parts) > 1:
            parts = [op(parts[i], parts[i + 1]) for i in range(0, len(parts) - 1, 2)] + parts[len(parts) & ~1:]
        return parts[0]

    def update(kfs, vfs, mask):
        ss = [_nt(q, kf) + mask for kf in kfs]
        mx = ss[0] if ss[0].shape[1] < LANES else functools.reduce(jnp.maximum, [lane_fold(jnp.maximum, s) for s in ss])
        m = m_ref[...]
        m_new = jnp.maximum(m, jnp.max(mx, axis=1, keepdims=True))
        alpha = jnp.exp(m - m_new)
        ps = [jnp.exp(s - m_new) for s in ss]
        sm = ps[0] if ps[0].shape[1] < LANES else functools.reduce(jnp.add, [lane_fold(jnp.add, p) for p in ps])
        l_ref[...] = alpha * l_ref[...] + jnp.sum(sm, axis=1, keepdims=True)
        pv = functools.reduce(jnp.add, [_mm(p.astype(bf16), vf) for p, vf in zip(ps, vfs)])
        acc_ref[...] = alpha * acc_ref[...] + pv
        m_ref[...] = m_new

    update([r[...].astype(bf16) for r in k_refs], [r[...].astype(bf16) for r in v_refs], mp_ref[...])

    @pl.when(j == pl.num_programs(1) - 1)
    def _():
        update([kn_ref[...]], [vn_ref[...]], mn_ref[...])
        lam = _lam(lam_ref) + lam_init
        o_ref[...] = _attn_finish(acc_ref[...], l_ref[...], lam, lam_init, sub_ref)


def _oproj_body(x_ref, o_ref_in, g1_ref, w_ref, o_ref):
    o_ref[...] = x_ref[...] + g1_ref[...] * _mm(o_ref_in[...], w_ref[...])


def _attn_layer(x_all, dims, l, ia, W, cache_k, cache_v, page_table):
    n_all, d = x_all.shape
    n_p, n_s, seq, nb, n_t, batch = (dims[k] for k in ('n_p', 'n_s', 'seq', 'nb', 'n_t', 'batch'))
    past_len = dims['past_len']
    hd2 = 2 * HEAD_DIM
    lam_init = 0.8 - 0.6 * math.exp(-0.3 * l)

    half = HEAD_DIM // 2
    inv = jnp.power(ROPE_THETA, -jnp.arange(half, dtype=f32) * (2.0 / HEAD_DIM))

    def tables(pos):
        ang = pos.astype(f32)[:, None] * inv[None, :]
        cos, sin = jnp.cos(ang), jnp.sin(ang)
        return (jnp.concatenate([cos] * 4, axis=1), jnp.concatenate([-sin, sin, -sin, sin], axis=1))

    cos_p, sin_p = tables(jnp.arange(seq))
    cos_s, sin_s = tables(past_len + jnp.repeat(jnp.arange(n_t), nb))
    seg = (jnp.arange(d)[:, None] // HEAD_DIM == jnp.arange(d // HEAD_DIM)[None, :]).astype(bf16)
    consts = [W['norm_mix'][l].reshape(1, d), W['w_qkv'][ia].astype(bf16),
              jnp.tile(W['q_norm'][ia], d // HEAD_DIM).reshape(1, d),
              jnp.tile(W['k_norm'][ia], d // HEAD_DIM).reshape(1, d), seg, seg.T]
    modp, mods = dims['modp'], dims['mods']
    tk = TQ_ATTN
    kb, k_pr, v_pr, qt, vt = _rows_call(
        functools.partial(_qkv_body, True), tb=TB_PROMPT, nblk=n_p // TB_PROMPT, off=0, bps=seq // TB_PROMPT,
        row_ins=[x_all], mod_ins=[(modp, 0), (modp, 1)], seq_ins=[cos_p, sin_p], consts=consts,
        outs=[((n_all, d), bf16, 'row'), ((n_p, d), f32, 'own'), ((n_p, d), f32, 'own'),
              ((d, n_p), bf16, 'col'), ((n_p // tk, d, tk), bf16, 'blk3')], name="qkv_p")
    kb, k_sm, v_sm, qs, vb = _rows_call(
        functools.partial(_qkv_body, False), tb=n_s, nblk=1, off=n_p // n_s, bps=1,
        row_ins=[x_all], mod_ins=[(mods, 0), (mods, 1)], seq_ins=[cos_s, sin_s], consts=consts,
        outs=[((n_all, d), bf16, 'row'), ((n_s, d), f32, 'own'), ((n_s, d), f32, 'own'),
              ((n_s, d), bf16, 'own'), ((n_s, d), bf16, 'own')], prev=[kb], name="qkv_s")

    lam4 = jnp.stack([W['lambda_q1'][ia], W['lambda_k1'][ia], W['lambda_q2'][ia], W['lambda_k2'][ia]])
    sub = W['subln'][ia].reshape(1, hd2)
    sub_c = W['subln'][ia].reshape(hd2, 1)
    nq = seq // TQ_ATTN
    hw = HEADS_PER_STEP * hd2
    o_all = pl.pallas_call(
        functools.partial(_flash_body, lam_init), grid=(batch, N_HEADS // HEADS_PER_STEP, nq),
        in_specs=[pl.BlockSpec((hw, TQ_ATTN), lambda b, h, i: (h, b * nq + i)),
                  pl.BlockSpec((seq, hw), lambda b, h, i: (b, h)),
                  pl.BlockSpec((seq // tk, hw, tk), lambda b, h, i: (b, h, 0)),
                  pl.BlockSpec(lam4.shape, lambda b, h, i: (0, 0)),
                  pl.BlockSpec(sub_c.shape, lambda b, h, i: (0, 0))],
        out_specs=pl.BlockSpec((TQ_ATTN, hw), lambda b, h, i: (b * nq + i, h)),
        out_shape=jax.ShapeDtypeStruct((n_all, d), bf16),
        scratch_shapes=[pltpu.VMEM((HEADS_PER_STEP, hd2, 2 * TQ_ATTN), f32),
                        pltpu.VMEM((2, HEADS_PER_STEP, tk, 2 * TQ_ATTN), f32),
                        pltpu.VMEM((HEADS_PER_STEP, tk, 2 * TQ_ATTN), bf16)],
        compiler_params=_cparams("arbitrary", "arbitrary", "arbitrary"), name="flash_p")(
            qt, kb, vt, lam4, sub_c)

    na, n_phys, page, _, _ = cache_k.shape
    flat = page * N_HEADS
    ck = cache_k.reshape(na * n_phys, flat, hd2)
    cv = cache_v.reshape(na * n_phys, flat, hd2)
    n_pages = page_table.shape[1]
    pp = PAGES_PER_STEP if n_pages % PAGES_PER_STEP == 0 else 1
    rows = 2 * N_HEADS * n_t
    q5 = jnp.transpose(qs.reshape(n_t, nb, N_HEADS, 2, HEAD_DIM), (1, 3, 2, 0, 4))
    z = jnp.zeros_like(q5[:, 0])
    qm = jnp.stack([jnp.concatenate([q5[:, 0], z], -1), jnp.concatenate([z, q5[:, 1]], -1)], axis=1)
    qm = qm.reshape(nb, rows, hd2)

    def new_rows(a):
        return jnp.transpose(a.reshape(n_t, nb, N_HEADS, hd2), (1, 0, 2, 3)).reshape(nb, n_t * N_HEADS, hd2)

    kn, vn = new_rows(kb[n_p:]), new_rows(vb)
    r_h = (jnp.arange(rows) % (N_HEADS * n_t)) // n_t
    r_t = jnp.arange(rows) % n_t
    mask_p = jnp.where(r_h[:, None] == (jnp.arange(flat) % N_HEADS)[None, :], 0.0, NEG_INF).astype(f32)
    cn = jnp.arange(n_t * N_HEADS)
    mask_n = jnp.where((r_h[:, None] == (cn % N_HEADS)[None, :]) & ((cn // N_HEADS)[None, :] <= r_t[:, None]),
                       0.0, NEG_INF).astype(f32)
    base = ia * n_phys

    def seq_spec(shape):
        return pl.BlockSpec((None,) + shape, lambda s, j, pt: (s, 0, 0))

    def cst_spec(a):
        return pl.BlockSpec(a.shape, lambda s, j, pt: (0, 0))

    def page_spec(u):
        return pl.BlockSpec((None, flat, hd2), lambda s, j, pt, _u=u: (base + pt[s, j * pp + _u], 0, 0))

    gs = pltpu.PrefetchScalarGridSpec(
        num_scalar_prefetch=1, grid=(nb, n_pages // pp),
        in_specs=[seq_spec((rows, hd2)), seq_spec((n_t * N_HEADS, hd2)), seq_spec((n_t * N_HEADS, hd2)),
                  cst_spec(mask_p), cst_spec(mask_n), cst_spec(lam4), cst_spec(sub)]
                 + [page_spec(u) for u in range(pp)] + [page_spec(u) for u in range(pp)],
        out_specs=pl.BlockSpec((None, rows // 2, hd2), lambda s, j, pt: (s, 0, 0)),
        scratch_shapes=[pltpu.VMEM((rows, 1), f32), pltpu.VMEM((rows, 1), f32), pltpu.VMEM((rows, hd2), f32)])
    o_s = pl.pallas_call(
        functools.partial(_attn_sample_body, lam_init, pp), grid_spec=gs,
        out_shape=jax.ShapeDtypeStruct((nb, rows // 2, hd2), f32),
        compiler_params=_cparams("arbitrary", "arbitrary"), name="attn_s")(
            page_table, qm, kn, vn, mask_p, mask_n, lam4, sub, *([ck] * pp), *([cv] * pp))
    o_s = jnp.transpose(o_s.reshape(nb, N_HEADS, n_t, hd2), (2, 0, 1, 3)).reshape(n_s, d)
    o_all = lax.dynamic_update_slice(o_all, o_s.astype(bf16), (n_p, 0))

    (x_new,) = _both_groups(
        _oproj_body, dims, row_ins=[x_all, o_all], mod_ks=(2,), consts=[W['w_o'][ia].astype(bf16)],
        outs=[((n_all, d), f32, 'row')], name="oproj")

    def smp(a):
        return jnp.transpose(a.reshape(n_t, nb, N_HEADS, hd2), (1, 0, 2, 3))

    shp = (batch, seq, N_HEADS, hd2)
    return x_new, k_pr.reshape(shp), v_pr.reshape(shp), smp(k_sm), smp(v_sm)


def kernel(x_prompt, x_sample, cache_k, cache_v, state_pool, state_ssm_re, state_ssm_im, page_table, c_prompt, c_sample, w_ada, b_ada, norm_mix, norm_ff, w_pool, b_pool, ls_pool, ssm_lambda_re, ssm_lambda_im, ssm_log_dt, ssm_b_re, ssm_b_im, ssm_c_re, ssm_c_im, ssm_d, w_glu, b_glu, w_qkv, q_norm, k_norm, lambda_q1, lambda_k1, lambda_q2, lambda_k2, subln, w_o, w_router, b_router, w_gate_up, w_down):
    batch, seq, d = x_prompt.shape
    nb, n_t, _ = x_sample.shape
    depth = w_ada.shape[0]
    n_p, n_s = batch * seq, nb * n_t
    past_len = page_table.shape[1] * cache_k.shape[2]
    W = dict(norm_mix=norm_mix, norm_ff=norm_ff, w_pool=w_pool, b_pool=b_pool, ls_pool=ls_pool,
             ssm_lambda_re=ssm_lambda_re, ssm_lambda_im=ssm_lambda_im, ssm_log_dt=ssm_log_dt,
             ssm_b_re=ssm_b_re, ssm_b_im=ssm_b_im, ssm_c_re=ssm_c_re, ssm_c_im=ssm_c_im, ssm_d=ssm_d,
             w_glu=w_glu, b_glu=b_glu, w_qkv=w_qkv, q_norm=q_norm, k_norm=k_norm, lambda_q1=lambda_q1,
             lambda_k1=lambda_k1, lambda_q2=lambda_q2, lambda_k2=lambda_k2, subln=subln, w_o=w_o,
             w_router_t=w_router.T, b_router_c=b_router.reshape(-1, 1),
             w_gate_up=w_gate_up, w_down=w_down)

    ada = _ada_all(jnp.concatenate([c_prompt, c_sample], axis=0), w_ada, b_ada)
    x_all = x_prompt.reshape(n_p, d)
    x_smp, smp_off = jnp.transpose(x_sample, (1, 0, 2)).reshape(n_s, d), 0
    dims = dict(n_p=n_p, n_s=n_s, seq=seq, nb=nb, n_t=n_t, batch=batch, past_len=past_len)

    pools_p, pools_s, k_ps, v_ps, k_ss, v_ss = [], [], [], [], [], []
    re_ps, im_ps, re_ss, im_ss = [], [], [], []
    ip = iq = ia = 0
    for l in range(depth):
        chunks = jnp.transpose(ada[l].reshape(batch + nb, 6, d), (1, 0, 2))
        dims['modp'] = chunks[:, :batch, None, :]
        dims['mods'] = jnp.tile(chunks[:, batch:], (1, n_t, 1))[:, None]
        kind = l % N_MIXERS
        if kind == 0:
            x_all, pp_, ps_ = _pool_layer(x_all, x_smp, smp_off, dims, l, ip, W, state_pool)
            pools_p.append(pp_)
            pools_s.append(ps_)
            ip += 1
        elif kind == 1:
            x_all, rp, imp, rs, ims = _ssm_layer(x_all, dims, l, iq, W, state_ssm_re[iq], state_ssm_im[iq])
            re_ps.append(rp)
            im_ps.append(imp)
            re_ss.append(rs)
            im_ss.append(ims)
            iq += 1
        else:
            x_all, kp, vp, ks, vs = _attn_layer(x_all, dims, l, ia, W, cache_k, cache_v, page_table)
            k_ps.append(kp)
            v_ps.append(vp)
            k_ss.append(ks)
            v_ss.append(vs)
            ia += 1
        x_all = _moe_layer(x_all, dims, l, W, l == depth - 1)
        x_smp, smp_off = x_all, n_p // n_s

    y_prompt = x_all[0].reshape(batch, seq, d)
    y_sample = jnp.transpose(x_all[1].reshape(n_t, nb, d), (1, 0, 2))
    return (y_prompt, y_sample, jnp.stack(k_ps), jnp.stack(v_ps), jnp.stack(k_ss), jnp.stack(v_ss),
            jnp.stack(pools_p), jnp.stack(pools_s), jnp.stack(re_ps), jnp.stack(im_ps),
            jnp.stack(re_ss), jnp.stack(im_ss))
```

```python
import functools
import math

import jax
import jax.numpy as jnp
from jax import lax
from jax.experimental import pallas as pl
from jax.experimental.pallas import tpu as pltpu

f32 = jnp.float32
bf16 = jnp.bfloat16
i32 = jnp.int32

N_MIXERS = 3
POOL_WINDOWS = (2, 4, 8, 16)
POOL_HIST = 15
HALO = 16
SSM_GROUP_CH = 16
SSM_STATE = 64
SLAB_GROUPS = 8
N_HEADS = 8
HEAD_DIM = 64
ROPE_THETA = 10000.0
N_EXPERTS = 16
EXPERTS_PER_GROUP = 4
NORM_EPS = 1e-6
NEG_INF = -1e30
LOG2E = 1.4426950408889634
LANES = 128
VMEM_LIMIT = 48 * 1024 * 1024

TB_PROMPT = 512
TB_POOL = 256
T_SSM = 256
TQ_ATTN = 256
HEADS_PER_STEP = 2
TM_MOE = 256
H2_ROWS_FACTOR = 3
XS_PARTS = 1
INVPERM_UNROLL = 8
PAGES_PER_STEP = 8


def _cparams(*sem):
    return pltpu.CompilerParams(dimension_semantics=sem, vmem_limit_bytes=VMEM_LIMIT)


def _mm(a, b):
    return jnp.dot(a, b, preferred_element_type=f32)


def _nt(a, b):
    return lax.dot_general(a, b, (((1,), (1,)), ((), ())), preferred_element_type=f32)


def _split2(a):
    hi = a.astype(bf16)
    lo = (a - hi.astype(f32)).astype(bf16)
    return hi, lo


def _split3(a):
    hi = a.astype(bf16)
    r = a - hi.astype(f32)
    mid = r.astype(bf16)
    lo = (r - mid.astype(f32)).astype(bf16)
    return hi, mid, lo


def _norm_mod(x, g, sc, sh):
    ms = jnp.mean(x * x, axis=-1, keepdims=True)
    return x * lax.rsqrt(ms + NORM_EPS) * g * (1.0 + sc) + sh


def _gelu_tanh(y):
    return 0.5 * y * (1.0 + jnp.tanh(0.7978845608028654 * (y + 0.044715 * y * y * y)))


def _sigmoid(x):
    return 1.0 / (1.0 + jnp.exp(-x))


def _full_spec(a):
    nd = a.ndim
    return pl.BlockSpec(a.shape, lambda i, _n=nd: (0,) * _n)


def _rows_call(body, *, tb, nblk, off, bps, row_ins, mod_ins, seq_ins, consts, outs, prev=None,
               scratch=(), name):
    in_specs, args = [], []
    for a in row_ins:
        a, o = a if isinstance(a, tuple) else (a, off)
        in_specs.append(pl.BlockSpec((tb, a.shape[1]), lambda i, _o=o: (i + _o, 0)))
        args.append(a)
    for a, k in mod_ins:
        r, d = a.shape[2], a.shape[3]
        in_specs.append(pl.BlockSpec((None, None, r, d), lambda i, _k=k: (_k, i // bps, 0, 0)))
        args.append(a)
    for a in seq_ins:
        in_specs.append(pl.BlockSpec((tb, a.shape[1]), lambda i: (i % bps, 0)))
        args.append(a)
    for a in consts:
        in_specs.append(_full_spec(a))
        args.append(a)
    n_real = len(args)
    aliases = {}
    if prev is not None:
        for j, a in enumerate(prev):
            in_specs.append(pl.BlockSpec(memory_space=pl.ANY))
            args.append(a)
            aliases[n_real + j] = j
    out_specs, out_shapes = [], []
    for shape, dtype, kind in outs:
        if kind == 'row':
            out_specs.append(pl.BlockSpec((tb, shape[1]), lambda i: (i + off, 0)))
        elif kind == 'own':
            out_specs.append(pl.BlockSpec((tb, shape[1]), lambda i: (i, 0)))
        elif kind == 'col':
            out_specs.append(pl.BlockSpec((shape[0], tb), lambda i: (0, i + off)))
        elif kind == 'blk3':
            out_specs.append(pl.BlockSpec((tb // shape[2], shape[1], shape[2]), lambda i: (i, 0, 0)))
        else:
            out_specs.append(pl.BlockSpec(shape, lambda i, _n=len(shape): (0,) * _n))
        out_shapes.append(jax.ShapeDtypeStruct(shape, dtype))
    n_prev = 0 if prev is None else len(prev)

    def wrapped(*refs):
        body(*refs[:n_real], *refs[n_real + n_prev:])

    res = pl.pallas_call(
        wrapped, grid=(nblk,), in_specs=in_specs, out_specs=out_specs, out_shape=out_shapes,
        scratch_shapes=list(scratch), input_output_aliases=aliases,
        compiler_params=_cparams("arbitrary"), name=name)(*args)
    return list(res)


def _both_groups(body, dims, *, row_ins, mod_ks, consts, outs, name, seq_ins_p=(), seq_ins_s=()):
    modp, mods = dims['modp'], dims['mods']
    n_p, n_s, seq = dims['n_p'], dims['n_s'], dims['seq']
    res = _rows_call(body, tb=TB_PROMPT, nblk=n_p // TB_PROMPT, off=0, bps=seq // TB_PROMPT,
                     row_ins=row_ins, mod_ins=[(modp, k) for k in mod_ks], seq_ins=list(seq_ins_p),
                     consts=consts, outs=outs, name=name + "_p")
    res = _rows_call(body, tb=n_s, nblk=1, off=n_p // n_s, bps=1,
                     row_ins=row_ins, mod_ins=[(mods, k) for k in mod_ks], seq_ins=list(seq_ins_s),
                     consts=consts, outs=outs, prev=res, name=name + "_s")
    return res


def _ada_body(c_ref, w_ref, b_ref, o_ref):
    c = c_ref[...]
    cond = c * _sigmoid(c)
    c_hi, c_lo = _split2(cond)
    w_hi, w_lo = _split2(w_ref[...])
    o_ref[...] = _mm(c_hi, w_hi) + _mm(c_hi, w_lo) + _mm(c_lo, w_hi) + b_ref[...]


def _ada_all(c_all, w_ada, b_ada):
    depth, d, d6 = w_ada.shape
    n = c_all.shape[0]
    nj = d6 // d
    return pl.pallas_call(
        _ada_body, grid=(depth, nj),
        in_specs=[pl.BlockSpec((n, d), lambda l, j: (0, 0)),
                  pl.BlockSpec((None, d, d), lambda l, j: (l, 0, j)),
                  pl.BlockSpec((None, 1, d), lambda l, j: (l, 0, j))],
        out_specs=pl.BlockSpec((None, n, d), lambda l, j: (l, 0, j)),
        out_shape=jax.ShapeDtypeStruct((depth, n, d6), f32),
        compiler_params=_cparams("arbitrary", "arbitrary"), name="ada")(
            c_all, w_ada, b_ada.reshape(depth, 1, d6))


def _router_body(x_ref, sh_ref, sc_ref, nw_ref, wrt_ref, br_ref, tri_ref, cnt0_ref,
                 h_ref, ids_ref, gates_ref, rank_ref, cnt_ref, carry_ref):
    @pl.when(pl.program_id(0) == 0)
    def _():
        carry_ref[...] = cnt0_ref[...]

    h = _norm_mod(x_ref[...], nw_ref[...], sc_ref[...], sh_ref[...])
    h_ref[...] = h.astype(bf16)
    h_hi, h_lo = _split2(h)
    w_hi, w_lo = _split2(wrt_ref[...])
    lg = _nt(w_hi, h_hi) + _nt(w_hi, h_lo) + _nt(w_lo, h_hi) + br_ref[...]
    e = jnp.exp(lg - jnp.max(lg, axis=0, keepdims=True))
    best = None
    for g in range(N_EXPERTS // EXPERTS_PER_GROUP):
        v = [e[EXPERTS_PER_GROUP * g + j:EXPERTS_PER_GROUP * g + j + 1, :] for j in range(EXPERTS_PER_GROUP)]
        t1 = jnp.maximum(jnp.maximum(v[0], v[1]), jnp.maximum(v[2], v[3]))
        i1 = jnp.where(v[0] == t1, 0, jnp.where(v[1] == t1, 1, jnp.where(v[2] == t1, 2, 3)))
        w = [jnp.where(i1 == j, -1.0, v[j]) for j in range(EXPERTS_PER_GROUP)]
        t2 = jnp.maximum(jnp.maximum(w[0], w[1]), jnp.maximum(w[2], w[3]))
        i2 = jnp.where(w[0] == t2, 0, jnp.where(w[1] == t2, 1, jnp.where(w[2] == t2, 2, 3)))
        cand = (t1 + t2, t1, t2, i1 + EXPERTS_PER_GROUP * g, i2 + EXPERTS_PER_GROUP * g)
        if best is None:
            best = cand
        else:
            upd = cand[0] > best[0]
            best = tuple(jnp.where(upd, c, b) for c, b in zip(cand, best))
    s, t1, t2, i1, i2 = best
    ids_ref[0:1, :] = i1.astype(i32)
    ids_ref[1:2, :] = i2.astype(i32)
    gates_ref[0:1, :] = t1 / s
    gates_ref[1:2, :] = t2 / s
    eidx = lax.broadcasted_iota(i32, lg.shape, 0)
    oh1 = (eidx == i1).astype(f32)
    oh2 = (eidx == i2).astype(f32)
    tri = tri_ref[...]
    ex1 = _mm(oh1.astype(bf16), tri)
    ex2 = _mm(oh2.astype(bf16), tri)
    carry = carry_ref[...]
    tot1 = jnp.sum(oh1, axis=1, keepdims=True)
    tot2 = jnp.sum(oh2, axis=1, keepdims=True)
    rank_ref[0:1, :] = jnp.sum(oh1 * (carry + ex1), axis=0, keepdims=True).astype(i32)
    rank_ref[1:2, :] = jnp.sum(oh2 * (carry + tot1 + ex2), axis=0, keepdims=True).astype(i32)
    carry_ref[...] = carry + tot1 + tot2
    cnt_ref[...] = carry_ref[...]


def _invperm_body(dest_ref, src_ref):
    n_tok = dest_ref.shape[0] // 2

    def zero(i, c):
        src_ref[i] = 0
        return c

    lax.fori_loop(0, src_ref.shape[0], zero, 0, unroll=INVPERM_UNROLL)

    def place(t, c):
        src_ref[dest_ref[t]] = t
        src_ref[dest_ref[n_tok + t]] = t
        return c

    lax.fori_loop(0, n_tok, place, 0, unroll=INVPERM_UNROLL)


def _moe_body(bounds, te_ref, nu_ref, *refs):
    n_parts = len(bounds) - 1
    x_refs = refs[:n_parts]
    wgu_ref, wdn_ref, o_ref, wgu_b, wdn_b = refs[n_parts:]
    i = pl.program_id(0)
    f = wdn_ref.shape[0]

    @pl.when(jnp.logical_or(i == 0, te_ref[i] != te_ref[jnp.maximum(i - 1, 0)]))
    def _():
        wgu_b[...] = wgu_ref[...].astype(bf16)
        wdn_b[...] = wdn_ref[...].astype(bf16)

    for k in range(n_parts):
        @pl.when(jnp.logical_and(i < nu_ref[0], jnp.logical_and(i >= bounds[k], i < bounds[k + 1])))
        def _(k=k):
            gu = _mm(x_refs[k][...], wgu_b[...])
            g = gu[:, :f]
            act = g * _sigmoid(g) * gu[:, f:]
            o_ref[...] = _mm(act.astype(bf16), wdn_b[...]).astype(bf16)

    @pl.when(i >= nu_ref[0])
    def _():
        o_ref[...] = jnp.zeros_like(o_ref)


def _moe_grouped(xs, wgu, wdn, l, tile_expert, n_used):
    d = xs[0].shape[1]
    f2 = wgu.shape[3]
    f = wdn.shape[2]
    bounds = [0]
    for x in xs:
        bounds.append(bounds[-1] + x.shape[0] // TM_MOE)

    def part_spec(k):
        return pl.BlockSpec((TM_MOE, d), lambda i, te, nu: (jnp.clip(i - bounds[k], 0, bounds[k + 1] - bounds[k] - 1), 0))

    gs = pltpu.PrefetchScalarGridSpec(
        num_scalar_prefetch=2, grid=(bounds[-1],),
        in_specs=[part_spec(k) for k in range(len(xs))]
                 + [pl.BlockSpec((None, None, d, f2), lambda i, te, nu: (l, te[i], 0, 0)),
                    pl.BlockSpec((None, None, f, d), lambda i, te, nu: (l, te[i], 0, 0))],
        out_specs=pl.BlockSpec((TM_MOE, d), lambda i, te, nu: (i, 0)),
        scratch_shapes=[pltpu.VMEM((d, f2), bf16), pltpu.VMEM((f, d), bf16)])
    return pl.pallas_call(functools.partial(_moe_body, tuple(bounds)), grid_spec=gs,
                          out_shape=jax.ShapeDtypeStruct((bounds[-1] * TM_MOE, d), bf16),
                          compiler_params=_cparams("arbitrary"), name="moe_experts")(
                              tile_expert, n_used, *xs, wgu, wdn)


def _combine_body(x_ref, oa_ref, ob_ref, gt_ref, g2_ref, o_ref):
    gt = gt_ref[...]
    y = gt[:, 0:1] * oa_ref[...].astype(f32) + gt[:, 1:2] * ob_ref[...].astype(f32)
    o_ref[...] = x_ref[...] + g2_ref[...] * y


def _tri(n):
    return (jnp.arange(n)[:, None] < jnp.arange(n)[None, :]).astype(bf16)


def _moe_layer(x_all, dims, l, W, last):
    n_all, d = x_all.shape
    n_p, n_s, seq = dims['n_p'], dims['n_s'], dims['seq']
    modp, mods = dims['modp'], dims['mods']
    consts = [W['norm_ff'][l].reshape(1, d), W['w_router_t'], W['b_router_c']]
    outs = [((H2_ROWS_FACTOR * n_all, d), bf16, 'row'), ((2, n_all), i32, 'col'), ((2, n_all), f32, 'col'),
            ((2, n_all), i32, 'col'), ((N_EXPERTS, 1), f32, 'const')]
    scratch = [pltpu.VMEM((N_EXPERTS, 1), f32)]
    res = _rows_call(_router_body, tb=TB_PROMPT, nblk=n_p // TB_PROMPT, off=0, bps=seq // TB_PROMPT,
                     row_ins=[x_all], mod_ins=[(modp, 3), (modp, 4)], seq_ins=[],
                     consts=consts + [_tri(TB_PROMPT), jnp.zeros((N_EXPERTS, 1), f32)],
                     outs=outs, scratch=scratch, name="router_p")
    h2, ids, gates, rank, cnt = _rows_call(
        _router_body, tb=n_s, nblk=1, off=n_p // n_s, bps=1,
        row_ins=[x_all], mod_ins=[(mods, 3), (mods, 4)], seq_ins=[],
        consts=consts + [_tri(n_s), res[4]], outs=outs, prev=res[:4], scratch=scratch, name="router_s")

    n_tiles = (2 * n_all + N_EXPERTS * (TM_MOE - 1)) // TM_MOE + 1
    counts = cnt[:, 0].astype(i32)
    padded = (counts + TM_MOE - 1) // TM_MOE * TM_MOE
    pend = jnp.cumsum(padded)
    n_used = (pend[-1:] // TM_MOE).astype(i32)
    tile_start = jnp.arange(n_tiles, dtype=i32) * TM_MOE
    tile_expert = jnp.minimum(jnp.sum((pend[None, :] <= tile_start[:, None]).astype(i32), axis=1),
                              N_EXPERTS - 1)
    pstart = (pend - padded).astype(i32)
    sel = ids[:, :, None] == jnp.arange(N_EXPERTS, dtype=i32)[None, None, :]
    dest = rank + jnp.sum(jnp.where(sel, pstart[None, None, :], 0), axis=2)
    smem = pl.BlockSpec(memory_space=pltpu.SMEM)
    row_src = pl.pallas_call(
        _invperm_body, in_specs=[smem], out_specs=smem,
        out_shape=jax.ShapeDtypeStruct((n_tiles * TM_MOE,), i32), name="moe_invperm")(dest.reshape(-1))
    def rows(a, idx):
        return a.at[idx].get(mode='promise_in_bounds')

    row_src = jnp.minimum(row_src, n_all - 1)
    cuts = [k * n_tiles // XS_PARTS * TM_MOE for k in range(XS_PARTS + 1)]
    xs = [rows(h2, row_src[cuts[k]:cuts[k + 1]]) for k in range(XS_PARTS)]
    osort = _moe_grouped(xs, W['w_gate_up'], W['w_down'], l, tile_expert, n_used)
    oa = rows(osort, dest[0])
    ob = rows(osort, dest[1])
    gt = gates.T
    if not last:
        (x_new,) = _both_groups(
            _combine_body, dims, row_ins=[x_all, oa, ob, gt], mod_ks=(5,), consts=[],
            outs=[((n_all, d), f32, 'row')], name="combine")
        return x_new
    (y_p,) = _rows_call(_combine_body, tb=TB_PROMPT, nblk=n_p // TB_PROMPT, off=0, bps=seq // TB_PROMPT,
                        row_ins=[x_all, oa, ob, gt], mod_ins=[(modp, 5)], seq_ins=[], consts=[],
                        outs=[((n_p, d), f32, 'own')], name="combine_last_p")
    (y_s,) = _rows_call(_combine_body, tb=n_s, nblk=1, off=n_p // n_s, bps=1,
                        row_ins=[x_all, oa, ob, gt], mod_ins=[(mods, 5)], seq_ins=[], consts=[],
                        outs=[((n_s, d), f32, 'own')], name="combine_last_s")
    return y_p, y_s


def _pool_tail(pooled_slabs, w_ref, b_ref, ls_ref):
    ys = [_mm(p.astype(bf16), w_ref[g]) for g, p in enumerate(pooled_slabs)]
    return (jnp.concatenate(ys, axis=1) + b_ref[...]) * ls_ref[...]


def _pool_prompt_body(bps, x_ref, xh_ref, sh_ref, sc_ref, g1_ref, nw_ref, w_ref, b_ref, ls_ref,
                      o_ref, st_ref, ext_ref):
    tb = x_ref.shape[0]
    gw = w_ref.shape[1]
    blk = pl.program_id(0) % bps
    x = x_ref[...]
    nw, sc, sh = nw_ref[...], sc_ref[...], sh_ref[...]
    h = _norm_mod(x, nw, sc, sh)
    hh = _norm_mod(xh_ref[...], nw, sc, sh)
    ext_ref[0:HALO, :] = jnp.where(blk == 0, 0.0, hh)
    ext_ref[HALO:, :] = h
    pos = blk * tb + lax.broadcasted_iota(i32, (tb, 1), 0)
    slabs = []
    for g, wdw in enumerate(POOL_WINDOWS):
        lo, hi = g * gw, (g + 1) * gw
        acc = h[:, lo:hi]
        for j in range(1, wdw):
            acc = acc + ext_ref[HALO - j:HALO - j + tb, lo:hi]
        cnt = jnp.minimum(pos + 1, wdw).astype(f32)
        slabs.append(acc / cnt - h[:, lo:hi])
    y = _pool_tail(slabs, w_ref, b_ref, ls_ref)
    o_ref[...] = x + g1_ref[...] * y
    st_ref[...] = h[tb - HALO:, :]


def _pool_sample_body(n_t, pos0, x_ref, past_ref, sh_ref, sc_ref, g1_ref, nw_ref, w_ref, b_ref, ls_ref,
                      o_ref, h_ref):
    nb = past_ref.shape[1]
    gw = w_ref.shape[1]
    x = x_ref[...]
    h = _norm_mod(x, nw_ref[...], sc_ref[...], sh_ref[...])
    h_ref[...] = h

    def ext(r, lo, hi):
        if r < POOL_HIST:
            return past_ref[r][:, lo:hi]
        t = r - POOL_HIST
        return h[t * nb:(t + 1) * nb, lo:hi]

    slabs = []
    for g, wdw in enumerate(POOL_WINDOWS):
        lo, hi = g * gw, (g + 1) * gw
        rows = []
        for t in range(n_t):
            acc = ext(POOL_HIST + t, lo, hi)
            for j in range(1, wdw):
                acc = acc + ext(POOL_HIST + t - j, lo, hi)
            cnt = float(min(pos0 + t + 1, wdw))
            rows.append(acc / cnt - ext(POOL_HIST + t, lo, hi))
        slabs.append(jnp.concatenate(rows, axis=0))
    y = _pool_tail(slabs, w_ref, b_ref, ls_ref)
    o_ref[...] = x + g1_ref[...] * y


def _pool_layer(x_all, x_smp, smp_off, dims, l, ip, W, state_pool):
    d = x_all.shape[1]
    n_p, n_s, seq, nb, n_t, batch = (dims[k] for k in ('n_p', 'n_s', 'seq', 'nb', 'n_t', 'batch'))
    n_all = n_p + n_s
    modp, mods = dims['modp'], dims['mods']
    bps = seq // TB_POOL
    consts = [W['norm_mix'][l].reshape(1, d), W['w_pool'][ip].astype(bf16),
              W['b_pool'][ip].reshape(1, d), W['ls_pool'][ip].reshape(1, d)]

    def mspec(k):
        return pl.BlockSpec((None, None, 1, d), lambda i, _k=k: (_k, i // bps, 0, 0))

    ratio = TB_POOL // HALO
    x_new, st = pl.pallas_call(
        functools.partial(_pool_prompt_body, bps), grid=(n_p // TB_POOL,),
        in_specs=[pl.BlockSpec((TB_POOL, d), lambda i: (i, 0)),
                  pl.BlockSpec((HALO, d), lambda i: (jnp.maximum(i * ratio - 1, 0), 0)),
                  mspec(0), mspec(1), mspec(2)] + [_full_spec(c) for c in consts],
        out_specs=[pl.BlockSpec((TB_POOL, d), lambda i: (i, 0)),
                   pl.BlockSpec((None, HALO, d), lambda i: (i // bps, 0, 0))],
        out_shape=[jax.ShapeDtypeStruct((n_all, d), f32), jax.ShapeDtypeStruct((batch, HALO, d), f32)],
        scratch_shapes=[pltpu.VMEM((TB_POOL + HALO, d), f32)],
        compiler_params=_cparams("arbitrary"), name="pool_p")(
            x_all, x_all, modp, modp, modp, *consts)
    pool_prompt = st[:, HALO - POOL_HIST:, :]

    past = jnp.transpose(state_pool[ip], (1, 0, 2))
    off = n_p // n_s

    def sspec(k):
        return pl.BlockSpec((None, None, n_s, d), lambda i, _k=k: (_k, 0, 0, 0))

    def body(x_ref, past_ref, sh, sc, g1, nw, w, b, ls, prev_ref, o_ref, h_ref):
        _pool_sample_body(n_t, dims['past_len'], x_ref, past_ref, sh, sc, g1, nw, w, b, ls, o_ref, h_ref)

    x_new, h_s = pl.pallas_call(
        body, grid=(1,),
        in_specs=[pl.BlockSpec((n_s, d), lambda i: (smp_off, 0)), _full_spec(past),
                  sspec(0), sspec(1), sspec(2)] + [_full_spec(c) for c in consts]
                 + [pl.BlockSpec(memory_space=pl.ANY)],
        out_specs=[pl.BlockSpec((n_s, d), lambda i: (off, 0)), pl.BlockSpec((n_s, d), lambda i: (0, 0))],
        out_shape=[jax.ShapeDtypeStruct((n_all, d), f32), jax.ShapeDtypeStruct((n_s, d), f32)],
        input_output_aliases={5 + len(consts): 0},
        compiler_params=_cparams("arbitrary"), name="pool_s")(
            x_smp, past, mods, mods, mods, *consts, x_new)
    ext = jnp.concatenate([past, h_s.reshape(n_t, nb, d)], axis=0)
    pool_sample = jnp.transpose(ext[-POOL_HIST:], (1, 0, 2))
    return x_new, pool_prompt, pool_sample


def _ssm_param_body(lr_ref, li_ref, ldt_ref, apr_ref, api_ref, kr_ref, ki_ref):
    lr, li = lr_ref[...], li_ref[...]
    dt = jnp.exp(ldt_ref[...])
    mag = jnp.exp(lr * dt)
    a_r, a_i = mag * jnp.cos(li * dt), mag * jnp.sin(li * dt)
    den = lr * lr + li * li
    nr, ni = a_r - 1.0, a_i
    kr_ref[...] = (nr * lr + ni * li) / den
    ki_ref[...] = (ni * lr - nr * li) / den
    pr, pi_ = a_r, a_i
    for k in range(8):
        apr_ref[k] = pr
        api_ref[k] = pi_
        pr, pi_ = pr * a_r - pi_ * a_i, pr * a_i + pi_ * a_r


def _ssm_tables(W, iq):
    g, p = W['ssm_lambda_re'][iq].shape
    c = SSM_GROUP_CH
    ns = g // SLAB_GROUPS
    sl = SLAB_GROUPS
    shp = [jax.ShapeDtypeStruct((8, g, p), f32)] * 2 + [jax.ShapeDtypeStruct((g, p), f32)] * 2
    apr, api, k_r, k_i = pl.pallas_call(_ssm_param_body, out_shape=shp, name="ssm_params")(
        W['ssm_lambda_re'][iq], W['ssm_lambda_im'][iq], W['ssm_log_dt'][iq].reshape(g, 1))
    b_re, b_im = W['ssm_b_re'][iq], W['ssm_b_im'][iq]
    bb_re = k_r[..., None] * b_re - k_i[..., None] * b_im
    bb_im = k_r[..., None] * b_im + k_i[..., None] * b_re
    eye = jnp.eye(sl, dtype=f32)

    def b_slab(bb):
        b4 = bb.reshape(ns, sl, p, c)
        return jnp.einsum('sgpc,gh->sgchp', b4, eye).reshape(ns, sl * c, sl * p)

    def c_slab(cc):
        c4 = cc.reshape(ns, sl, c, p)
        return jnp.einsum('sgcp,gh->shpgc', c4, eye).reshape(ns, sl * p, sl * c)

    bmat = jnp.concatenate([b_slab(bb_re), b_slab(bb_im)], axis=2).astype(bf16)
    cmat = jnp.concatenate([c_slab(W['ssm_c_re'][iq]), -c_slab(W['ssm_c_im'][iq])], axis=1).astype(bf16)

    def lay(a):
        return jnp.transpose(a.reshape(8, ns, sl * p), (1, 0, 2))

    apw = jnp.stack([lay(apr), lay(api)], axis=1)
    return bmat, cmat, apw


def _ssm_out(x, h, y, g1_ref, dsk_ref, wglu_ref, bglu_ref):
    d = x.shape[1]
    z = _gelu_tanh(y + dsk_ref[...] * h)
    gu = _mm(z.astype(bf16), wglu_ref[...]) + bglu_ref[...]
    return x + g1_ref[...] * (gu[:, :d] * _sigmoid(gu[:, d:]))


def _ssm_prompt_body(x_ref, sh_ref, sc_ref, g1_ref, nw_ref, b_ref, c_ref, t2_ref, dsk_ref,
                     wglu_ref, bglu_ref, o_ref, st_ref, xs_ref, hs_ref, carry_ref):
    t = x_ref.shape[0]
    ns = b_ref.shape[0]
    sw = b_ref.shape[1]
    half = b_ref.shape[2] // 2
    hc = half // LANES

    @pl.when(pl.program_id(1) == 0)
    def _():
        carry_ref[...] = jnp.zeros_like(carry_ref)

    x = x_ref[...]
    h = _norm_mod(x, nw_ref[...], sc_ref[...], sh_ref[...])
    hb = h.astype(bf16)
    ys = []
    for s in range(ns):
        xx = _mm(hb[:, s * sw:(s + 1) * sw], b_ref[s])
        for c in range(2 * hc):
            xs_ref[c] = xx[:, c * LANES:(c + 1) * LANES]

        for c in range(hc):
            ar = jnp.broadcast_to(t2_ref[s, 0, 0:1, c * LANES:(c + 1) * LANES], (8, LANES))
            ai = jnp.broadcast_to(t2_ref[s, 1, 0:1, c * LANES:(c + 1) * LANES], (8, LANES))
            for base in range(0, t, 64):
                hr = xs_ref[c, pl.ds(base, 8, stride=8), :]
                hi = xs_ref[hc + c, pl.ds(base, 8, stride=8), :]
                for i in range(1, 8):
                    rows = pl.ds(base + i, 8, stride=8)
                    hr, hi = (ar * hr - ai * hi + xs_ref[c, rows, :], ar * hi + ai * hr + xs_ref[hc + c, rows, :])
                    xs_ref[c, rows, :] = hr
                    xs_ref[hc + c, rows, :] = hi

        def grp(j, car, s=s):
            cr, ci = car
            r0 = pl.multiple_of(j * 8, 8)
            xr = jnp.concatenate([xs_ref[c, pl.ds(r0, 8), :] for c in range(hc)], axis=1)
            xi = jnp.concatenate([xs_ref[hc + c, pl.ds(r0, 8), :] for c in range(hc)], axis=1)
            pr = t2_ref[s, 0]
            pi_ = t2_ref[s, 1]
            hr = xr + pr * cr - pi_ * ci
            hi = xi + pr * ci + pi_ * cr
            hs_ref[pl.ds(r0, 8), 0:half] = hr
            hs_ref[pl.ds(r0, 8), half:] = hi
            return (jnp.broadcast_to(hr[7:8, :], hr.shape), jnp.broadcast_to(hi[7:8, :], hi.shape))

        cr, ci = lax.fori_loop(0, t // 8, grp, (carry_ref[s, :, 0:half], carry_ref[s, :, half:]))
        carry_ref[s, :, 0:half] = cr
        carry_ref[s, :, half:] = ci
        ys.append(_mm(hs_ref[...].astype(bf16), c_ref[s]))
        st_ref[s:s + 1, :] = jnp.concatenate([cr[0:1, :], ci[0:1, :]], axis=1)
    y = jnp.concatenate(ys, axis=1)
    o_ref[...] = _ssm_out(x, h, y, g1_ref, dsk_ref, wglu_ref, bglu_ref)


def _ssm_sample_body(n_t, x_ref, s0_ref, sh_ref, sc_ref, g1_ref, nw_ref, b_ref, c_ref, t2_ref, dsk_ref,
                     wglu_ref, bglu_ref, prev_ref, o_ref, so_ref, y_ref):
    ns = b_ref.shape[0]
    sw = b_ref.shape[1]
    half = b_ref.shape[2] // 2
    nb = s0_ref.shape[1]
    x = x_ref[...]
    h = _norm_mod(x, nw_ref[...], sc_ref[...], sh_ref[...])
    hb = h.astype(bf16)
    for s in range(ns):
        s_r = s0_ref[s, :, 0:half]
        s_i = s0_ref[s, :, half:]
        ar = t2_ref[s, 0, 0:1, :]
        ai = t2_ref[s, 1, 0:1, :]
        for t in range(n_t):
            xx = _mm(hb[t * nb:(t + 1) * nb, s * sw:(s + 1) * sw], b_ref[s])
            s_r, s_i = ar * s_r - ai * s_i + xx[:, 0:half], ar * s_i + ai * s_r + xx[:, half:]
            st = jnp.concatenate([s_r, s_i], axis=1)
            y_ref[t * nb:(t + 1) * nb, s * sw:(s + 1) * sw] = _mm(st.astype(bf16), c_ref[s])
        so_ref[s] = jnp.concatenate([s_r, s_i], axis=1)
    o_ref[...] = _ssm_out(x, h, y_ref[...], g1_ref, dsk_ref, wglu_ref, bglu_ref)


def _ssm_layer(x_all, dims, l, iq, W, st_re, st_im):
    n_all, d = x_all.shape
    n_p, n_s, seq, nb, n_t, batch = (dims[k] for k in ('n_p', 'n_s', 'seq', 'nb', 'n_t', 'batch'))
    modp, mods = dims['modp'], dims['mods']
    bmat, cmat, t2 = _ssm_tables(W, iq)
    ns = bmat.shape[0]
    sp = bmat.shape[2]
    g, p = W['ssm_lambda_re'][iq].shape
    sl = SLAB_GROUPS
    tail = [W['ssm_d'][iq].reshape(1, d), W['w_glu'][iq].astype(bf16), W['b_glu'][iq].reshape(1, 2 * d)]
    nw = W['norm_mix'][l].reshape(1, d)
    nchunk = seq // T_SSM

    def mspec(k):
        return pl.BlockSpec((None, None, 1, d), lambda b, c, _k=k: (_k, b, 0, 0))

    def cspec(a):
        return pl.BlockSpec(a.shape, lambda b, c, _n=a.ndim: (0,) * _n)

    consts = [nw, bmat, cmat, t2] + tail
    x_new, st = pl.pallas_call(
        _ssm_prompt_body, grid=(batch, nchunk),
        in_specs=[pl.BlockSpec((T_SSM, d), lambda b, c: (b * nchunk + c, 0)), mspec(0), mspec(1), mspec(2)]
                 + [cspec(a) for a in consts],
        out_specs=[pl.BlockSpec((T_SSM, d), lambda b, c: (b * nchunk + c, 0)),
                   pl.BlockSpec((None, ns, sp), lambda b, c: (b, 0, 0))],
        out_shape=[jax.ShapeDtypeStruct((n_all, d), f32), jax.ShapeDtypeStruct((batch, ns, sp), f32)],
        scratch_shapes=[pltpu.VMEM((sp // LANES, T_SSM, LANES), f32), pltpu.VMEM((T_SSM, sp), f32),
                        pltpu.VMEM((ns, 8, sp), f32)],
        compiler_params=_cparams("arbitrary", "arbitrary"), name="ssm_p")(
            x_all, modp, modp, modp, *consts)

    def unslab(a):
        a5 = a.reshape(a.shape[0], ns, 2, sl, p)
        return a5[:, :, 0].reshape(-1, g, p), a5[:, :, 1].reshape(-1, g, p)

    re_p, im_p = unslab(st)

    def slab(a):
        return a.reshape(a.shape[0], ns, sl * p)

    s0 = jnp.transpose(jnp.concatenate([slab(st_re), slab(st_im)], axis=2), (1, 0, 2))
    off = n_p // n_s

    def sspec(k):
        return pl.BlockSpec((None, None, n_s, d), lambda i, _k=k: (_k, 0, 0, 0))

    consts_s = [nw, bmat, cmat, t2] + tail
    x_new, so = pl.pallas_call(
        functools.partial(_ssm_sample_body, n_t), grid=(1,),
        in_specs=[pl.BlockSpec((n_s, d), lambda i: (off, 0)), _full_spec(s0), sspec(0), sspec(1), sspec(2)]
                 + [_full_spec(a) for a in consts_s] + [pl.BlockSpec(memory_space=pl.ANY)],
        out_specs=[pl.BlockSpec((n_s, d), lambda i: (off, 0)), _full_spec(s0)],
        out_shape=[jax.ShapeDtypeStruct((n_all, d), f32), jax.ShapeDtypeStruct(s0.shape, f32)],
        scratch_shapes=[pltpu.VMEM((n_s, d), f32)],
        input_output_aliases={5 + len(consts_s): 0},
        compiler_params=_cparams("arbitrary"), name="ssm_s")(
            x_all, s0, mods, mods, mods, *consts_s, x_new)
    re_s, im_s = unslab(jnp.transpose(so, (1, 0, 2)))
    return x_new, re_p, im_p, re_s, im_s


def _qkv_body(prompt, x_ref, sh_ref, sc_ref, cos_ref, sin_ref, nw_ref, w_ref, qn_ref, kn_ref, seg_ref, segt_ref,
              kb_ref, k_ref, v_ref, a_ref, b_ref):
    d = x_ref.shape[1]
    h = _norm_mod(x_ref[...], nw_ref[...], sc_ref[...], sh_ref[...])
    qkv = _mm(h.astype(bf16), w_ref[...])
    rep = d // LANES
    cos = jnp.concatenate([cos_ref[...]] * rep, axis=1)
    sin = jnp.concatenate([sin_ref[...]] * rep, axis=1)
    lane_lo = (lax.broadcasted_iota(i32, (1, d), 1) % HEAD_DIM) < (HEAD_DIM // 2)
    seg, segt = seg_ref[...], segt_ref[...]

    def norm_rope(t, g):
        s_hi, s_lo = _split2(t * t)
        ms = (_mm(s_hi, seg) + _mm(s_lo, seg)) * (1.0 / HEAD_DIM)
        r_hi, r_mid, r_lo = _split3(lax.rsqrt(ms + NORM_EPS))
        rf = _mm(r_hi, segt) + _mm(r_mid, segt) + _mm(r_lo, segt)
        tn = t * rf * g
        sw = jnp.where(lane_lo, pltpu.roll(tn, d - HEAD_DIM // 2, 1), pltpu.roll(tn, HEAD_DIM // 2, 1))
        return tn * cos + sw * sin

    q = norm_rope(qkv[:, :d], qn_ref[...])
    k = norm_rope(qkv[:, d:2 * d], kn_ref[...])
    v = qkv[:, 2 * d:]
    k_ref[...] = k
    v_ref[...] = v
    kb_ref[...] = k.astype(bf16)
    if prompt:
        a_ref[...] = (q * (HEAD_DIM ** -0.5 * LOG2E)).T.astype(bf16)
        tk = b_ref.shape[2]
        for c in range(b_ref.shape[0]):
            b_ref[c] = v[c * tk:(c + 1) * tk, :].T.astype(bf16)
    else:
        a_ref[...] = (q * (HEAD_DIM ** -0.5)).astype(bf16)
        b_ref[...] = v.astype(bf16)


def _lam(lam_ref):
    l4 = lam_ref[...]
    a = jnp.sum(l4[0:1, :] * l4[1:2, :], axis=1, keepdims=True)
    b = jnp.sum(l4[2:3, :] * l4[3:4, :], axis=1, keepdims=True)
    return jnp.exp(a) - jnp.exp(b)


def _attn_finish(acc, l, lam, lam_init, sub_ref):
    r = acc.shape[0] // 2
    o = acc[:r] / l[:r] - lam * (acc[r:] / l[r:])
    o = o * lax.rsqrt(jnp.mean(o * o, axis=-1, keepdims=True) + NORM_EPS) * sub_ref[...]
    return o * (1.0 - lam_init)


def _flash_body(lam_init, qt_ref, k_ref, vt_ref, lam_ref, sub_ref, o_ref, acc_ref, s_ref, p_ref):
    tq = qt_ref.shape[1]
    tk = vt_ref.shape[2]
    hd2 = 2 * HEAD_DIM
    nh = qt_ref.shape[0] // hd2
    qi = pl.program_id(2)
    top = lax.broadcasted_iota(i32, (hd2, 1), 0) < HEAD_DIM
    qqs = []
    for h in range(nh):
        qt = qt_ref[h * hd2:(h + 1) * hd2, :]
        zero = jnp.zeros_like(qt)
        qqs.append(jnp.concatenate([jnp.where(top, qt, zero), jnp.where(top, zero, qt)], axis=1))
    acc_ref[...] = jnp.zeros_like(acc_ref)
    p_ref[...] = jnp.zeros_like(p_ref)

    def scores(j):
        r0 = pl.multiple_of(j * tk, tk)
        return [_mm(k_ref[pl.ds(r0, tk), h * hd2:(h + 1) * hd2], qqs[h]) for h in range(nh)]

    def keep(ss, slot):
        for h in range(nh):
            s_ref[slot, h] = ss[h]

    def values(j):
        return [_mm(vt_ref[j, h * hd2:(h + 1) * hd2, :], p_ref[h]) for h in range(nh)]

    def softmax(slot, carry, masked):
        out = []
        for h in range(nh):
            m, l, _ = carry[h]
            s = s_ref[slot, h]
            if masked:
                key = lax.broadcasted_iota(i32, s.shape, 0)
                qry = lax.broadcasted_iota(i32, s.shape, 1) % tq
                s = jnp.where(key <= qry, s, NEG_INF)
            m_new = jnp.maximum(m, jnp.max(s, axis=0, keepdims=True))
            alpha = jnp.exp2(m - m_new)
            p = jnp.exp2(s - m_new)
            p_ref[h] = p.astype(bf16)
            out.append((m_new, alpha * l + jnp.sum(p, axis=0, keepdims=True), alpha))
        return tuple(out)

    def accumulate(pvs, carry):
        for h in range(nh):
            acc_ref[h] = carry[h][2] * acc_ref[h] + pvs[h]

    def body(j, carry):
        pvs = values(jnp.maximum(j - 1, 0))
        ss = scores(j + 1)
        new = softmax(j % 2, carry, False)
        keep(ss, (j + 1) % 2)
        accumulate(pvs, carry)
        return new

    keep(scores(0), 0)
    init = tuple((jnp.full((1, 2 * tq), NEG_INF, f32), jnp.zeros((1, 2 * tq), f32),
                  jnp.ones((1, 2 * tq), f32)) for _ in range(nh))
    carry = lax.fori_loop(0, qi, body, init)
    pvs = values(jnp.maximum(qi - 1, 0))
    last = softmax(qi % 2, carry, True)
    accumulate(pvs, carry)
    accumulate(values(qi), last)
    carry = last
    lam = _lam(lam_ref) + lam_init
    for h in range(nh):
        acc = acc_ref[h]
        l = carry[h][1]
        o = acc[:, :tq] / l[:, :tq] - lam * (acc[:, tq:] / l[:, tq:])
        o = o * lax.rsqrt(jnp.mean(o * o, axis=0, keepdims=True) + NORM_EPS) * sub_ref[...]
        o_ref[:, h * hd2:(h + 1) * hd2] = (o * (1.0 - lam_init)).T.astype(o_ref.dtype)


def _attn_sample_body(lam_init, pp, pt_ref, q_ref, kn_ref, vn_ref, mp_ref, mn_ref, lam_ref, sub_ref, *rest):
    k_refs, v_refs = rest[:pp], rest[pp:2 * pp]
    o_ref = rest[2 * pp]
    m_ref, l_ref, acc_ref = rest[2 * pp + 1:]
    j = pl.program_id(1)
    q = q_ref[...]

    @pl.when(j == 0)
    def _():
        m_ref[...] = jnp.full_like(m_ref, NEG_INF)
        l_ref[...] = jnp.zeros_like(l_ref)
        acc_ref[...] = jnp.zeros_like(acc_ref)

    def lane_fold(op, a):
        parts = [a[:, c:c + LANES] for c in range(0, a.shape[1], LANES)]
        while len(parts) > 1:
            parts = [op(parts[i], parts[i + 1]) for i in range(0, len(parts) - 1, 2)] + parts[len(parts) & ~1:]
        return parts[0]

    def update(kfs, vfs, mask):
        ss = [_nt(q, kf) + mask for kf in kfs]
        mx = ss[0] if ss[0].shape[1] < LANES else functools.reduce(jnp.maximum, [lane_fold(jnp.maximum, s) for s in ss])
        m = m_ref[...]
        m_new = jnp.maximum(m, jnp.max(mx, axis=1, keepdims=True))
        alpha = jnp.exp(m - m_new)
        ps = [jnp.exp(s - m_new) for s in ss]
        sm = ps[0] if ps[0].shape[1] < LANES else functools.reduce(jnp.add, [lane_fold(jnp.add, p) for p in ps])
        l_ref[...] = alpha * l_ref[...] + jnp.sum(sm, axis=1, keepdims=True)
        pv = functools.reduce(jnp.add, [_mm(p.astype(bf16), vf) for p, vf in zip(ps, vfs)])
        acc_ref[...] = alpha * acc_ref[...] + pv
        m_ref[...] = m_new

    update([r[...].astype(bf16) for r in k_refs], [r[...].astype(bf16) for r in v_refs], mp_ref[...])

    @pl.when(j == pl.num_programs(1) - 1)
    def _():
        update([kn_ref[...]], [vn_ref[...]], mn_ref[...])
        lam = _lam(lam_ref) + lam_init
        o_ref[...] = _attn_finish(acc_ref[...], l_ref[...], lam, lam_init, sub_ref)


def _oproj_body(x_ref, o_ref_in, g1_ref, w_ref, o_ref):
    o_ref[...] = x_ref[...] + g1_ref[...] * _mm(o_ref_in[...], w_ref[...])


def _attn_layer(x_all, dims, l, ia, W, cache_k, cache_v, page_table):
    n_all, d = x_all.shape
    n_p, n_s, seq, nb, n_t, batch = (dims[k] for k in ('n_p', 'n_s', 'seq', 'nb', 'n_t', 'batch'))
    past_len = dims['past_len']
    hd2 = 2 * HEAD_DIM
    lam_init = 0.8 - 0.6 * math.exp(-0.3 * l)

    half = HEAD_DIM // 2
    inv = jnp.power(ROPE_THETA, -jnp.arange(half, dtype=f32) * (2.0 / HEAD_DIM))

    def tables(pos):
        ang = pos.astype(f32)[:, None] * inv[None, :]
        cos, sin = jnp.cos(ang), jnp.sin(ang)
        return (jnp.concatenate([cos] * 4, axis=1), jnp.concatenate([-sin, sin, -sin, sin], axis=1))

    cos_p, sin_p = tables(jnp.arange(seq))
    cos_s, sin_s = tables(past_len + jnp.repeat(jnp.arange(n_t), nb))
    seg = (jnp.arange(d)[:, None] // HEAD_DIM == jnp.arange(d // HEAD_DIM)[None, :]).astype(bf16)
    consts = [W['norm_mix'][l].reshape(1, d), W['w_qkv'][ia].astype(bf16),
              jnp.tile(W['q_norm'][ia], d // HEAD_DIM).reshape(1, d),
              jnp.tile(W['k_norm'][ia], d // HEAD_DIM).reshape(1, d), seg, seg.T]
    modp, mods = dims['modp'], dims['mods']
    tk = TQ_ATTN
    kb, k_pr, v_pr, qt, vt = _rows_call(
        functools.partial(_qkv_body, True), tb=TB_PROMPT, nblk=n_p // TB_PROMPT, off=0, bps=seq // TB_PROMPT,
        row_ins=[x_all], mod_ins=[(modp, 0), (modp, 1)], seq_ins=[cos_p, sin_p], consts=consts,
        outs=[((n_all, d), bf16, 'row'), ((n_p, d), f32, 'own'), ((n_p, d), f32, 'own'),
              ((d, n_p), bf16, 'col'), ((n_p // tk, d, tk), bf16, 'blk3')], name="qkv_p")
    kb, k_sm, v_sm, qs, vb = _rows_call(
        functools.partial(_qkv_body, False), tb=n_s, nblk=1, off=n_p // n_s, bps=1,
        row_ins=[x_all], mod_ins=[(mods, 0), (mods, 1)], seq_ins=[cos_s, sin_s], consts=consts,
        outs=[((n_all, d), bf16, 'row'), ((n_s, d), f32, 'own'), ((n_s, d), f32, 'own'),
              ((n_s, d), bf16, 'own'), ((n_s, d), bf16, 'own')], prev=[kb], name="qkv_s")

    lam4 = jnp.stack([W['lambda_q1'][ia], W['lambda_k1'][ia], W['lambda_q2'][ia], W['lambda_k2'][ia]])
    sub = W['subln'][ia].reshape(1, hd2)
    sub_c = W['subln'][ia].reshape(hd2, 1)
    nq = seq // TQ_ATTN
    hw = HEADS_PER_STEP * hd2
    o_all = pl.pallas_call(
        functools.partial(_flash_body, lam_init), grid=(batch, N_HEADS // HEADS_PER_STEP, nq),
        in_specs=[pl.BlockSpec((hw, TQ_ATTN), lambda b, h, i: (h, b * nq + i)),
                  pl.BlockSpec((seq, hw), lambda b, h, i: (b, h)),
                  pl.BlockSpec((seq // tk, hw, tk), lambda b, h, i: (b, h, 0)),
                  pl.BlockSpec(lam4.shape, lambda b, h, i: (0, 0)),
                  pl.BlockSpec(sub_c.shape, lambda b, h, i: (0, 0))],
        out_specs=pl.BlockSpec((TQ_ATTN, hw), lambda b, h, i: (b * nq + i, h)),
        out_shape=jax.ShapeDtypeStruct((n_all, d), bf16),
        scratch_shapes=[pltpu.VMEM((HEADS_PER_STEP, hd2, 2 * TQ_ATTN), f32),
                        pltpu.VMEM((2, HEADS_PER_STEP, tk, 2 * TQ_ATTN), f32),
                        pltpu.VMEM((HEADS_PER_STEP, tk, 2 * TQ_ATTN), bf16)],
        compiler_params=_cparams("arbitrary", "arbitrary", "arbitrary"), name="flash_p")(
            qt, kb, vt, lam4, sub_c)

    na, n_phys, page, _, _ = cache_k.shape
    flat = page * N_HEADS
    ck = cache_k.reshape(na * n_phys, flat, hd2)
    cv = cache_v.reshape(na * n_phys, flat, hd2)
    n_pages = page_table.shape[1]
    pp = PAGES_PER_STEP if n_pages % PAGES_PER_STEP == 0 else 1
    rows = 2 * N_HEADS * n_t
    q5 = jnp.transpose(qs.reshape(n_t, nb, N_HEADS, 2, HEAD_DIM), (1, 3, 2, 0, 4))
    z = jnp.zeros_like(q5[:, 0])
    qm = jnp.stack([jnp.concatenate([q5[:, 0], z], -1), jnp.concatenate([z, q5[:, 1]], -1)], axis=1)
    qm = qm.reshape(nb, rows, hd2)

    def new_rows(a):
        return jnp.transpose(a.reshape(n_t, nb, N_HEADS, hd2), (1, 0, 2, 3)).reshape(nb, n_t * N_HEADS, hd2)

    kn, vn = new_rows(kb[n_p:]), new_rows(vb)
    r_h = (jnp.arange(rows) % (N_HEADS * n_t)) // n_t
    r_t = jnp.arange(rows) % n_t
    mask_p = jnp.where(r_h[:, None] == (jnp.arange(flat) % N_HEADS)[None, :], 0.0, NEG_INF).astype(f32)
    cn = jnp.arange(n_t * N_HEADS)
    mask_n = jnp.where((r_h[:, None] == (cn % N_HEADS)[None, :]) & ((cn // N_HEADS)[None, :] <= r_t[:, None]),
                       0.0, NEG_INF).astype(f32)
    base = ia * n_phys

    def seq_spec(shape):
        return pl.BlockSpec((None,) + shape, lambda s, j, pt: (s, 0, 0))

    def cst_spec(a):
        return pl.BlockSpec(a.shape, lambda s, j, pt: (0, 0))

    def page_spec(u):
        return pl.BlockSpec((None, flat, hd2), lambda s, j, pt, _u=u: (base + pt[s, j * pp + _u], 0, 0))

    gs = pltpu.PrefetchScalarGridSpec(
        num_scalar_prefetch=1, grid=(nb, n_pages // pp),
        in_specs=[seq_spec((rows, hd2)), seq_spec((n_t * N_HEADS, hd2)), seq_spec((n_t * N_HEADS, hd2)),
                  cst_spec(mask_p), cst_spec(mask_n), cst_spec(lam4), cst_spec(sub)]
                 + [page_spec(u) for u in range(pp)] + [page_spec(u) for u in range(pp)],
        out_specs=pl.BlockSpec((None, rows // 2, hd2), lambda s, j, pt: (s, 0, 0)),
        scratch_shapes=[pltpu.VMEM((rows, 1), f32), pltpu.VMEM((rows, 1), f32), pltpu.VMEM((rows, hd2), f32)])
    o_s = pl.pallas_call(
        functools.partial(_attn_sample_body, lam_init, pp), grid_spec=gs,
        out_shape=jax.ShapeDtypeStruct((nb, rows // 2, hd2), f32),
        compiler_params=_cparams("arbitrary", "arbitrary"), name="attn_s")(
            page_table, qm, kn, vn, mask_p, mask_n, lam4, sub, *([ck] * pp), *([cv] * pp))
    o_s = jnp.transpose(o_s.reshape(nb, N_HEADS, n_t, hd2), (2, 0, 1, 3)).reshape(n_s, d)
    o_all = lax.dynamic_update_slice(o_all, o_s.astype(bf16), (n_p, 0))

    (x_new,) = _both_groups(
        _oproj_body, dims, row_ins=[x_all, o_all], mod_ks=(2,), consts=[W['w_o'][ia].astype(bf16)],
        outs=[((n_all, d), f32, 'row')], name="oproj")

    def smp(a):
        return jnp.transpose(a.reshape(n_t, nb, N_HEADS, hd2), (1, 0, 2, 3))

    shp = (batch, seq, N_HEADS, hd2)
    return x_new, k_pr.reshape(shp), v_pr.reshape(shp), smp(k_sm), smp(v_sm)


def kernel(x_prompt, x_sample, cache_k, cache_v, state_pool, state_ssm_re, state_ssm_im, page_table, c_prompt, c_sample, w_ada, b_ada, norm_mix, norm_ff, w_pool, b_pool, ls_pool, ssm_lambda_re, ssm_lambda_im, ssm_log_dt, ssm_b_re, ssm_b_im, ssm_c_re, ssm_c_im, ssm_d, w_glu, b_glu, w_qkv, q_norm, k_norm, lambda_q1, lambda_k1, lambda_q2, lambda_k2, subln, w_o, w_router, b_router, w_gate_up, w_down):
    batch, seq, d = x_prompt.shape
    nb, n_t, _ = x_sample.shape
    depth = w_ada.shape[0]
    n_p, n_s = batch * seq, nb * n_t
    past_len = page_table.shape[1] * cache_k.shape[2]
    W = dict(norm_mix=norm_mix, norm_ff=norm_ff, w_pool=w_pool, b_pool=b_pool, ls_pool=ls_pool,
             ssm_lambda_re=ssm_lambda_re, ssm_lambda_im=ssm_lambda_im, ssm_log_dt=ssm_log_dt,
             ssm_b_re=ssm_b_re, ssm_b_im=ssm_b_im, ssm_c_re=ssm_c_re, ssm_c_im=ssm_c_im, ssm_d=ssm_d,
             w_glu=w_glu, b_glu=b_glu, w_qkv=w_qkv, q_norm=q_norm, k_norm=k_norm, lambda_q1=lambda_q1,
             lambda_k1=lambda_k1, lambda_q2=lambda_q2, lambda_k2=lambda_k2, subln=subln, w_o=w_o,
             w_router_t=w_router.T, b_router_c=b_router.reshape(-1, 1),
             w_gate_up=w_gate_up, w_down=w_down)

    ada = _ada_all(jnp.concatenate([c_prompt, c_sample], axis=0), w_ada, b_ada)
    x_all = x_prompt.reshape(n_p, d)
    x_smp, smp_off = jnp.transpose(x_sample, (1, 0, 2)).reshape(n_s, d), 0
    dims = dict(n_p=n_p, n_s=n_s, seq=seq, nb=nb, n_t=n_t, batch=batch, past_len=past_len)

    pools_p, pools_s, k_ps, v_ps, k_ss, v_ss = [], [], [], [], [], []
    re_ps, im_ps, re_ss, im_ss = [], [], [], []
    ip = iq = ia = 0
    for l in range(depth):
        chunks = jnp.transpose(ada[l].reshape(batch + nb, 6, d), (1, 0, 2))
        dims['modp'] = chunks[:, :batch, None, :]
        dims['mods'] = jnp.tile(chunks[:, batch:], (1, n_t, 1))[:, None]
        kind = l % N_MIXERS
        if kind == 0:
            x_all, pp_, ps_ = _pool_layer(x_all, x_smp, smp_off, dims, l, ip, W, state_pool)
            pools_p.append(pp_)
            pools_s.append(ps_)
            ip += 1
        elif kind == 1:
            x_all, rp, imp, rs, ims = _ssm_layer(x_all, dims, l, iq, W, state_ssm_re[iq], state_ssm_im[iq])
            re_ps.append(rp)
            im_ps.append(imp)
            re_ss.append(rs)
            im_ss.append(ims)
            iq += 1
        else:
            x_all, kp, vp, ks, vs = _attn_layer(x_all, dims, l, ia, W, cache_k, cache_v, page_table)
            k_ps.append(kp)
            v_ps.append(vp)
            k_ss.append(ks)
            v_ss.append(vs)
            ia += 1
        x_all = _moe_layer(x_all, dims, l, W, l == depth - 1)
        x_smp, smp_off = x_all, n_p // n_s

    y_prompt = x_all[0].reshape(batch, seq, d)
    y_sample = jnp.transpose(x_all[1].reshape(n_t, nb, d), (1, 0, 2))
    return (y_prompt, y_sample, jnp.stack(k_ps), jnp.stack(v_ps), jnp.stack(k_ss), jnp.stack(v_ss),
            jnp.stack(pools_p), jnp.stack(pools_s), jnp.stack(re_ps), jnp.stack(im_ps),
            jnp.stack(re_ss), jnp.stack(im_ss))
```

```python
import functools
import math

import jax
import jax.numpy as jnp
from jax import lax
from jax.experimental import pallas as pl
from jax.experimental.pallas import tpu as pltpu

f32 = jnp.float32
bf16 = jnp.bfloat16
i32 = jnp.int32

N_MIXERS = 3
POOL_WINDOWS = (2, 4, 8, 16)
POOL_HIST = 15
HALO = 16
SSM_GROUP_CH = 16
SSM_STATE = 64
SLAB_GROUPS = 8
N_HEADS = 8
HEAD_DIM = 64
ROPE_THETA = 10000.0
N_EXPERTS = 16
EXPERTS_PER_GROUP = 4
NORM_EPS = 1e-6
NEG_INF = -1e30
LOG2E = 1.4426950408889634
LANES = 128
VMEM_LIMIT = 48 * 1024 * 1024

TB_PROMPT = 512
TB_POOL = 256
T_SSM = 256
TQ_ATTN = 256
HEADS_PER_STEP = 2
TM_MOE = 256
H2_ROWS_FACTOR = 3
XS_PARTS = 2
INVPERM_UNROLL = 8
PAGES_PER_STEP = 8


def _cparams(*sem):
    return pltpu.CompilerParams(dimension_semantics=sem, vmem_limit_bytes=VMEM_LIMIT)


def _mm(a, b):
    return jnp.dot(a, b, preferred_element_type=f32)


def _nt(a, b):
    return lax.dot_general(a, b, (((1,), (1,)), ((), ())), preferred_element_type=f32)


def _split2(a):
    hi = a.astype(bf16)
    lo = (a - hi.astype(f32)).astype(bf16)
    return hi, lo


def _split3(a):
    hi = a.astype(bf16)
    r = a - hi.astype(f32)
    mid = r.astype(bf16)
    lo = (r - mid.astype(f32)).astype(bf16)
    return hi, mid, lo


def _norm_mod(x, g, sc, sh):
    ms = jnp.mean(x * x, axis=-1, keepdims=True)
    return x * lax.rsqrt(ms + NORM_EPS) * g * (1.0 + sc) + sh


def _gelu_tanh(y):
    return 0.5 * y * (1.0 + jnp.tanh(0.7978845608028654 * (y + 0.044715 * y * y * y)))


def _sigmoid(x):
    return 1.0 / (1.0 + jnp.exp(-x))


def _full_spec(a):
    nd = a.ndim
    return pl.BlockSpec(a.shape, lambda i, _n=nd: (0,) * _n)


def _rows_call(body, *, tb, nblk, off, bps, row_ins, mod_ins, seq_ins, consts, outs, prev=None,
               scratch=(), name):
    in_specs, args = [], []
    for a in row_ins:
        a, o = a if isinstance(a, tuple) else (a, off)
        in_specs.append(pl.BlockSpec((tb, a.shape[1]), lambda i, _o=o: (i + _o, 0)))
        args.append(a)
    for a, k in mod_ins:
        r, d = a.shape[2], a.shape[3]
        in_specs.append(pl.BlockSpec((None, None, r, d), lambda i, _k=k: (_k, i // bps, 0, 0)))
        args.append(a)
    for a in seq_ins:
        in_specs.append(pl.BlockSpec((tb, a.shape[1]), lambda i: (i % bps, 0)))
        args.append(a)
    for a in consts:
        in_specs.append(_full_spec(a))
        args.append(a)
    n_real = len(args)
    aliases = {}
    if prev is not None:
        for j, a in enumerate(prev):
            in_specs.append(pl.BlockSpec(memory_space=pl.ANY))
            args.append(a)
            aliases[n_real + j] = j
    out_specs, out_shapes = [], []
    for shape, dtype, kind in outs:
        if kind == 'row':
            out_specs.append(pl.BlockSpec((tb, shape[1]), lambda i: (i + off, 0)))
        elif kind == 'own':
            out_specs.append(pl.BlockSpec((tb, shape[1]), lambda i: (i, 0)))
        elif kind == 'col':
            out_specs.append(pl.BlockSpec((shape[0], tb), lambda i: (0, i + off)))
        elif kind == 'blk3':
            out_specs.append(pl.BlockSpec((tb // shape[2], shape[1], shape[2]), lambda i: (i, 0, 0)))
        else:
            out_specs.append(pl.BlockSpec(shape, lambda i, _n=len(shape): (0,) * _n))
        out_shapes.append(jax.ShapeDtypeStruct(shape, dtype))
    n_prev = 0 if prev is None else len(prev)

    def wrapped(*refs):
        body(*refs[:n_real], *refs[n_real + n_prev:])

    res = pl.pallas_call(
        wrapped, grid=(nblk,), in_specs=in_specs, out_specs=out_specs, out_shape=out_shapes,
        scratch_shapes=list(scratch), input_output_aliases=aliases,
        compiler_params=_cparams("arbitrary"), name=name)(*args)
    return list(res)


def _both_groups(body, dims, *, row_ins, mod_ks, consts, outs, name, seq_ins_p=(), seq_ins_s=()):
    modp, mods = dims['modp'], dims['mods']
    n_p, n_s, seq = dims['n_p'], dims['n_s'], dims['seq']
    res = _rows_call(body, tb=TB_PROMPT, nblk=n_p // TB_PROMPT, off=0, bps=seq // TB_PROMPT,
                     row_ins=row_ins, mod_ins=[(modp, k) for k in mod_ks], seq_ins=list(seq_ins_p),
                     consts=consts, outs=outs, name=name + "_p")
    res = _rows_call(body, tb=n_s, nblk=1, off=n_p // n_s, bps=1,
                     row_ins=row_ins, mod_ins=[(mods, k) for k in mod_ks], seq_ins=list(seq_ins_s),
                     consts=consts, outs=outs, prev=res, name=name + "_s")
    return res


def _ada_body(c_ref, w_ref, b_ref, o_ref):
    c = c_ref[...]
    cond = c * _sigmoid(c)
    c_hi, c_lo = _split2(cond)
    w_hi, w_lo = _split2(w_ref[...])
    o_ref[...] = _mm(c_hi, w_hi) + _mm(c_hi, w_lo) + _mm(c_lo, w_hi) + b_ref[...]


def _ada_all(c_all, w_ada, b_ada):
    depth, d, d6 = w_ada.shape
    n = c_all.shape[0]
    nj = d6 // d
    return pl.pallas_call(
        _ada_body, grid=(depth, nj),
        in_specs=[pl.BlockSpec((n, d), lambda l, j: (0, 0)),
                  pl.BlockSpec((None, d, d), lambda l, j: (l, 0, j)),
                  pl.BlockSpec((None, 1, d), lambda l, j: (l, 0, j))],
        out_specs=pl.BlockSpec((None, n, d), lambda l, j: (l, 0, j)),
        out_shape=jax.ShapeDtypeStruct((depth, n, d6), f32),
        compiler_params=_cparams("arbitrary", "arbitrary"), name="ada")(
            c_all, w_ada, b_ada.reshape(depth, 1, d6))


def _router_body(x_ref, sh_ref, sc_ref, nw_ref, wrt_ref, br_ref, tri_ref, cnt0_ref,
                 h_ref, ids_ref, gates_ref, rank_ref, cnt_ref, carry_ref):
    @pl.when(pl.program_id(0) == 0)
    def _():
        carry_ref[...] = cnt0_ref[...]

    h = _norm_mod(x_ref[...], nw_ref[...], sc_ref[...], sh_ref[...])
    h_ref[...] = h.astype(bf16)
    h_hi, h_lo = _split2(h)
    w_hi, w_lo = _split2(wrt_ref[...])
    lg = _nt(w_hi, h_hi) + _nt(w_hi, h_lo) + _nt(w_lo, h_hi) + br_ref[...]
    e = jnp.exp(lg - jnp.max(lg, axis=0, keepdims=True))
    best = None
    for g in range(N_EXPERTS // EXPERTS_PER_GROUP):
        v = [e[EXPERTS_PER_GROUP * g + j:EXPERTS_PER_GROUP * g + j + 1, :] for j in range(EXPERTS_PER_GROUP)]
        t1 = jnp.maximum(jnp.maximum(v[0], v[1]), jnp.maximum(v[2], v[3]))
        i1 = jnp.where(v[0] == t1, 0, jnp.where(v[1] == t1, 1, jnp.where(v[2] == t1, 2, 3)))
        w = [jnp.where(i1 == j, -1.0, v[j]) for j in range(EXPERTS_PER_GROUP)]
        t2 = jnp.maximum(jnp.maximum(w[0], w[1]), jnp.maximum(w[2], w[3]))
        i2 = jnp.where(w[0] == t2, 0, jnp.where(w[1] == t2, 1, jnp.where(w[2] == t2, 2, 3)))
        cand = (t1 + t2, t1, t2, i1 + EXPERTS_PER_GROUP * g, i2 + EXPERTS_PER_GROUP * g)
        if best is None:
            best = cand
        else:
            upd = cand[0] > best[0]
            best = tuple(jnp.where(upd, c, b) for c, b in zip(cand, best))
    s, t1, t2, i1, i2 = best
    ids_ref[0:1, :] = i1.astype(i32)
    ids_ref[1:2, :] = i2.astype(i32)
    gates_ref[0:1, :] = t1 / s
    gates_ref[1:2, :] = t2 / s
    eidx = lax.broadcasted_iota(i32, lg.shape, 0)
    oh1 = (eidx == i1).astype(f32)
    oh2 = (eidx == i2).astype(f32)
    tri = tri_ref[...]
    ex1 = _mm(oh1.astype(bf16), tri)
    ex2 = _mm(oh2.astype(bf16), tri)
    carry = carry_ref[...]
    tot1 = jnp.sum(oh1, axis=1, keepdims=True)
    tot2 = jnp.sum(oh2, axis=1, keepdims=True)
    rank_ref[0:1, :] = jnp.sum(oh1 * (carry + ex1), axis=0, keepdims=True).astype(i32)
    rank_ref[1:2, :] = jnp.sum(oh2 * (carry + tot1 + ex2), axis=0, keepdims=True).astype(i32)
    carry_ref[...] = carry + tot1 + tot2
    cnt_ref[...] = carry_ref[...]


def _invperm_body(dest_ref, src_ref):
    n_tok = dest_ref.shape[0] // 2

    spread = (1 << (n_tok.bit_length() - 1)) - 1

    def fill(i, c):
        src_ref[i] = i & spread
        return c

    lax.fori_loop(0, src_ref.shape[0], fill, 0, unroll=INVPERM_UNROLL)

    def place(t, c):
        src_ref[dest_ref[t]] = t
        src_ref[dest_ref[n_tok + t]] = t
        return c

    lax.fori_loop(0, n_tok, place, 0, unroll=INVPERM_UNROLL)


def _moe_body(bounds, te_ref, nu_ref, *refs):
    n_parts = len(bounds) - 1
    x_refs = refs[:n_parts]
    wgu_ref, wdn_ref, o_ref, wgu_b, wdn_b = refs[n_parts:]
    i = pl.program_id(0)
    f = wdn_ref.shape[0]

    @pl.when(jnp.logical_or(i == 0, te_ref[i] != te_ref[jnp.maximum(i - 1, 0)]))
    def _():
        wgu_b[...] = wgu_ref[...].astype(bf16)
        wdn_b[...] = wdn_ref[...].astype(bf16)

    for k in range(n_parts):
        @pl.when(jnp.logical_and(i < nu_ref[0], jnp.logical_and(i >= bounds[k], i < bounds[k + 1])))
        def _(k=k):
            gu = _mm(x_refs[k][...], wgu_b[...])
            g = gu[:, :f]
            act = g * _sigmoid(g) * gu[:, f:]
            o_ref[...] = _mm(act.astype(bf16), wdn_b[...]).astype(bf16)

    @pl.when(i >= nu_ref[0])
    def _():
        o_ref[...] = jnp.zeros_like(o_ref)


def _moe_grouped(xs, wgu, wdn, l, tile_expert, n_used):
    d = xs[0].shape[1]
    f2 = wgu.shape[3]
    f = wdn.shape[2]
    bounds = [0]
    for x in xs:
        bounds.append(bounds[-1] + x.shape[0] // TM_MOE)

    def part_spec(k):
        return pl.BlockSpec((TM_MOE, d), lambda i, te, nu: (jnp.clip(i - bounds[k], 0, bounds[k + 1] - bounds[k] - 1), 0))

    gs = pltpu.PrefetchScalarGridSpec(
        num_scalar_prefetch=2, grid=(bounds[-1],),
        in_specs=[part_spec(k) for k in range(len(xs))]
                 + [pl.BlockSpec((None, None, d, f2), lambda i, te, nu: (l, te[i], 0, 0)),
                    pl.BlockSpec((None, None, f, d), lambda i, te, nu: (l, te[i], 0, 0))],
        out_specs=pl.BlockSpec((TM_MOE, d), lambda i, te, nu: (i, 0)),
        scratch_shapes=[pltpu.VMEM((d, f2), bf16), pltpu.VMEM((f, d), bf16)])
    return pl.pallas_call(functools.partial(_moe_body, tuple(bounds)), grid_spec=gs,
                          out_shape=jax.ShapeDtypeStruct((bounds[-1] * TM_MOE, d), bf16),
                          compiler_params=_cparams("arbitrary"), name="moe_experts")(
                              tile_expert, n_used, *xs, wgu, wdn)


def _combine_body(x_ref, oa_ref, ob_ref, gt_ref, g2_ref, o_ref):
    gt = gt_ref[...]
    y = gt[:, 0:1] * oa_ref[...].astype(f32) + gt[:, 1:2] * ob_ref[...].astype(f32)
    o_ref[...] = x_ref[...] + g2_ref[...] * y


def _tri(n):
    return (jnp.arange(n)[:, None] < jnp.arange(n)[None, :]).astype(bf16)


def _moe_layer(x_all, dims, l, W, last):
    n_all, d = x_all.shape
    n_p, n_s, seq = dims['n_p'], dims['n_s'], dims['seq']
    modp, mods = dims['modp'], dims['mods']
    consts = [W['norm_ff'][l].reshape(1, d), W['w_router_t'], W['b_router_c']]
    outs = [((H2_ROWS_FACTOR * n_all, d), bf16, 'row'), ((2, n_all), i32, 'col'), ((2, n_all), f32, 'col'),
            ((2, n_all), i32, 'col'), ((N_EXPERTS, 1), f32, 'const')]
    scratch = [pltpu.VMEM((N_EXPERTS, 1), f32)]
    res = _rows_call(_router_body, tb=TB_PROMPT, nblk=n_p // TB_PROMPT, off=0, bps=seq // TB_PROMPT,
                     row_ins=[x_all], mod_ins=[(modp, 3), (modp, 4)], seq_ins=[],
                     consts=consts + [_tri(TB_PROMPT), jnp.zeros((N_EXPERTS, 1), f32)],
                     outs=outs, scratch=scratch, name="router_p")
    h2, ids, gates, rank, cnt = _rows_call(
        _router_body, tb=n_s, nblk=1, off=n_p // n_s, bps=1,
        row_ins=[x_all], mod_ins=[(mods, 3), (mods, 4)], seq_ins=[],
        consts=consts + [_tri(n_s), res[4]], outs=outs, prev=res[:4], scratch=scratch, name="router_s")

    n_tiles = (2 * n_all + N_EXPERTS * (TM_MOE - 1)) // TM_MOE + 1
    counts = cnt[:, 0].astype(i32)
    padded = (counts + TM_MOE - 1) // TM_MOE * TM_MOE
    pend = jnp.cumsum(padded)
    n_used = (pend[-1:] // TM_MOE).astype(i32)
    tile_start = jnp.arange(n_tiles, dtype=i32) * TM_MOE
    tile_expert = jnp.minimum(jnp.sum((pend[None, :] <= tile_start[:, None]).astype(i32), axis=1),
                              N_EXPERTS - 1)
    pstart = (pend - padded).astype(i32)
    sel = ids[:, :, None] == jnp.arange(N_EXPERTS, dtype=i32)[None, None, :]
    dest = rank + jnp.sum(jnp.where(sel, pstart[None, None, :], 0), axis=2)
    smem = pl.BlockSpec(memory_space=pltpu.SMEM)
    row_src = pl.pallas_call(
        _invperm_body, in_specs=[smem], out_specs=smem,
        out_shape=jax.ShapeDtypeStruct((n_tiles * TM_MOE,), i32), name="moe_invperm")(dest.reshape(-1))
    def rows(a, idx):
        return a.at[idx].get(mode='promise_in_bounds')

    row_src = jnp.minimum(row_src, n_all - 1)
    cuts = [k * n_tiles // XS_PARTS * TM_MOE for k in range(XS_PARTS + 1)]
    xs = [rows(h2, row_src[cuts[k]:cuts[k + 1]]) for k in range(XS_PARTS)]
    osort = _moe_grouped(xs, W['w_gate_up'], W['w_down'], l, tile_expert, n_used)
    oa = rows(osort, dest[0])
    ob = rows(osort, dest[1])
    gt = gates.T
    if not last:
        (x_new,) = _both_groups(
            _combine_body, dims, row_ins=[x_all, oa, ob, gt], mod_ks=(5,), consts=[],
            outs=[((n_all, d), f32, 'row')], name="combine")
        return x_new
    (y_p,) = _rows_call(_combine_body, tb=TB_PROMPT, nblk=n_p // TB_PROMPT, off=0, bps=seq // TB_PROMPT,
                        row_ins=[x_all, oa, ob, gt], mod_ins=[(modp, 5)], seq_ins=[], consts=[],
                        outs=[((n_p, d), f32, 'own')], name="combine_last_p")
    (y_s,) = _rows_call(_combine_body, tb=n_s, nblk=1, off=n_p // n_s, bps=1,
                        row_ins=[x_all, oa, ob, gt], mod_ins=[(mods, 5)], seq_ins=[], consts=[],
                        outs=[((n_s, d), f32, 'own')], name="combine_last_s")
    return y_p, y_s


def _pool_tail(pooled_slabs, w_ref, b_ref, ls_ref):
    ys = [_mm(p.astype(bf16), w_ref[g]) for g, p in enumerate(pooled_slabs)]
    return (jnp.concatenate(ys, axis=1) + b_ref[...]) * ls_ref[...]


def _pool_prompt_body(bps, x_ref, xh_ref, sh_ref, sc_ref, g1_ref, nw_ref, w_ref, b_ref, ls_ref,
                      o_ref, st_ref, ext_ref):
    tb = x_ref.shape[0]
    gw = w_ref.shape[1]
    blk = pl.program_id(0) % bps
    x = x_ref[...]
    nw, sc, sh = nw_ref[...], sc_ref[...], sh_ref[...]
    h = _norm_mod(x, nw, sc, sh)
    hh = _norm_mod(xh_ref[...], nw, sc, sh)
    ext_ref[0:HALO, :] = jnp.where(blk == 0, 0.0, hh)
    ext_ref[HALO:, :] = h
    pos = blk * tb + lax.broadcasted_iota(i32, (tb, 1), 0)
    slabs = []
    for g, wdw in enumerate(POOL_WINDOWS):
        lo, hi = g * gw, (g + 1) * gw
        acc = h[:, lo:hi]
        for j in range(1, wdw):
            acc = acc + ext_ref[HALO - j:HALO - j + tb, lo:hi]
        cnt = jnp.minimum(pos + 1, wdw).astype(f32)
        slabs.append(acc / cnt - h[:, lo:hi])
    y = _pool_tail(slabs, w_ref, b_ref, ls_ref)
    o_ref[...] = x + g1_ref[...] * y
    st_ref[...] = h[tb - HALO:, :]


def _pool_sample_body(n_t, pos0, x_ref, past_ref, sh_ref, sc_ref, g1_ref, nw_ref, w_ref, b_ref, ls_ref,
                      o_ref, h_ref):
    nb = past_ref.shape[1]
    gw = w_ref.shape[1]
    x = x_ref[...]
    h = _norm_mod(x, nw_ref[...], sc_ref[...], sh_ref[...])
    h_ref[...] = h

    def ext(r, lo, hi):
        if r < POOL_HIST:
            return past_ref[r][:, lo:hi]
        t = r - POOL_HIST
        return h[t * nb:(t + 1) * nb, lo:hi]

    slabs = []
    for g, wdw in enumerate(POOL_WINDOWS):
        lo, hi = g * gw, (g + 1) * gw
        rows = []
        for t in range(n_t):
            acc = ext(POOL_HIST + t, lo, hi)
            for j in range(1, wdw):
                acc = acc + ext(POOL_HIST + t - j, lo, hi)
            cnt = float(min(pos0 + t + 1, wdw))
            rows.append(acc / cnt - ext(POOL_HIST + t, lo, hi))
        slabs.append(jnp.concatenate(rows, axis=0))
    y = _pool_tail(slabs, w_ref, b_ref, ls_ref)
    o_ref[...] = x + g1_ref[...] * y


def _pool_layer(x_all, x_smp, smp_off, dims, l, ip, W, state_pool):
    d = x_all.shape[1]
    n_p, n_s, seq, nb, n_t, batch = (dims[k] for k in ('n_p', 'n_s', 'seq', 'nb', 'n_t', 'batch'))
    n_all = n_p + n_s
    modp, mods = dims['modp'], dims['mods']
    bps = seq // TB_POOL
    consts = [W['norm_mix'][l].reshape(1, d), W['w_pool'][ip].astype(bf16),
              W['b_pool'][ip].reshape(1, d), W['ls_pool'][ip].reshape(1, d)]

    def mspec(k):
        return pl.BlockSpec((None, None, 1, d), lambda i, _k=k: (_k, i // bps, 0, 0))

    ratio = TB_POOL // HALO
    x_new, st = pl.pallas_call(
        functools.partial(_pool_prompt_body, bps), grid=(n_p // TB_POOL,),
        in_specs=[pl.BlockSpec((TB_POOL, d), lambda i: (i, 0)),
                  pl.BlockSpec((HALO, d), lambda i: (jnp.maximum(i * ratio - 1, 0), 0)),
                  mspec(0), mspec(1), mspec(2)] + [_full_spec(c) for c in consts],
        out_specs=[pl.BlockSpec((TB_POOL, d), lambda i: (i, 0)),
                   pl.BlockSpec((None, HALO, d), lambda i: (i // bps, 0, 0))],
        out_shape=[jax.ShapeDtypeStruct((n_all, d), f32), jax.ShapeDtypeStruct((batch, HALO, d), f32)],
        scratch_shapes=[pltpu.VMEM((TB_POOL + HALO, d), f32)],
        compiler_params=_cparams("arbitrary"), name="pool_p")(
            x_all, x_all, modp, modp, modp, *consts)
    pool_prompt = st[:, HALO - POOL_HIST:, :]

    past = jnp.transpose(state_pool[ip], (1, 0, 2))
    off = n_p // n_s

    def sspec(k):
        return pl.BlockSpec((None, None, n_s, d), lambda i, _k=k: (_k, 0, 0, 0))

    def body(x_ref, past_ref, sh, sc, g1, nw, w, b, ls, prev_ref, o_ref, h_ref):
        _pool_sample_body(n_t, dims['past_len'], x_ref, past_ref, sh, sc, g1, nw, w, b, ls, o_ref, h_ref)

    x_new, h_s = pl.pallas_call(
        body, grid=(1,),
        in_specs=[pl.BlockSpec((n_s, d), lambda i: (smp_off, 0)), _full_spec(past),
                  sspec(0), sspec(1), sspec(2)] + [_full_spec(c) for c in consts]
                 + [pl.BlockSpec(memory_space=pl.ANY)],
        out_specs=[pl.BlockSpec((n_s, d), lambda i: (off, 0)), pl.BlockSpec((n_s, d), lambda i: (0, 0))],
        out_shape=[jax.ShapeDtypeStruct((n_all, d), f32), jax.ShapeDtypeStruct((n_s, d), f32)],
        input_output_aliases={5 + len(consts): 0},
        compiler_params=_cparams("arbitrary"), name="pool_s")(
            x_smp, past, mods, mods, mods, *consts, x_new)
    ext = jnp.concatenate([past, h_s.reshape(n_t, nb, d)], axis=0)
    pool_sample = jnp.transpose(ext[-POOL_HIST:], (1, 0, 2))
    return x_new, pool_prompt, pool_sample


def _ssm_param_body(lr_ref, li_ref, ldt_ref, apr_ref, api_ref, kr_ref, ki_ref):
    lr, li = lr_ref[...], li_ref[...]
    dt = jnp.exp(ldt_ref[...])
    mag = jnp.exp(lr * dt)
    a_r, a_i = mag * jnp.cos(li * dt), mag * jnp.sin(li * dt)
    den = lr * lr + li * li
    nr, ni = a_r - 1.0, a_i
    kr_ref[...] = (nr * lr + ni * li) / den
    ki_ref[...] = (ni * lr - nr * li) / den
    pr, pi_ = a_r, a_i
    for k in range(8):
        apr_ref[k] = pr
        api_ref[k] = pi_
        pr, pi_ = pr * a_r - pi_ * a_i, pr * a_i + pi_ * a_r


def _ssm_tables(W, iq):
    g, p = W['ssm_lambda_re'][iq].shape
    c = SSM_GROUP_CH
    ns = g // SLAB_GROUPS
    sl = SLAB_GROUPS
    shp = [jax.ShapeDtypeStruct((8, g, p), f32)] * 2 + [jax.ShapeDtypeStruct((g, p), f32)] * 2
    apr, api, k_r, k_i = pl.pallas_call(_ssm_param_body, out_shape=shp, name="ssm_params")(
        W['ssm_lambda_re'][iq], W['ssm_lambda_im'][iq], W['ssm_log_dt'][iq].reshape(g, 1))
    b_re, b_im = W['ssm_b_re'][iq], W['ssm_b_im'][iq]
    bb_re = k_r[..., None] * b_re - k_i[..., None] * b_im
    bb_im = k_r[..., None] * b_im + k_i[..., None] * b_re
    eye = jnp.eye(sl, dtype=f32)

    def b_slab(bb):
        b4 = bb.reshape(ns, sl, p, c)
        return jnp.einsum('sgpc,gh->sgchp', b4, eye).reshape(ns, sl * c, sl * p)

    def c_slab(cc):
        c4 = cc.reshape(ns, sl, c, p)
        return jnp.einsum('sgcp,gh->shpgc', c4, eye).reshape(ns, sl * p, sl * c)

    bmat = jnp.concatenate([b_slab(bb_re), b_slab(bb_im)], axis=2).astype(bf16)
    cmat = jnp.concatenate([c_slab(W['ssm_c_re'][iq]), -c_slab(W['ssm_c_im'][iq])], axis=1).astype(bf16)

    def lay(a):
        return jnp.transpose(a.reshape(8, ns, sl * p), (1, 0, 2))

    apw = jnp.stack([lay(apr), lay(api)], axis=1)
    return bmat, cmat, apw


def _ssm_out(x, h, y, g1_ref, dsk_ref, wglu_ref, bglu_ref):
    d = x.shape[1]
    z = _gelu_tanh(y + dsk_ref[...] * h)
    gu = _mm(z.astype(bf16), wglu_ref[...]) + bglu_ref[...]
    return x + g1_ref[...] * (gu[:, :d] * _sigmoid(gu[:, d:]))


def _ssm_prompt_body(x_ref, sh_ref, sc_ref, g1_ref, nw_ref, b_ref, c_ref, t2_ref, dsk_ref,
                     wglu_ref, bglu_ref, o_ref, st_ref, xs_ref, hs_ref, carry_ref):
    t = x_ref.shape[0]
    ns = b_ref.shape[0]
    sw = b_ref.shape[1]
    half = b_ref.shape[2] // 2
    hc = half // LANES

    @pl.when(pl.program_id(1) == 0)
    def _():
        carry_ref[...] = jnp.zeros_like(carry_ref)

    x = x_ref[...]
    h = _norm_mod(x, nw_ref[...], sc_ref[...], sh_ref[...])
    hb = h.astype(bf16)
    ys = []
    for s in range(ns):
        xx = _mm(hb[:, s * sw:(s + 1) * sw], b_ref[s])
        for c in range(2 * hc):
            xs_ref[c] = xx[:, c * LANES:(c + 1) * LANES]

        for c in range(hc):
            ar = jnp.broadcast_to(t2_ref[s, 0, 0:1, c * LANES:(c + 1) * LANES], (8, LANES))
            ai = jnp.broadcast_to(t2_ref[s, 1, 0:1, c * LANES:(c + 1) * LANES], (8, LANES))
            for base in range(0, t, 64):
                hr = xs_ref[c, pl.ds(base, 8, stride=8), :]
                hi = xs_ref[hc + c, pl.ds(base, 8, stride=8), :]
                for i in range(1, 8):
                    rows = pl.ds(base + i, 8, stride=8)
                    hr, hi = (ar * hr - ai * hi + xs_ref[c, rows, :], ar * hi + ai * hr + xs_ref[hc + c, rows, :])
                    xs_ref[c, rows, :] = hr
                    xs_ref[hc + c, rows, :] = hi

        def grp(j, car, s=s):
            cr, ci = car
            r0 = pl.multiple_of(j * 8, 8)
            xr = jnp.concatenate([xs_ref[c, pl.ds(r0, 8), :] for c in range(hc)], axis=1)
            xi = jnp.concatenate([xs_ref[hc + c, pl.ds(r0, 8), :] for c in range(hc)], axis=1)
            pr = t2_ref[s, 0]
            pi_ = t2_ref[s, 1]
            hr = xr + pr * cr - pi_ * ci
            hi = xi + pr * ci + pi_ * cr
            hs_ref[pl.ds(r0, 8), 0:half] = hr
            hs_ref[pl.ds(r0, 8), half:] = hi
            return (jnp.broadcast_to(hr[7:8, :], hr.shape), jnp.broadcast_to(hi[7:8, :], hi.shape))

        cr, ci = lax.fori_loop(0, t // 8, grp, (carry_ref[s, :, 0:half], carry_ref[s, :, half:]))
        carry_ref[s, :, 0:half] = cr
        carry_ref[s, :, half:] = ci
        ys.append(_mm(hs_ref[...].astype(bf16), c_ref[s]))
        st_ref[s:s + 1, :] = jnp.concatenate([cr[0:1, :], ci[0:1, :]], axis=1)
    y = jnp.concatenate(ys, axis=1)
    o_ref[...] = _ssm_out(x, h, y, g1_ref, dsk_ref, wglu_ref, bglu_ref)


def _ssm_sample_body(n_t, x_ref, s0_ref, sh_ref, sc_ref, g1_ref, nw_ref, b_ref, c_ref, t2_ref, dsk_ref,
                     wglu_ref, bglu_ref, prev_ref, o_ref, so_ref, y_ref):
    ns = b_ref.shape[0]
    sw = b_ref.shape[1]
    half = b_ref.shape[2] // 2
    nb = s0_ref.shape[1]
    x = x_ref[...]
    h = _norm_mod(x, nw_ref[...], sc_ref[...], sh_ref[...])
    hb = h.astype(bf16)
    for s in range(ns):
        s_r = s0_ref[s, :, 0:half]
        s_i = s0_ref[s, :, half:]
        ar = t2_ref[s, 0, 0:1, :]
        ai = t2_ref[s, 1, 0:1, :]
        for t in range(n_t):
            xx = _mm(hb[t * nb:(t + 1) * nb, s * sw:(s + 1) * sw], b_ref[s])
            s_r, s_i = ar * s_r - ai * s_i + xx[:, 0:half], ar * s_i + ai * s_r + xx[:, half:]
            st = jnp.concatenate([s_r, s_i], axis=1)
            y_ref[t * nb:(t + 1) * nb, s * sw:(s + 1) * sw] = _mm(st.astype(bf16), c_ref[s])
        so_ref[s] = jnp.concatenate([s_r, s_i], axis=1)
    o_ref[...] = _ssm_out(x, h, y_ref[...], g1_ref, dsk_ref, wglu_ref, bglu_ref)


def _ssm_layer(x_all, dims, l, iq, W, st_re, st_im):
    n_all, d = x_all.shape
    n_p, n_s, seq, nb, n_t, batch = (dims[k] for k in ('n_p', 'n_s', 'seq', 'nb', 'n_t', 'batch'))
    modp, mods = dims['modp'], dims['mods']
    bmat, cmat, t2 = _ssm_tables(W, iq)
    ns = bmat.shape[0]
    sp = bmat.shape[2]
    g, p = W['ssm_lambda_re'][iq].shape
    sl = SLAB_GROUPS
    tail = [W['ssm_d'][iq].reshape(1, d), W['w_glu'][iq].astype(bf16), W['b_glu'][iq].reshape(1, 2 * d)]
    nw = W['norm_mix'][l].reshape(1, d)
    nchunk = seq // T_SSM

    def mspec(k):
        return pl.BlockSpec((None, None, 1, d), lambda b, c, _k=k: (_k, b, 0, 0))

    def cspec(a):
        return pl.BlockSpec(a.shape, lambda b, c, _n=a.ndim: (0,) * _n)

    consts = [nw, bmat, cmat, t2] + tail
    x_new, st = pl.pallas_call(
        _ssm_prompt_body, grid=(batch, nchunk),
        in_specs=[pl.BlockSpec((T_SSM, d), lambda b, c: (b * nchunk + c, 0)), mspec(0), mspec(1), mspec(2)]
                 + [cspec(a) for a in consts],
        out_specs=[pl.BlockSpec((T_SSM, d), lambda b, c: (b * nchunk + c, 0)),
                   pl.BlockSpec((None, ns, sp), lambda b, c: (b, 0, 0))],
        out_shape=[jax.ShapeDtypeStruct((n_all, d), f32), jax.ShapeDtypeStruct((batch, ns, sp), f32)],
        scratch_shapes=[pltpu.VMEM((sp // LANES, T_SSM, LANES), f32), pltpu.VMEM((T_SSM, sp), f32),
                        pltpu.VMEM((ns, 8, sp), f32)],
        compiler_params=_cparams("arbitrary", "arbitrary"), name="ssm_p")(
            x_all, modp, modp, modp, *consts)

    def unslab(a):
        a5 = a.reshape(a.shape[0], ns, 2, sl, p)
        return a5[:, :, 0].reshape(-1, g, p), a5[:, :, 1].reshape(-1, g, p)

    re_p, im_p = unslab(st)

    def slab(a):
        return a.reshape(a.shape[0], ns, sl * p)

    s0 = jnp.transpose(jnp.concatenate([slab(st_re), slab(st_im)], axis=2), (1, 0, 2))
    off = n_p // n_s

    def sspec(k):
        return pl.BlockSpec((None, None, n_s, d), lambda i, _k=k: (_k, 0, 0, 0))

    consts_s = [nw, bmat, cmat, t2] + tail
    x_new, so = pl.pallas_call(
        functools.partial(_ssm_sample_body, n_t), grid=(1,),
        in_specs=[pl.BlockSpec((n_s, d), lambda i: (off, 0)), _full_spec(s0), sspec(0), sspec(1), sspec(2)]
                 + [_full_spec(a) for a in consts_s] + [pl.BlockSpec(memory_space=pl.ANY)],
        out_specs=[pl.BlockSpec((n_s, d), lambda i: (off, 0)), _full_spec(s0)],
        out_shape=[jax.ShapeDtypeStruct((n_all, d), f32), jax.ShapeDtypeStruct(s0.shape, f32)],
        scratch_shapes=[pltpu.VMEM((n_s, d), f32)],
        input_output_aliases={5 + len(consts_s): 0},
        compiler_params=_cparams("arbitrary"), name="ssm_s")(
            x_all, s0, mods, mods, mods, *consts_s, x_new)
    re_s, im_s = unslab(jnp.transpose(so, (1, 0, 2)))
    return x_new, re_p, im_p, re_s, im_s


def _qkv_body(prompt, x_ref, sh_ref, sc_ref, cos_ref, sin_ref, nw_ref, w_ref, qn_ref, kn_ref, seg_ref, segt_ref,
              kb_ref, k_ref, v_ref, a_ref, b_ref):
    d = x_ref.shape[1]
    h = _norm_mod(x_ref[...], nw_ref[...], sc_ref[...], sh_ref[...])
    qkv = _mm(h.astype(bf16), w_ref[...])
    rep = d // LANES
    cos = jnp.concatenate([cos_ref[...]] * rep, axis=1)
    sin = jnp.concatenate([sin_ref[...]] * rep, axis=1)
    lane_lo = (lax.broadcasted_iota(i32, (1, d), 1) % HEAD_DIM) < (HEAD_DIM // 2)
    seg, segt = seg_ref[...], segt_ref[...]

    def norm_rope(t, g):
        s_hi, s_lo = _split2(t * t)
        ms = (_mm(s_hi, seg) + _mm(s_lo, seg)) * (1.0 / HEAD_DIM)
        r_hi, r_mid, r_lo = _split3(lax.rsqrt(ms + NORM_EPS))
        rf = _mm(r_hi, segt) + _mm(r_mid, segt) + _mm(r_lo, segt)
        tn = t * rf * g
        sw = jnp.where(lane_lo, pltpu.roll(tn, d - HEAD_DIM // 2, 1), pltpu.roll(tn, HEAD_DIM // 2, 1))
        return tn * cos + sw * sin

    q = norm_rope(qkv[:, :d], qn_ref[...])
    k = norm_rope(qkv[:, d:2 * d], kn_ref[...])
    v = qkv[:, 2 * d:]
    k_ref[...] = k
    v_ref[...] = v
    kb_ref[...] = k.astype(bf16)
    if prompt:
        a_ref[...] = (q * (HEAD_DIM ** -0.5 * LOG2E)).T.astype(bf16)
        tk = b_ref.shape[2]
        for c in range(b_ref.shape[0]):
            b_ref[c] = v[c * tk:(c + 1) * tk, :].T.astype(bf16)
    else:
        a_ref[...] = (q * (HEAD_DIM ** -0.5)).astype(bf16)
        b_ref[...] = v.astype(bf16)


def _lam(lam_ref):
    l4 = lam_ref[...]
    a = jnp.sum(l4[0:1, :] * l4[1:2, :], axis=1, keepdims=True)
    b = jnp.sum(l4[2:3, :] * l4[3:4, :], axis=1, keepdims=True)
    return jnp.exp(a) - jnp.exp(b)


def _attn_finish(acc, l, lam, lam_init, sub_ref):
    r = acc.shape[0] // 2
    o = acc[:r] / l[:r] - lam * (acc[r:] / l[r:])
    o = o * lax.rsqrt(jnp.mean(o * o, axis=-1, keepdims=True) + NORM_EPS) * sub_ref[...]
    return o * (1.0 - lam_init)


def _flash_body(lam_init, qt_ref, k_ref, vt_ref, lam_ref, sub_ref, o_ref, acc_ref, s_ref, p_ref):
    tq = qt_ref.shape[1]
    tk = vt_ref.shape[2]
    hd2 = 2 * HEAD_DIM
    nh = qt_ref.shape[0] // hd2
    qi = pl.program_id(2)
    top = lax.broadcasted_iota(i32, (hd2, 1), 0) < HEAD_DIM
    qqs = []
    for h in range(nh):
        qt = qt_ref[h * hd2:(h + 1) * hd2, :]
        zero = jnp.zeros_like(qt)
        qqs.append(jnp.concatenate([jnp.where(top, qt, zero), jnp.where(top, zero, qt)], axis=1))
    acc_ref[...] = jnp.zeros_like(acc_ref)
    p_ref[...] = jnp.zeros_like(p_ref)

    def scores(j):
        r0 = pl.multiple_of(j * tk, tk)
        return [_mm(k_ref[pl.ds(r0, tk), h * hd2:(h + 1) * hd2], qqs[h]) for h in range(nh)]

    def keep(ss, slot):
        for h in range(nh):
            s_ref[slot, h] = ss[h]

    def values(j):
        return [_mm(vt_ref[j, h * hd2:(h + 1) * hd2, :], p_ref[h]) for h in range(nh)]

    def softmax(slot, carry, masked):
        out = []
        for h in range(nh):
            m, l, _ = carry[h]
            s = s_ref[slot, h]
            if masked:
                key = lax.broadcasted_iota(i32, s.shape, 0)
                qry = lax.broadcasted_iota(i32, s.shape, 1) % tq
                s = jnp.where(key <= qry, s, NEG_INF)
            m_new = jnp.maximum(m, jnp.max(s, axis=0, keepdims=True))
            alpha = jnp.exp2(m - m_new)
            p = jnp.exp2(s - m_new)
            p_ref[h] = p.astype(bf16)
            out.append((m_new, alpha * l + jnp.sum(p, axis=0, keepdims=True), alpha))
        return tuple(out)

    def accumulate(pvs, carry):
        for h in range(nh):
            acc_ref[h] = carry[h][2] * acc_ref[h] + pvs[h]

    def step(j, slot, carry):
        pvs = values(jnp.maximum(j - 1, 0))
        ss = scores(j + 1)
        new = softmax(slot, carry, False)
        keep(ss, 1 - slot)
        accumulate(pvs, carry)
        return new

    def finish(slot, carry):
        pvs = values(jnp.maximum(qi - 1, 0))
        last = softmax(slot, carry, True)
        accumulate(pvs, carry)
        accumulate(values(qi), last)
        return tuple(c[1] for c in last)

    keep(scores(0), 0)
    init = tuple((jnp.full((1, 2 * tq), NEG_INF, f32), jnp.zeros((1, 2 * tq), f32),
                  jnp.ones((1, 2 * tq), f32)) for _ in range(nh))
    carry = lax.fori_loop(0, qi // 2, lambda i, c: step(2 * i + 1, 1, step(2 * i, 0, c)), init)
    ls = lax.cond(qi % 2 == 0, lambda c: finish(0, c), lambda c: finish(1, step(qi - 1, 0, c)), carry)
    lam = _lam(lam_ref) + lam_init
    for h in range(nh):
        acc = acc_ref[h]
        l = ls[h]
        o = acc[:, :tq] / l[:, :tq] - lam * (acc[:, tq:] / l[:, tq:])
        o = o * lax.rsqrt(jnp.mean(o * o, axis=0, keepdims=True) + NORM_EPS) * sub_ref[...]
        o_ref[:, h * hd2:(h + 1) * hd2] = (o * (1.0 - lam_init)).T.astype(o_ref.dtype)


def _attn_sample_body(lam_init, pp, pt_ref, q_ref, kn_ref, vn_ref, mp_ref, mn_ref, lam_ref, sub_ref, *rest):
    k_refs, v_refs = rest[:pp], rest[pp:2 * pp]
    o_ref = rest[2 * pp]
    m_ref, l_ref, acc_ref = rest[2 * pp + 1:]
    j = pl.program_id(1)
    q = q_ref[...]

    @pl.when(j == 0)
    def _():
        m_ref[...] = jnp.full_like(m_ref, NEG_INF)
        l_ref[...] = jnp.zeros_like(l_ref)
        acc_ref[...] = jnp.zeros_like(acc_ref)

    def lane_fold(op, a):
        parts = [a[:, c:c + LANES] for c in range(0, a.shape[1], LANES)]
        while len(parts) > 1:
            parts = [op(parts[i], parts[i + 1]) for i in range(0, len(parts) - 1, 2)] + parts[len(parts) & ~1:]
        return parts[0]

    def update(kfs, vfs, mask):
        ss = [_nt(q, kf) + mask for kf in kfs]
        mx = ss[0] if ss[0].shape[1] < LANES else functools.reduce(jnp.maximum, [lane_fold(jnp.maximum, s) for s in ss])
        m = m_ref[...]
        m_new = jnp.maximum(m, jnp.max(mx, axis=1, keepdims=True))
        alpha = jnp.exp(m - m_new)
        ps = [jnp.exp(s - m_new) for s in ss]
        sm = ps[0] if ps[0].shape[1] < LANES else functools.reduce(jnp.add, [lane_fold(jnp.add, p) for p in ps])
        l_ref[...] = alpha * l_ref[...] + jnp.sum(sm, axis=1, keepdims=True)
        pv = functools.reduce(jnp.add, [_mm(p.astype(bf16), vf) for p, vf in zip(ps, vfs)])
        acc_ref[...] = alpha * acc_ref[...] + pv
        m_ref[...] = m_new

    update([r[...].astype(bf16) for r in k_refs], [r[...].astype(bf16) for r in v_refs], mp_ref[...])

    @pl.when(j == pl.num_programs(1) - 1)
    def _():
        update([kn_ref[...]], [vn_ref[...]], mn_ref[...])
        lam = _lam(lam_ref) + lam_init
        o_ref[...] = _attn_finish(acc_ref[...], l_ref[...], lam, lam_init, sub_ref)


def _oproj_body(x_ref, o_ref_in, g1_ref, w_ref, o_ref):
    o_ref[...] = x_ref[...] + g1_ref[...] * _mm(o_ref_in[...], w_ref[...])


def _attn_layer(x_all, dims, l, ia, W, cache_k, cache_v, page_table):
    n_all, d = x_all.shape
    n_p, n_s, seq, nb, n_t, batch = (dims[k] for k in ('n_p', 'n_s', 'seq', 'nb', 'n_t', 'batch'))
    past_len = dims['past_len']
    hd2 = 2 * HEAD_DIM
    lam_init = 0.8 - 0.6 * math.exp(-0.3 * l)

    half = HEAD_DIM // 2
    inv = jnp.power(ROPE_THETA, -jnp.arange(half, dtype=f32) * (2.0 / HEAD_DIM))

    def tables(pos):
        ang = pos.astype(f32)[:, None] * inv[None, :]
        cos, sin = jnp.cos(ang), jnp.sin(ang)
        return (jnp.concatenate([cos] * 4, axis=1), jnp.concatenate([-sin, sin, -sin, sin], axis=1))

    cos_p, sin_p = tables(jnp.arange(seq))
    cos_s, sin_s = tables(past_len + jnp.repeat(jnp.arange(n_t), nb))
    seg = (jnp.arange(d)[:, None] // HEAD_DIM == jnp.arange(d // HEAD_DIM)[None, :]).astype(bf16)
    consts = [W['norm_mix'][l].reshape(1, d), W['w_qkv'][ia].astype(bf16),
              jnp.tile(W['q_norm'][ia], d // HEAD_DIM).reshape(1, d),
              jnp.tile(W['k_norm'][ia], d // HEAD_DIM).reshape(1, d), seg, seg.T]
    modp, mods = dims['modp'], dims['mods']
    tk = TQ_ATTN
    kb, k_pr, v_pr, qt, vt = _rows_call(
        functools.partial(_qkv_body, True), tb=TB_PROMPT, nblk=n_p // TB_PROMPT, off=0, bps=seq // TB_PROMPT,
        row_ins=[x_all], mod_ins=[(modp, 0), (modp, 1)], seq_ins=[cos_p, sin_p], consts=consts,
        outs=[((n_all, d), bf16, 'row'), ((n_p, d), f32, 'own'), ((n_p, d), f32, 'own'),
              ((d, n_p), bf16, 'col'), ((n_p // tk, d, tk), bf16, 'blk3')], name="qkv_p")
    kb, k_sm, v_sm, qs, vb = _rows_call(
        functools.partial(_qkv_body, False), tb=n_s, nblk=1, off=n_p // n_s, bps=1,
        row_ins=[x_all], mod_ins=[(mods, 0), (mods, 1)], seq_ins=[cos_s, sin_s], consts=consts,
        outs=[((n_all, d), bf16, 'row'), ((n_s, d), f32, 'own'), ((n_s, d), f32, 'own'),
              ((n_s, d), bf16, 'own'), ((n_s, d), bf16, 'own')], prev=[kb], name="qkv_s")

    lam4 = jnp.stack([W['lambda_q1'][ia], W['lambda_k1'][ia], W['lambda_q2'][ia], W['lambda_k2'][ia]])
    sub = W['subln'][ia].reshape(1, hd2)
    sub_c = W['subln'][ia].reshape(hd2, 1)
    nq = seq // TQ_ATTN
    hw = HEADS_PER_STEP * hd2
    o_all = pl.pallas_call(
        functools.partial(_flash_body, lam_init), grid=(batch, N_HEADS // HEADS_PER_STEP, nq),
        in_specs=[pl.BlockSpec((hw, TQ_ATTN), lambda b, h, i: (h, b * nq + i)),
                  pl.BlockSpec((seq, hw), lambda b, h, i: (b, h)),
                  pl.BlockSpec((seq // tk, hw, tk), lambda b, h, i: (b, h, 0)),
                  pl.BlockSpec(lam4.shape, lambda b, h, i: (0, 0)),
                  pl.BlockSpec(sub_c.shape, lambda b, h, i: (0, 0))],
        out_specs=pl.BlockSpec((TQ_ATTN, hw), lambda b, h, i: (b * nq + i, h)),
        out_shape=jax.ShapeDtypeStruct((n_all, d), bf16),
        scratch_shapes=[pltpu.VMEM((HEADS_PER_STEP, hd2, 2 * TQ_ATTN), f32),
                        pltpu.VMEM((2, HEADS_PER_STEP, tk, 2 * TQ_ATTN), f32),
                        pltpu.VMEM((HEADS_PER_STEP, tk, 2 * TQ_ATTN), bf16)],
        compiler_params=_cparams("arbitrary", "arbitrary", "arbitrary"), name="flash_p")(
            qt, kb, vt, lam4, sub_c)

    na, n_phys, page, _, _ = cache_k.shape
    flat = page * N_HEADS
    ck = cache_k.reshape(na * n_phys, flat, hd2)
    cv = cache_v.reshape(na * n_phys, flat, hd2)
    n_pages = page_table.shape[1]
    pp = PAGES_PER_STEP if n_pages % PAGES_PER_STEP == 0 else 1
    rows = 2 * N_HEADS * n_t
    q5 = jnp.transpose(qs.reshape(n_t, nb, N_HEADS, 2, HEAD_DIM), (1, 3, 2, 0, 4))
    z = jnp.zeros_like(q5[:, 0])
    qm = jnp.stack([jnp.concatenate([q5[:, 0], z], -1), jnp.concatenate([z, q5[:, 1]], -1)], axis=1)
    qm = qm.reshape(nb, rows, hd2)

    def new_rows(a):
        return jnp.transpose(a.reshape(n_t, nb, N_HEADS, hd2), (1, 0, 2, 3)).reshape(nb, n_t * N_HEADS, hd2)

    kn, vn = new_rows(kb[n_p:]), new_rows(vb)
    r_h = (jnp.arange(rows) % (N_HEADS * n_t)) // n_t
    r_t = jnp.arange(rows) % n_t
    mask_p = jnp.where(r_h[:, None] == (jnp.arange(flat) % N_HEADS)[None, :], 0.0, NEG_INF).astype(f32)
    cn = jnp.arange(n_t * N_HEADS)
    mask_n = jnp.where((r_h[:, None] == (cn % N_HEADS)[None, :]) & ((cn // N_HEADS)[None, :] <= r_t[:, None]),
                       0.0, NEG_INF).astype(f32)
    base = ia * n_phys

    def seq_spec(shape):
        return pl.BlockSpec((None,) + shape, lambda s, j, pt: (s, 0, 0))

    def cst_spec(a):
        return pl.BlockSpec(a.shape, lambda s, j, pt: (0, 0))

    def page_spec(u):
        return pl.BlockSpec((None, flat, hd2), lambda s, j, pt, _u=u: (base + pt[s, j * pp + _u], 0, 0))

    gs = pltpu.PrefetchScalarGridSpec(
        num_scalar_prefetch=1, grid=(nb, n_pages // pp),
        in_specs=[seq_spec((rows, hd2)), seq_spec((n_t * N_HEADS, hd2)), seq_spec((n_t * N_HEADS, hd2)),
                  cst_spec(mask_p), cst_spec(mask_n), cst_spec(lam4), cst_spec(sub)]
                 + [page_spec(u) for u in range(pp)] + [page_spec(u) for u in range(pp)],
        out_specs=pl.BlockSpec((None, rows // 2, hd2), lambda s, j, pt: (s, 0, 0)),
        scratch_shapes=[pltpu.VMEM((rows, 1), f32), pltpu.VMEM((rows, 1), f32), pltpu.VMEM((rows, hd2), f32)])
    o_s = pl.pallas_call(
        functools.partial(_attn_sample_body, lam_init, pp), grid_spec=gs,
        out_shape=jax.ShapeDtypeStruct((nb, rows // 2, hd2), f32),
        compiler_params=_cparams("arbitrary", "arbitrary"), name="attn_s")(
            page_table, qm, kn, vn, mask_p, mask_n, lam4, sub, *([ck] * pp), *([cv] * pp))
    o_s = jnp.transpose(o_s.reshape(nb, N_HEADS, n_t, hd2), (2, 0, 1, 3)).reshape(n_s, d)
    o_all = lax.dynamic_update_slice(o_all, o_s.astype(bf16), (n_p, 0))

    (x_new,) = _both_groups(
        _oproj_body, dims, row_ins=[x_all, o_all], mod_ks=(2,), consts=[W['w_o'][ia].astype(bf16)],
        outs=[((n_all, d), f32, 'row')], name="oproj")

    def smp(a):
        return jnp.transpose(a.reshape(n_t, nb, N_HEADS, hd2), (1, 0, 2, 3))

    shp = (batch, seq, N_HEADS, hd2)
    return x_new, k_pr.reshape(shp), v_pr.reshape(shp), smp(k_sm), smp(v_sm)


def kernel(x_prompt, x_sample, cache_k, cache_v, state_pool, state_ssm_re, state_ssm_im, page_table, c_prompt, c_sample, w_ada, b_ada, norm_mix, norm_ff, w_pool, b_pool, ls_pool, ssm_lambda_re, ssm_lambda_im, ssm_log_dt, ssm_b_re, ssm_b_im, ssm_c_re, ssm_c_im, ssm_d, w_glu, b_glu, w_qkv, q_norm, k_norm, lambda_q1, lambda_k1, lambda_q2, lambda_k2, subln, w_o, w_router, b_router, w_gate_up, w_down):
    batch, seq, d = x_prompt.shape
    nb, n_t, _ = x_sample.shape
    depth = w_ada.shape[0]
    n_p, n_s = batch * seq, nb * n_t
    past_len = page_table.shape[1] * cache_k.shape[2]
    W = dict(norm_mix=norm_mix, norm_ff=norm_ff, w_pool=w_pool, b_pool=b_pool, ls_pool=ls_pool,
             ssm_lambda_re=ssm_lambda_re, ssm_lambda_im=ssm_lambda_im, ssm_log_dt=ssm_log_dt,
             ssm_b_re=ssm_b_re, ssm_b_im=ssm_b_im, ssm_c_re=ssm_c_re, ssm_c_im=ssm_c_im, ssm_d=ssm_d,
             w_glu=w_glu, b_glu=b_glu, w_qkv=w_qkv, q_norm=q_norm, k_norm=k_norm, lambda_q1=lambda_q1,
             lambda_k1=lambda_k1, lambda_q2=lambda_q2, lambda_k2=lambda_k2, subln=subln, w_o=w_o,
             w_router_t=w_router.T, b_router_c=b_router.reshape(-1, 1),
             w_gate_up=w_gate_up, w_down=w_down)

    ada = _ada_all(jnp.concatenate([c_prompt, c_sample], axis=0), w_ada, b_ada)
    x_all = x_prompt.reshape(n_p, d)
    x_smp, smp_off = jnp.transpose(x_sample, (1, 0, 2)).reshape(n_s, d), 0
    dims = dict(n_p=n_p, n_s=n_s, seq=seq, nb=nb, n_t=n_t, batch=batch, past_len=past_len)

    pools_p, pools_s, k_ps, v_ps, k_ss, v_ss = [], [], [], [], [], []
    re_ps, im_ps, re_ss, im_ss = [], [], [], []
    ip = iq = ia = 0
    for l in range(depth):
        chunks = jnp.transpose(ada[l].reshape(batch + nb, 6, d), (1, 0, 2))
        dims['modp'] = chunks[:, :batch, None, :]
        dims['mods'] = jnp.tile(chunks[:, batch:], (1, n_t, 1))[:, None]
        kind = l % N_MIXERS
        if kind == 0:
            x_all, pp_, ps_ = _pool_layer(x_all, x_smp, smp_off, dims, l, ip, W, state_pool)
            pools_p.append(pp_)
            pools_s.append(ps_)
            ip += 1
        elif kind == 1:
            x_all, rp, imp, rs, ims = _ssm_layer(x_all, dims, l, iq, W, state_ssm_re[iq], state_ssm_im[iq])
            re_ps.append(rp)
            im_ps.append(imp)
            re_ss.append(rs)
            im_ss.append(ims)
            iq += 1
        else:
            x_all, kp, vp, ks, vs = _attn_layer(x_all, dims, l, ia, W, cache_k, cache_v, page_table)
            k_ps.append(kp)
            v_ps.append(vp)
            k_ss.append(ks)
            v_ss.append(vs)
            ia += 1
        x_all = _moe_layer(x_all, dims, l, W, l == depth - 1)
        x_smp, smp_off = x_all, n_p // n_s

    y_prompt = x_all[0].reshape(batch, seq, d)
    y_sample = jnp.transpose(x_all[1].reshape(n_t, nb, d), (1, 0, 2))
    return (y_prompt, y_sample, jnp.stack(k_ps), jnp.stack(v_ps), jnp.stack(k_ss), jnp.stack(v_ss),
            jnp.stack(pools_p), jnp.stack(pools_s), jnp.stack(re_ps), jnp.stack(im_ps),
            jnp.stack(re_ss), jnp.stack(im_ss))
```

```python
import functools
import math

import jax
import jax.numpy as jnp
from jax import lax
from jax.experimental import pallas as pl
from jax.experimental.pallas import tpu as pltpu

f32 = jnp.float32
bf16 = jnp.bfloat16
i32 = jnp.int32

N_MIXERS = 3
POOL_WINDOWS = (2, 4, 8, 16)
POOL_HIST = 15
HALO = 16
SSM_GROUP_CH = 16
SSM_STATE = 64
SLAB_GROUPS = 8
N_HEADS = 8
HEAD_DIM = 64
ROPE_THETA = 10000.0
N_EXPERTS = 16
EXPERTS_PER_GROUP = 4
NORM_EPS = 1e-6
NEG_INF = -1e30
LOG2E = 1.4426950408889634
LANES = 128
VMEM_LIMIT = 48 * 1024 * 1024

TB_PROMPT = 512
TB_POOL = 256
T_SSM = 256
TQ_ATTN = 256
HEADS_PER_STEP = 2
TM_MOE = 512
H2_ROWS_FACTOR = 3
XS_PARTS = 2
INVPERM_UNROLL = 8
PAGES_PER_STEP = 16


def _cparams(*sem):
    return pltpu.CompilerParams(dimension_semantics=sem, vmem_limit_bytes=VMEM_LIMIT)


def _mm(a, b):
    return jnp.dot(a, b, preferred_element_type=f32)


def _nt(a, b):
    return lax.dot_general(a, b, (((1,), (1,)), ((), ())), preferred_element_type=f32)


def _split2(a):
    hi = a.astype(bf16)
    lo = (a - hi.astype(f32)).astype(bf16)
    return hi, lo


def _norm_mod(x, g, sc, sh):
    ms = jnp.mean(x * x, axis=-1, keepdims=True)
    return x * lax.rsqrt(ms + NORM_EPS) * g * (1.0 + sc) + sh


def _gelu_tanh(y):
    return 0.5 * y * (1.0 + jnp.tanh(0.7978845608028654 * (y + 0.044715 * y * y * y)))


def _sigmoid(x):
    return 1.0 / (1.0 + jnp.exp(-x))


def _full_spec(a):
    nd = a.ndim
    return pl.BlockSpec(a.shape, lambda i, _n=nd: (0,) * _n)


def _rows_call(body, *, tb, nblk, off, bps, row_ins, mod_ins, seq_ins, consts, outs, prev=None,
               scratch=(), name):
    in_specs, args = [], []
    for a in row_ins:
        a, o = a if isinstance(a, tuple) else (a, off)
        in_specs.append(pl.BlockSpec((tb, a.shape[1]), lambda i, _o=o: (i + _o, 0)))
        args.append(a)
    for a, k in mod_ins:
        r, d = a.shape[2], a.shape[3]
        in_specs.append(pl.BlockSpec((None, None, r, d), lambda i, _k=k: (_k, i // bps, 0, 0)))
        args.append(a)
    for a in seq_ins:
        in_specs.append(pl.BlockSpec((tb, a.shape[1]), lambda i: (i % bps, 0)))
        args.append(a)
    for a in consts:
        in_specs.append(_full_spec(a))
        args.append(a)
    n_real = len(args)
    aliases = {}
    if prev is not None:
        for j, a in enumerate(prev):
            in_specs.append(pl.BlockSpec(memory_space=pl.ANY))
            args.append(a)
            aliases[n_real + j] = j
    out_specs, out_shapes = [], []
    for shape, dtype, kind in outs:
        if kind == 'row':
            out_specs.append(pl.BlockSpec((tb, shape[1]), lambda i: (i + off, 0)))
        elif kind == 'own':
            out_specs.append(pl.BlockSpec((tb, shape[1]), lambda i: (i, 0)))
        elif kind == 'col':
            out_specs.append(pl.BlockSpec((shape[0], tb), lambda i: (0, i + off)))
        elif kind == 'blk3':
            out_specs.append(pl.BlockSpec((tb // shape[2], shape[1], shape[2]), lambda i: (i, 0, 0)))
        else:
            out_specs.append(pl.BlockSpec(shape, lambda i, _n=len(shape): (0,) * _n))
        out_shapes.append(jax.ShapeDtypeStruct(shape, dtype))
    n_prev = 0 if prev is None else len(prev)

    def wrapped(*refs):
        body(*refs[:n_real], *refs[n_real + n_prev:])

    res = pl.pallas_call(
        wrapped, grid=(nblk,), in_specs=in_specs, out_specs=out_specs, out_shape=out_shapes,
        scratch_shapes=list(scratch), input_output_aliases=aliases,
        compiler_params=_cparams("arbitrary"), name=name)(*args)
    return list(res)


def _both_groups(body, dims, *, row_ins, mod_ks, consts, outs, name, seq_ins_p=(), seq_ins_s=()):
    modp, mods = dims['modp'], dims['mods']
    n_p, n_s, seq = dims['n_p'], dims['n_s'], dims['seq']
    res = _rows_call(body, tb=TB_PROMPT, nblk=n_p // TB_PROMPT, off=0, bps=seq // TB_PROMPT,
                     row_ins=row_ins, mod_ins=[(modp, k) for k in mod_ks], seq_ins=list(seq_ins_p),
                     consts=consts, outs=outs, name=name + "_p")
    res = _rows_call(body, tb=n_s, nblk=1, off=n_p // n_s, bps=1,
                     row_ins=row_ins, mod_ins=[(mods, k) for k in mod_ks], seq_ins=list(seq_ins_s),
                     consts=consts, outs=outs, prev=res, name=name + "_s")
    return res


def _ada_body(c_ref, w_ref, b_ref, o_ref):
    c = c_ref[...]
    cond = c * _sigmoid(c)
    c_hi, c_lo = _split2(cond)
    w_hi, w_lo = _split2(w_ref[...])
    o_ref[...] = _mm(c_hi, w_hi) + _mm(c_hi, w_lo) + _mm(c_lo, w_hi) + b_ref[...]


def _ada_all(c_all, w_ada, b_ada):
    depth, d, d6 = w_ada.shape
    n = c_all.shape[0]
    nj = d6 // d
    return pl.pallas_call(
        _ada_body, grid=(depth, nj),
        in_specs=[pl.BlockSpec((n, d), lambda l, j: (0, 0)),
                  pl.BlockSpec((None, d, d), lambda l, j: (l, 0, j)),
                  pl.BlockSpec((None, 1, d), lambda l, j: (l, 0, j))],
        out_specs=pl.BlockSpec((None, n, d), lambda l, j: (l, 0, j)),
        out_shape=jax.ShapeDtypeStruct((depth, n, d6), f32),
        compiler_params=_cparams("arbitrary", "arbitrary"), name="ada")(
            c_all, w_ada, b_ada.reshape(depth, 1, d6))


def _router_body(x_ref, sh_ref, sc_ref, nw_ref, wrt_ref, br_ref, tri_ref, cnt0_ref,
                 h_ref, ids_ref, gates_ref, rank_ref, cnt_ref, carry_ref):
    @pl.when(pl.program_id(0) == 0)
    def _():
        carry_ref[...] = cnt0_ref[...]

    h = _norm_mod(x_ref[...], nw_ref[...], sc_ref[...], sh_ref[...])
    h_ref[...] = h.astype(bf16)
    h_hi, h_lo = _split2(h)
    w_hi, w_lo = _split2(wrt_ref[...])
    lg = _nt(w_hi, h_hi) + _nt(w_hi, h_lo) + _nt(w_lo, h_hi) + br_ref[...]
    e = jnp.exp(lg - jnp.max(lg, axis=0, keepdims=True))
    best = None
    for g in range(N_EXPERTS // EXPERTS_PER_GROUP):
        v = [e[EXPERTS_PER_GROUP * g + j:EXPERTS_PER_GROUP * g + j + 1, :] for j in range(EXPERTS_PER_GROUP)]
        t1 = jnp.maximum(jnp.maximum(v[0], v[1]), jnp.maximum(v[2], v[3]))
        i1 = jnp.where(v[0] == t1, 0, jnp.where(v[1] == t1, 1, jnp.where(v[2] == t1, 2, 3)))
        w = [jnp.where(i1 == j, -1.0, v[j]) for j in range(EXPERTS_PER_GROUP)]
        t2 = jnp.maximum(jnp.maximum(w[0], w[1]), jnp.maximum(w[2], w[3]))
        i2 = jnp.where(w[0] == t2, 0, jnp.where(w[1] == t2, 1, jnp.where(w[2] == t2, 2, 3)))
        cand = (t1 + t2, t1, t2, i1 + EXPERTS_PER_GROUP * g, i2 + EXPERTS_PER_GROUP * g)
        if best is None:
            best = cand
        else:
            upd = cand[0] > best[0]
            best = tuple(jnp.where(upd, c, b) for c, b in zip(cand, best))
    s, t1, t2, i1, i2 = best
    ids_ref[0:1, :] = i1.astype(i32)
    ids_ref[1:2, :] = i2.astype(i32)
    gates_ref[0:1, :] = t1 / s
    gates_ref[1:2, :] = t2 / s
    eidx = lax.broadcasted_iota(i32, lg.shape, 0)
    oh1 = (eidx == i1).astype(f32)
    oh2 = (eidx == i2).astype(f32)
    tri = tri_ref[...]
    ex1 = _mm(oh1.astype(bf16), tri)
    ex2 = _mm(oh2.astype(bf16), tri)
    carry = carry_ref[...]
    tot1 = jnp.sum(oh1, axis=1, keepdims=True)
    tot2 = jnp.sum(oh2, axis=1, keepdims=True)
    rank_ref[0:1, :] = jnp.sum(oh1 * (carry + ex1), axis=0, keepdims=True).astype(i32)
    rank_ref[1:2, :] = jnp.sum(oh2 * (carry + tot1 + ex2), axis=0, keepdims=True).astype(i32)
    carry_ref[...] = carry + tot1 + tot2
    cnt_ref[...] = carry_ref[...]


def _invperm_body(dest_ref, gap_ref, src_ref):
    n_tok = dest_ref.shape[0] // 2
    spread = (1 << (n_tok.bit_length() - 1)) - 1

    def fill(i, c):
        src_ref[i] = i & spread
        return c

    for g in range(gap_ref.shape[0] // 2):
        lax.fori_loop(gap_ref[2 * g], gap_ref[2 * g + 1], fill, 0)

    def place(t, c):
        src_ref[dest_ref[t]] = t
        src_ref[dest_ref[n_tok + t]] = t
        return c

    lax.fori_loop(0, n_tok, place, 0, unroll=INVPERM_UNROLL)


def _moe_body(bounds, te_ref, nu_ref, *refs):
    n_parts = len(bounds) - 1
    x_refs = refs[:n_parts]
    wgu_ref, wdn_ref, o_ref, wgu_b, wdn_b = refs[n_parts:]
    i = pl.program_id(0)
    f = wdn_ref.shape[0]

    @pl.when(jnp.logical_or(i == 0, te_ref[i] != te_ref[jnp.maximum(i - 1, 0)]))
    def _():
        wgu_b[...] = wgu_ref[...].astype(bf16)
        wdn_b[...] = wdn_ref[...].astype(bf16)

    for k in range(n_parts):
        @pl.when(jnp.logical_and(i < nu_ref[0], jnp.logical_and(i >= bounds[k], i < bounds[k + 1])))
        def _(k=k):
            gu = _mm(x_refs[k][...], wgu_b[...])
            g = gu[:, :f]
            act = g * _sigmoid(g) * gu[:, f:]
            o_ref[...] = _mm(act.astype(bf16), wdn_b[...]).astype(bf16)

    @pl.when(i >= nu_ref[0])
    def _():
        o_ref[...] = jnp.zeros_like(o_ref)


def _moe_grouped(xs, wgu, wdn, l, tile_expert, n_used):
    d = xs[0].shape[1]
    f2 = wgu.shape[3]
    f = wdn.shape[2]
    bounds = [0]
    for x in xs:
        bounds.append(bounds[-1] + x.shape[0] // TM_MOE)

    def part_spec(k):
        return pl.BlockSpec((TM_MOE, d), lambda i, te, nu: (jnp.clip(i - bounds[k], 0, bounds[k + 1] - bounds[k] - 1), 0))

    gs = pltpu.PrefetchScalarGridSpec(
        num_scalar_prefetch=2, grid=(bounds[-1],),
        in_specs=[part_spec(k) for k in range(len(xs))]
                 + [pl.BlockSpec((None, None, d, f2), lambda i, te, nu: (l, te[i], 0, 0)),
                    pl.BlockSpec((None, None, f, d), lambda i, te, nu: (l, te[i], 0, 0))],
        out_specs=pl.BlockSpec((TM_MOE, d), lambda i, te, nu: (i, 0)),
        scratch_shapes=[pltpu.VMEM((d, f2), bf16), pltpu.VMEM((f, d), bf16)])
    return pl.pallas_call(functools.partial(_moe_body, tuple(bounds)), grid_spec=gs,
                          out_shape=jax.ShapeDtypeStruct((bounds[-1] * TM_MOE, d), bf16),
                          compiler_params=_cparams("arbitrary"), name="moe_experts")(
                              tile_expert, n_used, *xs, wgu, wdn)


def _combine_body(x_ref, oa_ref, ob_ref, gt_ref, g2_ref, o_ref):
    gt = gt_ref[...]
    y = gt[:, 0:1] * oa_ref[...].astype(f32) + gt[:, 1:2] * ob_ref[...].astype(f32)
    o_ref[...] = x_ref[...] + g2_ref[...] * y


def _tri(n):
    return (jnp.arange(n)[:, None] < jnp.arange(n)[None, :]).astype(bf16)


def _moe_layer(x_all, dims, l, W, last):
    n_all, d = x_all.shape
    n_p, n_s, seq = dims['n_p'], dims['n_s'], dims['seq']
    modp, mods = dims['modp'], dims['mods']
    consts = [W['norm_ff'][l].reshape(1, d), W['w_router_t'], W['b_router_c']]
    outs = [((H2_ROWS_FACTOR * n_all, d), bf16, 'row'), ((2, n_all), i32, 'col'), ((2, n_all), f32, 'col'),
            ((2, n_all), i32, 'col'), ((N_EXPERTS, 1), f32, 'const')]
    scratch = [pltpu.VMEM((N_EXPERTS, 1), f32)]
    res = _rows_call(_router_body, tb=TB_PROMPT, nblk=n_p // TB_PROMPT, off=0, bps=seq // TB_PROMPT,
                     row_ins=[x_all], mod_ins=[(modp, 3), (modp, 4)], seq_ins=[],
                     consts=consts + [_tri(TB_PROMPT), jnp.zeros((N_EXPERTS, 1), f32)],
                     outs=outs, scratch=scratch, name="router_p")
    h2, ids, gates, rank, cnt = _rows_call(
        _router_body, tb=n_s, nblk=1, off=n_p // n_s, bps=1,
        row_ins=[x_all], mod_ins=[(mods, 3), (mods, 4)], seq_ins=[],
        consts=consts + [_tri(n_s), res[4]], outs=outs, prev=res[:4], scratch=scratch, name="router_s")

    n_tiles = (2 * n_all + N_EXPERTS * (TM_MOE - 1)) // TM_MOE + 1
    counts = cnt[:, 0].astype(i32)
    padded = (counts + TM_MOE - 1) // TM_MOE * TM_MOE
    pend = jnp.cumsum(padded)
    n_used = (pend[-1:] // TM_MOE).astype(i32)
    tile_start = jnp.arange(n_tiles, dtype=i32) * TM_MOE
    tile_expert = jnp.minimum(jnp.sum((pend[None, :] <= tile_start[:, None]).astype(i32), axis=1),
                              N_EXPERTS - 1)
    pstart = (pend - padded).astype(i32)
    sel = ids[:, :, None] == jnp.arange(N_EXPERTS, dtype=i32)[None, None, :]
    dest = rank + jnp.sum(jnp.where(sel, pstart[None, None, :], 0), axis=2)
    gaps = jnp.stack([jnp.append(pstart + counts, pend[-1]),
                      jnp.append(pend, n_tiles * TM_MOE)], axis=1).reshape(-1).astype(i32)
    smem = pl.BlockSpec(memory_space=pltpu.SMEM)
    row_src = pl.pallas_call(
        _invperm_body, in_specs=[smem, smem], out_specs=smem,
        out_shape=jax.ShapeDtypeStruct((n_tiles * TM_MOE,), i32), name="moe_invperm")(dest.reshape(-1), gaps)

    def rows(a, idx):
        return a.at[idx].get(mode='promise_in_bounds')

    cuts = [k * n_tiles // XS_PARTS * TM_MOE for k in range(XS_PARTS + 1)]
    xs = [rows(h2, row_src[cuts[k]:cuts[k + 1]]) for k in range(XS_PARTS)]
    osort = _moe_grouped(xs, W['w_gate_up'], W['w_down'], l, tile_expert, n_used)
    oa = rows(osort, dest[0])
    ob = rows(osort, dest[1])
    gt = gates.T
    if not last:
        (x_new,) = _both_groups(
            _combine_body, dims, row_ins=[x_all, oa, ob, gt], mod_ks=(5,), consts=[],
            outs=[((n_all, d), f32, 'row')], name="combine")
        return x_new
    (y_p,) = _rows_call(_combine_body, tb=TB_PROMPT, nblk=n_p // TB_PROMPT, off=0, bps=seq // TB_PROMPT,
                        row_ins=[x_all, oa, ob, gt], mod_ins=[(modp, 5)], seq_ins=[], consts=[],
                        outs=[((n_p, d), f32, 'own')], name="combine_last_p")
    (y_s,) = _rows_call(_combine_body, tb=n_s, nblk=1, off=n_p // n_s, bps=1,
                        row_ins=[x_all, oa, ob, gt], mod_ins=[(mods, 5)], seq_ins=[], consts=[],
                        outs=[((n_s, d), f32, 'own')], name="combine_last_s")
    return y_p, y_s


def _pool_tail(pooled_slabs, w_ref, b_ref, ls_ref):
    ys = [_mm(p.astype(bf16), w_ref[g]) for g, p in enumerate(pooled_slabs)]
    return (jnp.concatenate(ys, axis=1) + b_ref[...]) * ls_ref[...]


def _pool_prompt_body(bps, x_ref, xh_ref, sh_ref, sc_ref, g1_ref, nw_ref, w_ref, b_ref, ls_ref,
                      o_ref, st_ref, ext_ref):
    tb = x_ref.shape[0]
    gw = w_ref.shape[1]
    blk = pl.program_id(0) % bps
    x = x_ref[...]
    nw, sc, sh = nw_ref[...], sc_ref[...], sh_ref[...]
    h = _norm_mod(x, nw, sc, sh)
    hh = _norm_mod(xh_ref[...], nw, sc, sh)
    ext_ref[0:HALO, :] = jnp.where(blk == 0, 0.0, hh)
    ext_ref[HALO:, :] = h
    pos = blk * tb + lax.broadcasted_iota(i32, (tb, 1), 0)
    slabs = []
    for g, wdw in enumerate(POOL_WINDOWS):
        lo, hi = g * gw, (g + 1) * gw
        acc = h[:, lo:hi]
        for j in range(1, wdw):
            acc = acc + ext_ref[HALO - j:HALO - j + tb, lo:hi]
        cnt = jnp.minimum(pos + 1, wdw).astype(f32)
        slabs.append(acc / cnt - h[:, lo:hi])
    y = _pool_tail(slabs, w_ref, b_ref, ls_ref)
    o_ref[...] = x + g1_ref[...] * y
    st_ref[...] = h[tb - HALO:, :]


def _pool_sample_body(n_t, pos0, x_ref, past_ref, sh_ref, sc_ref, g1_ref, nw_ref, w_ref, b_ref, ls_ref,
                      o_ref, h_ref):
    nb = past_ref.shape[1]
    gw = w_ref.shape[1]
    x = x_ref[...]
    h = _norm_mod(x, nw_ref[...], sc_ref[...], sh_ref[...])
    h_ref[...] = h

    def ext(r, lo, hi):
        if r < POOL_HIST:
            return past_ref[r][:, lo:hi]
        t = r - POOL_HIST
        return h[t * nb:(t + 1) * nb, lo:hi]

    slabs = []
    for g, wdw in enumerate(POOL_WINDOWS):
        lo, hi = g * gw, (g + 1) * gw
        rows = []
        for t in range(n_t):
            acc = ext(POOL_HIST + t, lo, hi)
            for j in range(1, wdw):
                acc = acc + ext(POOL_HIST + t - j, lo, hi)
            cnt = float(min(pos0 + t + 1, wdw))
            rows.append(acc / cnt - ext(POOL_HIST + t, lo, hi))
        slabs.append(jnp.concatenate(rows, axis=0))
    y = _pool_tail(slabs, w_ref, b_ref, ls_ref)
    o_ref[...] = x + g1_ref[...] * y


def _pool_layer(x_all, x_smp, smp_off, dims, l, ip, W, state_pool):
    d = x_all.shape[1]
    n_p, n_s, seq, nb, n_t, batch = (dims[k] for k in ('n_p', 'n_s', 'seq', 'nb', 'n_t', 'batch'))
    n_all = n_p + n_s
    modp, mods = dims['modp'], dims['mods']
    bps = seq // TB_POOL
    consts = [W['norm_mix'][l].reshape(1, d), W['w_pool'][ip].astype(bf16),
              W['b_pool'][ip].reshape(1, d), W['ls_pool'][ip].reshape(1, d)]

    def mspec(k):
        return pl.BlockSpec((None, None, 1, d), lambda i, _k=k: (_k, i // bps, 0, 0))

    ratio = TB_POOL // HALO
    x_new, st = pl.pallas_call(
        functools.partial(_pool_prompt_body, bps), grid=(n_p // TB_POOL,),
        in_specs=[pl.BlockSpec((TB_POOL, d), lambda i: (i, 0)),
                  pl.BlockSpec((HALO, d), lambda i: (jnp.maximum(i * ratio - 1, 0), 0)),
                  mspec(0), mspec(1), mspec(2)] + [_full_spec(c) for c in consts],
        out_specs=[pl.BlockSpec((TB_POOL, d), lambda i: (i, 0)),
                   pl.BlockSpec((None, HALO, d), lambda i: (i // bps, 0, 0))],
        out_shape=[jax.ShapeDtypeStruct((n_all, d), f32), jax.ShapeDtypeStruct((batch, HALO, d), f32)],
        scratch_shapes=[pltpu.VMEM((TB_POOL + HALO, d), f32)],
        compiler_params=_cparams("arbitrary"), name="pool_p")(
            x_all, x_all, modp, modp, modp, *consts)
    pool_prompt = st[:, HALO - POOL_HIST:, :]

    past = jnp.transpose(state_pool[ip], (1, 0, 2))
    off = n_p // n_s

    def sspec(k):
        return pl.BlockSpec((None, None, n_s, d), lambda i, _k=k: (_k, 0, 0, 0))

    def body(x_ref, past_ref, sh, sc, g1, nw, w, b, ls, prev_ref, o_ref, h_ref):
        _pool_sample_body(n_t, dims['past_len'], x_ref, past_ref, sh, sc, g1, nw, w, b, ls, o_ref, h_ref)

    x_new, h_s = pl.pallas_call(
        body, grid=(1,),
        in_specs=[pl.BlockSpec((n_s, d), lambda i: (smp_off, 0)), _full_spec(past),
                  sspec(0), sspec(1), sspec(2)] + [_full_spec(c) for c in consts]
                 + [pl.BlockSpec(memory_space=pl.ANY)],
        out_specs=[pl.BlockSpec((n_s, d), lambda i: (off, 0)), pl.BlockSpec((n_s, d), lambda i: (0, 0))],
        out_shape=[jax.ShapeDtypeStruct((n_all, d), f32), jax.ShapeDtypeStruct((n_s, d), f32)],
        input_output_aliases={5 + len(consts): 0},
        compiler_params=_cparams("arbitrary"), name="pool_s")(
            x_smp, past, mods, mods, mods, *consts, x_new)
    ext = jnp.concatenate([past, h_s.reshape(n_t, nb, d)], axis=0)
    pool_sample = jnp.transpose(ext[-POOL_HIST:], (1, 0, 2))
    return x_new, pool_prompt, pool_sample


def _ssm_param_body(lr_ref, li_ref, ldt_ref, apr_ref, api_ref, kr_ref, ki_ref):
    lr, li = lr_ref[...], li_ref[...]
    dt = jnp.exp(ldt_ref[...])
    mag = jnp.exp(lr * dt)
    a_r, a_i = mag * jnp.cos(li * dt), mag * jnp.sin(li * dt)
    den = lr * lr + li * li
    nr, ni = a_r - 1.0, a_i
    kr_ref[...] = (nr * lr + ni * li) / den
    ki_ref[...] = (ni * lr - nr * li) / den
    pr, pi_ = a_r, a_i
    for k in range(8):
        apr_ref[k] = pr
        api_ref[k] = pi_
        pr, pi_ = pr * a_r - pi_ * a_i, pr * a_i + pi_ * a_r


def _ssm_tables(W, iq):
    g, p = W['ssm_lambda_re'][iq].shape
    c = SSM_GROUP_CH
    ns = g // SLAB_GROUPS
    sl = SLAB_GROUPS
    shp = [jax.ShapeDtypeStruct((8, g, p), f32)] * 2 + [jax.ShapeDtypeStruct((g, p), f32)] * 2
    apr, api, k_r, k_i = pl.pallas_call(_ssm_param_body, out_shape=shp, name="ssm_params")(
        W['ssm_lambda_re'][iq], W['ssm_lambda_im'][iq], W['ssm_log_dt'][iq].reshape(g, 1))
    b_re, b_im = W['ssm_b_re'][iq], W['ssm_b_im'][iq]
    bb_re = k_r[..., None] * b_re - k_i[..., None] * b_im
    bb_im = k_r[..., None] * b_im + k_i[..., None] * b_re
    eye = jnp.eye(sl, dtype=f32)

    def b_slab(bb):
        b4 = bb.reshape(ns, sl, p, c)
        return jnp.einsum('sgpc,gh->sgchp', b4, eye).reshape(ns, sl * c, sl * p)

    def c_slab(cc):
        c4 = cc.reshape(ns, sl, c, p)
        return jnp.einsum('sgcp,gh->shpgc', c4, eye).reshape(ns, sl * p, sl * c)

    bmat = jnp.concatenate([b_slab(bb_re), b_slab(bb_im)], axis=2).astype(bf16)
    cmat = jnp.concatenate([c_slab(W['ssm_c_re'][iq]), -c_slab(W['ssm_c_im'][iq])], axis=1).astype(bf16)

    def lay(a):
        return jnp.transpose(a.reshape(8, ns, sl * p), (1, 0, 2))

    apw = jnp.stack([lay(apr), lay(api)], axis=1)
    return bmat, cmat, apw


def _ssm_out(x, h, y, g1_ref, dsk_ref, wglu_ref, bglu_ref):
    d = x.shape[1]
    z = _gelu_tanh(y + dsk_ref[...] * h)
    gu = _mm(z.astype(bf16), wglu_ref[...]) + bglu_ref[...]
    return x + g1_ref[...] * (gu[:, :d] * _sigmoid(gu[:, d:]))


def _ssm_prompt_body(x_ref, sh_ref, sc_ref, g1_ref, nw_ref, b_ref, c_ref, t2_ref, dsk_ref,
                     wglu_ref, bglu_ref, o_ref, st_ref, xs_ref, hs_ref, carry_ref, y_ref):
    t = x_ref.shape[0]
    ns = b_ref.shape[0]
    sw = b_ref.shape[1]
    half = b_ref.shape[2] // 2
    hc = half // LANES

    @pl.when(pl.program_id(1) == 0)
    def _():
        carry_ref[...] = jnp.zeros_like(carry_ref)

    x = x_ref[...]
    h = _norm_mod(x, nw_ref[...], sc_ref[...], sh_ref[...])
    hb = h.astype(bf16)
    def expand(s):
        xx = _mm(hb[:, s * sw:(s + 1) * sw], b_ref[s])
        for c in range(2 * hc):
            xs_ref[s % 3, c] = xx[:, c * LANES:(c + 1) * LANES]

    def group_scan(s):
        b = s % 3
        for c in range(hc):
            ar = jnp.broadcast_to(t2_ref[s, 0, 0:1, c * LANES:(c + 1) * LANES], (8, LANES))
            ai = jnp.broadcast_to(t2_ref[s, 1, 0:1, c * LANES:(c + 1) * LANES], (8, LANES))
            for base in range(0, t, 64):
                hr = xs_ref[b, c, pl.ds(base, 8, stride=8), :]
                hi = xs_ref[b, hc + c, pl.ds(base, 8, stride=8), :]
                for i in range(1, 8):
                    rows = pl.ds(base + i, 8, stride=8)
                    hr, hi = (ar * hr - ai * hi + xs_ref[b, c, rows, :],
                              ar * hi + ai * hr + xs_ref[b, hc + c, rows, :])
                    xs_ref[b, c, rows, :] = hr
                    xs_ref[b, hc + c, rows, :] = hi

    def carry_scan(s):
        b = s % 3

        def grp(j, car):
            cr, ci = car
            r0 = pl.multiple_of(j * 8, 8)
            xr = jnp.concatenate([xs_ref[b, c, pl.ds(r0, 8), :] for c in range(hc)], axis=1)
            xi = jnp.concatenate([xs_ref[b, hc + c, pl.ds(r0, 8), :] for c in range(hc)], axis=1)
            pr = t2_ref[s, 0]
            pi_ = t2_ref[s, 1]
            hr = xr + pr * cr - pi_ * ci
            hi = xi + pr * ci + pi_ * cr
            hs_ref[s % 2, pl.ds(r0, 8), 0:half] = hr
            hs_ref[s % 2, pl.ds(r0, 8), half:] = hi
            return (jnp.broadcast_to(hr[7:8, :], hr.shape), jnp.broadcast_to(hi[7:8, :], hi.shape))

        cr, ci = lax.fori_loop(0, t // 8, grp, (carry_ref[s, :, 0:half], carry_ref[s, :, half:]))
        carry_ref[s, :, 0:half] = cr
        carry_ref[s, :, half:] = ci
        st_ref[s:s + 1, :] = jnp.concatenate([cr[0:1, :], ci[0:1, :]], axis=1)

    expand(0)
    if ns > 1:
        expand(1)
    group_scan(0)
    for s in range(ns):
        carry_scan(s)
        y_ref[:, s * sw:(s + 1) * sw] = _mm(hs_ref[s % 2].astype(bf16), c_ref[s])
        if s + 2 < ns:
            expand(s + 2)
        if s + 1 < ns:
            group_scan(s + 1)
    y = y_ref[...]
    o_ref[...] = _ssm_out(x, h, y, g1_ref, dsk_ref, wglu_ref, bglu_ref)


def _ssm_sample_body(n_t, x_ref, s0_ref, sh_ref, sc_ref, g1_ref, nw_ref, b_ref, c_ref, t2_ref, dsk_ref,
                     wglu_ref, bglu_ref, prev_ref, o_ref, so_ref, y_ref):
    ns = b_ref.shape[0]
    sw = b_ref.shape[1]
    half = b_ref.shape[2] // 2
    nb = s0_ref.shape[1]
    x = x_ref[...]
    h = _norm_mod(x, nw_ref[...], sc_ref[...], sh_ref[...])
    hb = h.astype(bf16)
    for s in range(ns):
        s_r = s0_ref[s, :, 0:half]
        s_i = s0_ref[s, :, half:]
        ar = t2_ref[s, 0, 0:1, :]
        ai = t2_ref[s, 1, 0:1, :]
        for t in range(n_t):
            xx = _mm(hb[t * nb:(t + 1) * nb, s * sw:(s + 1) * sw], b_ref[s])
            s_r, s_i = ar * s_r - ai * s_i + xx[:, 0:half], ar * s_i + ai * s_r + xx[:, half:]
            st = jnp.concatenate([s_r, s_i], axis=1)
            y_ref[t * nb:(t + 1) * nb, s * sw:(s + 1) * sw] = _mm(st.astype(bf16), c_ref[s])
        so_ref[s] = jnp.concatenate([s_r, s_i], axis=1)
    o_ref[...] = _ssm_out(x, h, y_ref[...], g1_ref, dsk_ref, wglu_ref, bglu_ref)


def _ssm_layer(x_all, dims, l, iq, W, st_re, st_im):
    n_all, d = x_all.shape
    n_p, n_s, seq, nb, n_t, batch = (dims[k] for k in ('n_p', 'n_s', 'seq', 'nb', 'n_t', 'batch'))
    modp, mods = dims['modp'], dims['mods']
    bmat, cmat, t2 = _ssm_tables(W, iq)
    ns = bmat.shape[0]
    sp = bmat.shape[2]
    g, p = W['ssm_lambda_re'][iq].shape
    sl = SLAB_GROUPS
    tail = [W['ssm_d'][iq].reshape(1, d), W['w_glu'][iq].astype(bf16), W['b_glu'][iq].reshape(1, 2 * d)]
    nw = W['norm_mix'][l].reshape(1, d)
    nchunk = seq // T_SSM

    def mspec(k):
        return pl.BlockSpec((None, None, 1, d), lambda b, c, _k=k: (_k, b, 0, 0))

    def cspec(a):
        return pl.BlockSpec(a.shape, lambda b, c, _n=a.ndim: (0,) * _n)

    consts = [nw, bmat, cmat, t2] + tail
    x_new, st = pl.pallas_call(
        _ssm_prompt_body, grid=(batch, nchunk),
        in_specs=[pl.BlockSpec((T_SSM, d), lambda b, c: (b * nchunk + c, 0)), mspec(0), mspec(1), mspec(2)]
                 + [cspec(a) for a in consts],
        out_specs=[pl.BlockSpec((T_SSM, d), lambda b, c: (b * nchunk + c, 0)),
                   pl.BlockSpec((None, ns, sp), lambda b, c: (b, 0, 0))],
        out_shape=[jax.ShapeDtypeStruct((n_all, d), f32), jax.ShapeDtypeStruct((batch, ns, sp), f32)],
        scratch_shapes=[pltpu.VMEM((3, sp // LANES, T_SSM, LANES), f32), pltpu.VMEM((2, T_SSM, sp), f32),
                        pltpu.VMEM((ns, 8, sp), f32), pltpu.VMEM((T_SSM, d), f32)],
        compiler_params=_cparams("arbitrary", "arbitrary"), name="ssm_p")(
            x_all, modp, modp, modp, *consts)

    def unslab(a):
        a5 = a.reshape(a.shape[0], ns, 2, sl, p)
        return a5[:, :, 0].reshape(-1, g, p), a5[:, :, 1].reshape(-1, g, p)

    re_p, im_p = unslab(st)

    def slab(a):
        return a.reshape(a.shape[0], ns, sl * p)

    s0 = jnp.transpose(jnp.concatenate([slab(st_re), slab(st_im)], axis=2), (1, 0, 2))
    off = n_p // n_s

    def sspec(k):
        return pl.BlockSpec((None, None, n_s, d), lambda i, _k=k: (_k, 0, 0, 0))

    consts_s = [nw, bmat, cmat, t2] + tail
    x_new, so = pl.pallas_call(
        functools.partial(_ssm_sample_body, n_t), grid=(1,),
        in_specs=[pl.BlockSpec((n_s, d), lambda i: (off, 0)), _full_spec(s0), sspec(0), sspec(1), sspec(2)]
                 + [_full_spec(a) for a in consts_s] + [pl.BlockSpec(memory_space=pl.ANY)],
        out_specs=[pl.BlockSpec((n_s, d), lambda i: (off, 0)), _full_spec(s0)],
        out_shape=[jax.ShapeDtypeStruct((n_all, d), f32), jax.ShapeDtypeStruct(s0.shape, f32)],
        scratch_shapes=[pltpu.VMEM((n_s, d), f32)],
        input_output_aliases={5 + len(consts_s): 0},
        compiler_params=_cparams("arbitrary"), name="ssm_s")(
            x_all, s0, mods, mods, mods, *consts_s, x_new)
    re_s, im_s = unslab(jnp.transpose(so, (1, 0, 2)))
    return x_new, re_p, im_p, re_s, im_s


def _qkv_body(prompt, x_ref, sh_ref, sc_ref, cos_ref, sin_ref, nw_ref, w_ref, qn_ref, kn_ref, seg_ref, segt_ref,
              kb_ref, k_ref, v_ref, a_ref, b_ref):
    d = x_ref.shape[1]
    h = _norm_mod(x_ref[...], nw_ref[...], sc_ref[...], sh_ref[...])
    qkv = _mm(h.astype(bf16), w_ref[...])
    rep = d // LANES
    cos = jnp.concatenate([cos_ref[...]] * rep, axis=1)
    sin = jnp.concatenate([sin_ref[...]] * rep, axis=1)
    lane_lo = (lax.broadcasted_iota(i32, (1, d), 1) % HEAD_DIM) < (HEAD_DIM // 2)
    seg, segt = seg_ref[...], segt_ref[...]

    def norm_rope(t, g):
        s_hi, s_lo = _split2(t * t)
        ms = (_mm(s_hi, seg) + _mm(s_lo, seg)) * (1.0 / HEAD_DIM)
        r_hi, r_lo = _split2(lax.rsqrt(ms + NORM_EPS))
        rf = _mm(r_hi, segt) + _mm(r_lo, segt)
        tn = t * rf * g
        sw = jnp.where(lane_lo, pltpu.roll(tn, d - HEAD_DIM // 2, 1), pltpu.roll(tn, HEAD_DIM // 2, 1))
        return tn * cos + sw * sin

    q = norm_rope(qkv[:, :d], qn_ref[...])
    k = norm_rope(qkv[:, d:2 * d], kn_ref[...])
    v = qkv[:, 2 * d:]
    k_ref[...] = k
    v_ref[...] = v
    kb_ref[...] = k.astype(bf16)
    if prompt:
        a_ref[...] = (q * (HEAD_DIM ** -0.5 * LOG2E)).T.astype(bf16)
        tk = b_ref.shape[2]
        for c in range(b_ref.shape[0]):
            b_ref[c] = v[c * tk:(c + 1) * tk, :].T.astype(bf16)
    else:
        a_ref[...] = (q * (HEAD_DIM ** -0.5)).astype(bf16)
        b_ref[...] = v.astype(bf16)


def _lam(lam_ref):
    l4 = lam_ref[...]
    a = jnp.sum(l4[0:1, :] * l4[1:2, :], axis=1, keepdims=True)
    b = jnp.sum(l4[2:3, :] * l4[3:4, :], axis=1, keepdims=True)
    return jnp.exp(a) - jnp.exp(b)


def _attn_finish(acc, l, lam, lam_init, sub_ref):
    r = acc.shape[0] // 2
    o = acc[:r] / l[:r] - lam * (acc[r:] / l[r:])
    o = o * lax.rsqrt(jnp.mean(o * o, axis=-1, keepdims=True) + NORM_EPS) * sub_ref[...]
    return o * (1.0 - lam_init)


def _flash_body(lam_init, qt_ref, k_ref, vt_ref, lam_ref, sub_ref, o_ref, acc_ref, s_ref, p_ref):
    tq = qt_ref.shape[1]
    tk = vt_ref.shape[2]
    hd2 = 2 * HEAD_DIM
    nh = qt_ref.shape[0] // hd2
    qi = pl.program_id(2)
    top = lax.broadcasted_iota(i32, (hd2, 1), 0) < HEAD_DIM
    qqs = []
    for h in range(nh):
        qt = qt_ref[h * hd2:(h + 1) * hd2, :]
        zero = jnp.zeros_like(qt)
        qqs.append(jnp.concatenate([jnp.where(top, qt, zero), jnp.where(top, zero, qt)], axis=1))
    acc_ref[...] = jnp.zeros_like(acc_ref)
    p_ref[...] = jnp.zeros_like(p_ref)

    def scores(j):
        r0 = pl.multiple_of(j * tk, tk)
        return [_mm(k_ref[pl.ds(r0, tk), h * hd2:(h + 1) * hd2], qqs[h]) for h in range(nh)]

    def keep(ss, slot):
        for h in range(nh):
            s_ref[slot, h] = ss[h]

    def values(j):
        return [_mm(vt_ref[j, h * hd2:(h + 1) * hd2, :], p_ref[h]) for h in range(nh)]

    def softmax(slot, carry, masked):
        out = []
        for h in range(nh):
            m, l, _ = carry[h]
            s = s_ref[slot, h]
            if masked:
                key = lax.broadcasted_iota(i32, s.shape, 0)
                qry = lax.broadcasted_iota(i32, s.shape, 1) % tq
                s = jnp.where(key <= qry, s, NEG_INF)
            m_new = jnp.maximum(m, jnp.max(s, axis=0, keepdims=True))
            alpha = jnp.exp2(m - m_new)
            p = jnp.exp2(s - m_new)
            p_ref[h] = p.astype(bf16)
            out.append((m_new, alpha * l + jnp.sum(p, axis=0, keepdims=True), alpha))
        return tuple(out)

    def accumulate(pvs, carry):
        for h in range(nh):
            acc_ref[h] = carry[h][2] * acc_ref[h] + pvs[h]

    def step(j, slot, carry):
        pvs = values(jnp.maximum(j - 1, 0))
        ss = scores(j + 1)
        new = softmax(slot, carry, False)
        keep(ss, 1 - slot)
        accumulate(pvs, carry)
        return new

    def finish(slot, carry):
        pvs = values(jnp.maximum(qi - 1, 0))
        last = softmax(slot, carry, True)
        accumulate(pvs, carry)
        accumulate(values(qi), last)
        return tuple(c[1] for c in last)

    keep(scores(0), 0)
    init = tuple((jnp.full((1, 2 * tq), NEG_INF, f32), jnp.zeros((1, 2 * tq), f32),
                  jnp.ones((1, 2 * tq), f32)) for _ in range(nh))
    carry = lax.fori_loop(0, qi // 2, lambda i, c: step(2 * i + 1, 1, step(2 * i, 0, c)), init)
    ls = lax.cond(qi % 2 == 0, lambda c: finish(0, c), lambda c: finish(1, step(qi - 1, 0, c)), carry)
    lam = _lam(lam_ref) + lam_init
    for h in range(nh):
        acc = acc_ref[h]
        l = ls[h]
        o = acc[:, :tq] / l[:, :tq] - lam * (acc[:, tq:] / l[:, tq:])
        o = o * lax.rsqrt(jnp.mean(o * o, axis=0, keepdims=True) + NORM_EPS) * sub_ref[...]
        o_ref[:, h * hd2:(h + 1) * hd2] = (o * (1.0 - lam_init)).T.astype(o_ref.dtype)


def _attn_sample_body(lam_init, pp, pt_ref, q_ref, kn_ref, vn_ref, mp_ref, mn_ref, lam_ref, sub_ref, *rest):
    k_refs, v_refs = rest[:pp], rest[pp:2 * pp]
    o_ref = rest[2 * pp]
    m_ref, l_ref, acc_ref = rest[2 * pp + 1:]
    j = pl.program_id(1)
    q = q_ref[...]

    @pl.when(j == 0)
    def _():
        m_ref[...] = jnp.full_like(m_ref, NEG_INF)
        l_ref[...] = jnp.zeros_like(l_ref)
        acc_ref[...] = jnp.zeros_like(acc_ref)

    def lane_fold(op, a):
        parts = [a[:, c:c + LANES] for c in range(0, a.shape[1], LANES)]
        while len(parts) > 1:
            parts = [op(parts[i], parts[i + 1]) for i in range(0, len(parts) - 1, 2)] + parts[len(parts) & ~1:]
        return parts[0]

    def update(kfs, vfs, mask):
        ss = [_nt(q, kf) + mask for kf in kfs]
        mx = ss[0] if ss[0].shape[1] < LANES else functools.reduce(jnp.maximum, [lane_fold(jnp.maximum, s) for s in ss])
        m = m_ref[...]
        m_new = jnp.maximum(m, jnp.max(mx, axis=1, keepdims=True))
        alpha = jnp.exp(m - m_new)
        ps = [jnp.exp(s - m_new) for s in ss]
        sm = ps[0] if ps[0].shape[1] < LANES else functools.reduce(jnp.add, [lane_fold(jnp.add, p) for p in ps])
        l_ref[...] = alpha * l_ref[...] + jnp.sum(sm, axis=1, keepdims=True)
        pv = functools.reduce(jnp.add, [_mm(p.astype(bf16), vf) for p, vf in zip(ps, vfs)])
        acc_ref[...] = alpha * acc_ref[...] + pv
        m_ref[...] = m_new

    update([r[...].astype(bf16) for r in k_refs], [r[...].astype(bf16) for r in v_refs], mp_ref[...])

    @pl.when(j == pl.num_programs(1) - 1)
    def _():
        update([kn_ref[...]], [vn_ref[...]], mn_ref[...])
        lam = _lam(lam_ref) + lam_init
        o_ref[...] = _attn_finish(acc_ref[...], l_ref[...], lam, lam_init, sub_ref)


def _oproj_body(x_ref, o_ref_in, g1_ref, w_ref, o_ref):
    o_ref[...] = x_ref[...] + g1_ref[...] * _mm(o_ref_in[...], w_ref[...])


def _attn_layer(x_all, dims, l, ia, W, cache_k, cache_v, page_table):
    n_all, d = x_all.shape
    n_p, n_s, seq, nb, n_t, batch = (dims[k] for k in ('n_p', 'n_s', 'seq', 'nb', 'n_t', 'batch'))
    past_len = dims['past_len']
    hd2 = 2 * HEAD_DIM
    lam_init = 0.8 - 0.6 * math.exp(-0.3 * l)

    half = HEAD_DIM // 2
    inv = jnp.power(ROPE_THETA, -jnp.arange(half, dtype=f32) * (2.0 / HEAD_DIM))

    def tables(pos):
        ang = pos.astype(f32)[:, None] * inv[None, :]
        cos, sin = jnp.cos(ang), jnp.sin(ang)
        return (jnp.concatenate([cos] * 4, axis=1), jnp.concatenate([-sin, sin, -sin, sin], axis=1))

    cos_p, sin_p = tables(jnp.arange(seq))
    cos_s, sin_s = tables(past_len + jnp.repeat(jnp.arange(n_t), nb))
    seg = (jnp.arange(d)[:, None] // HEAD_DIM == jnp.arange(d // HEAD_DIM)[None, :]).astype(bf16)
    consts = [W['norm_mix'][l].reshape(1, d), W['w_qkv'][ia].astype(bf16),
              jnp.tile(W['q_norm'][ia], d // HEAD_DIM).reshape(1, d),
              jnp.tile(W['k_norm'][ia], d // HEAD_DIM).reshape(1, d), seg, seg.T]
    modp, mods = dims['modp'], dims['mods']
    tk = TQ_ATTN
    kb, k_pr, v_pr, qt, vt = _rows_call(
        functools.partial(_qkv_body, True), tb=TB_PROMPT, nblk=n_p // TB_PROMPT, off=0, bps=seq // TB_PROMPT,
        row_ins=[x_all], mod_ins=[(modp, 0), (modp, 1)], seq_ins=[cos_p, sin_p], consts=consts,
        outs=[((n_all, d), bf16, 'row'), ((n_p, d), f32, 'own'), ((n_p, d), f32, 'own'),
              ((d, n_p), bf16, 'col'), ((n_p // tk, d, tk), bf16, 'blk3')], name="qkv_p")
    kb, k_sm, v_sm, qs, vb = _rows_call(
        functools.partial(_qkv_body, False), tb=n_s, nblk=1, off=n_p // n_s, bps=1,
        row_ins=[x_all], mod_ins=[(mods, 0), (mods, 1)], seq_ins=[cos_s, sin_s], consts=consts,
        outs=[((n_all, d), bf16, 'row'), ((n_s, d), f32, 'own'), ((n_s, d), f32, 'own'),
              ((n_s, d), bf16, 'own'), ((n_s, d), bf16, 'own')], prev=[kb], name="qkv_s")

    lam4 = jnp.stack([W['lambda_q1'][ia], W['lambda_k1'][ia], W['lambda_q2'][ia], W['lambda_k2'][ia]])
    sub = W['subln'][ia].reshape(1, hd2)
    sub_c = W['subln'][ia].reshape(hd2, 1)
    nq = seq // TQ_ATTN
    hw = HEADS_PER_STEP * hd2
    o_all = pl.pallas_call(
        functools.partial(_flash_body, lam_init), grid=(batch, N_HEADS // HEADS_PER_STEP, nq),
        in_specs=[pl.BlockSpec((hw, TQ_ATTN), lambda b, h, i: (h, b * nq + i)),
                  pl.BlockSpec((seq, hw), lambda b, h, i: (b, h)),
                  pl.BlockSpec((seq // tk, hw, tk), lambda b, h, i: (b, h, 0)),
                  pl.BlockSpec(lam4.shape, lambda b, h, i: (0, 0)),
                  pl.BlockSpec(sub_c.shape, lambda b, h, i: (0, 0))],
        out_specs=pl.BlockSpec((TQ_ATTN, hw), lambda b, h, i: (b * nq + i, h)),
        out_shape=jax.ShapeDtypeStruct((n_all, d), bf16),
        scratch_shapes=[pltpu.VMEM((HEADS_PER_STEP, hd2, 2 * TQ_ATTN), f32),
                        pltpu.VMEM((2, HEADS_PER_STEP, tk, 2 * TQ_ATTN), f32),
                        pltpu.VMEM((HEADS_PER_STEP, tk, 2 * TQ_ATTN), bf16)],
        compiler_params=_cparams("arbitrary", "arbitrary", "arbitrary"), name="flash_p")(
            qt, kb, vt, lam4, sub_c)

    na, n_phys, page, _, _ = cache_k.shape
    flat = page * N_HEADS
    ck = cache_k.reshape(na * n_phys, flat, hd2)
    cv = cache_v.reshape(na * n_phys, flat, hd2)
    n_pages = page_table.shape[1]
    pp = PAGES_PER_STEP if n_pages % PAGES_PER_STEP == 0 else 1
    rows = 2 * N_HEADS * n_t
    q5 = jnp.transpose(qs.reshape(n_t, nb, N_HEADS, 2, HEAD_DIM), (1, 3, 2, 0, 4))
    z = jnp.zeros_like(q5[:, 0])
    qm = jnp.stack([jnp.concatenate([q5[:, 0], z], -1), jnp.concatenate([z, q5[:, 1]], -1)], axis=1)
    qm = qm.reshape(nb, rows, hd2)

    def new_rows(a):
        return jnp.transpose(a.reshape(n_t, nb, N_HEADS, hd2), (1, 0, 2, 3)).reshape(nb, n_t * N_HEADS, hd2)

    kn, vn = new_rows(kb[n_p:]), new_rows(vb)
    r_h = (jnp.arange(rows) % (N_HEADS * n_t)) // n_t
    r_t = jnp.arange(rows) % n_t
    mask_p = jnp.where(r_h[:, None] == (jnp.arange(flat) % N_HEADS)[None, :], 0.0, NEG_INF).astype(f32)
    cn = jnp.arange(n_t * N_HEADS)
    mask_n = jnp.where((r_h[:, None] == (cn % N_HEADS)[None, :]) & ((cn // N_HEADS)[None, :] <= r_t[:, None]),
                       0.0, NEG_INF).astype(f32)
    base = ia * n_phys

    def seq_spec(shape):
        return pl.BlockSpec((None,) + shape, lambda s, j, pt: (s, 0, 0))

    def cst_spec(a):
        return pl.BlockSpec(a.shape, lambda s, j, pt: (0, 0))

    def page_spec(u):
        return pl.BlockSpec((None, flat, hd2), lambda s, j, pt, _u=u: (base + pt[s, j * pp + _u], 0, 0))

    gs = pltpu.PrefetchScalarGridSpec(
        num_scalar_prefetch=1, grid=(nb, n_pages // pp),
        in_specs=[seq_spec((rows, hd2)), seq_spec((n_t * N_HEADS, hd2)), seq_spec((n_t * N_HEADS, hd2)),
                  cst_spec(mask_p), cst_spec(mask_n), cst_spec(lam4), cst_spec(sub)]
                 + [page_spec(u) for u in range(pp)] + [page_spec(u) for u in range(pp)],
        out_specs=pl.BlockSpec((None, rows // 2, hd2), lambda s, j, pt: (s, 0, 0)),
        scratch_shapes=[pltpu.VMEM((rows, 1), f32), pltpu.VMEM((rows, 1), f32), pltpu.VMEM((rows, hd2), f32)])
    o_s = pl.pallas_call(
        functools.partial(_attn_sample_body, lam_init, pp), grid_spec=gs,
        out_shape=jax.ShapeDtypeStruct((nb, rows // 2, hd2), f32),
        compiler_params=_cparams("arbitrary", "arbitrary"), name="attn_s")(
            page_table, qm, kn, vn, mask_p, mask_n, lam4, sub, *([ck] * pp), *([cv] * pp))
    o_s = jnp.transpose(o_s.reshape(nb, N_HEADS, n_t, hd2), (2, 0, 1, 3)).reshape(n_s, d)
    o_all = lax.dynamic_update_slice(o_all, o_s.astype(bf16), (n_p, 0))

    (x_new,) = _both_groups(
        _oproj_body, dims, row_ins=[x_all, o_all], mod_ks=(2,), consts=[W['w_o'][ia].astype(bf16)],
        outs=[((n_all, d), f32, 'row')], name="oproj")

    def smp(a):
        return jnp.transpose(a.reshape(n_t, nb, N_HEADS, hd2), (1, 0, 2, 3))

    shp = (batch, seq, N_HEADS, hd2)
    return x_new, k_pr.reshape(shp), v_pr.reshape(shp), smp(k_sm), smp(v_sm)


def kernel(x_prompt, x_sample, cache_k, cache_v, state_pool, state_ssm_re, state_ssm_im, page_table, c_prompt, c_sample, w_ada, b_ada, norm_mix, norm_ff, w_pool, b_pool, ls_pool, ssm_lambda_re, ssm_lambda_im, ssm_log_dt, ssm_b_re, ssm_b_im, ssm_c_re, ssm_c_im, ssm_d, w_glu, b_glu, w_qkv, q_norm, k_norm, lambda_q1, lambda_k1, lambda_q2, lambda_k2, subln, w_o, w_router, b_router, w_gate_up, w_down):
    batch, seq, d = x_prompt.shape
    nb, n_t, _ = x_sample.shape
    depth = w_ada.shape[0]
    n_p, n_s = batch * seq, nb * n_t
    past_len = page_table.shape[1] * cache_k.shape[2]
    W = dict(norm_mix=norm_mix, norm_ff=norm_ff, w_pool=w_pool, b_pool=b_pool, ls_pool=ls_pool,
             ssm_lambda_re=ssm_lambda_re, ssm_lambda_im=ssm_lambda_im, ssm_log_dt=ssm_log_dt,
             ssm_b_re=ssm_b_re, ssm_b_im=ssm_b_im, ssm_c_re=ssm_c_re, ssm_c_im=ssm_c_im, ssm_d=ssm_d,
             w_glu=w_glu, b_glu=b_glu, w_qkv=w_qkv, q_norm=q_norm, k_norm=k_norm, lambda_q1=lambda_q1,
             lambda_k1=lambda_k1, lambda_q2=lambda_q2, lambda_k2=lambda_k2, subln=subln, w_o=w_o,
             w_router_t=w_router.T, b_router_c=b_router.reshape(-1, 1),
             w_gate_up=w_gate_up, w_down=w_down)

    ada = _ada_all(jnp.concatenate([c_prompt, c_sample], axis=0), w_ada, b_ada)
    x_all = x_prompt.reshape(n_p, d)
    x_smp, smp_off = jnp.transpose(x_sample, (1, 0, 2)).reshape(n_s, d), 0
    dims = dict(n_p=n_p, n_s=n_s, seq=seq, nb=nb, n_t=n_t, batch=batch, past_len=past_len)

    pools_p, pools_s, k_ps, v_ps, k_ss, v_ss = [], [], [], [], [], []
    re_ps, im_ps, re_ss, im_ss = [], [], [], []
    ip = iq = ia = 0
    for l in range(depth):
        chunks = jnp.transpose(ada[l].reshape(batch + nb, 6, d), (1, 0, 2))
        dims['modp'] = chunks[:, :batch, None, :]
        dims['mods'] = jnp.tile(chunks[:, batch:], (1, n_t, 1))[:, None]
        kind = l % N_MIXERS
        if kind == 0:
            x_all, pp_, ps_ = _pool_layer(x_all, x_smp, smp_off, dims, l, ip, W, state_pool)
            pools_p.append(pp_)
            pools_s.append(ps_)
            ip += 1
        elif kind == 1:
            x_all, rp, imp, rs, ims = _ssm_layer(x_all, dims, l, iq, W, state_ssm_re[iq], state_ssm_im[iq])
            re_ps.append(rp)
            im_ps.append(imp)
            re_ss.append(rs)
            im_ss.append(ims)
            iq += 1
        else:
            x_all, kp, vp, ks, vs = _attn_layer(x_all, dims, l, ia, W, cache_k, cache_v, page_table)
            k_ps.append(kp)
            v_ps.append(vp)
            k_ss.append(ks)
            v_ss.append(vs)
            ia += 1
        x_all = _moe_layer(x_all, dims, l, W, l == depth - 1)
        x_smp, smp_off = x_all, n_p // n_s

    y_prompt = x_all[0].reshape(batch, seq, d)
    y_sample = jnp.transpose(x_all[1].reshape(n_t, nb, d), (1, 0, 2))
    return (y_prompt, y_sample, jnp.stack(k_ps), jnp.stack(v_ps), jnp.stack(k_ss), jnp.stack(v_ss),
            jnp.stack(pools_p), jnp.stack(pools_s), jnp.stack(re_ps), jnp.stack(im_ps),
            jnp.stack(re_ss), jnp.stack(im_ss))
```

```python
import functools
import math

import jax
import jax.numpy as jnp
from jax import lax
from jax.experimental import pallas as pl
from jax.experimental.pallas import tpu as pltpu

f32 = jnp.float32
bf16 = jnp.bfloat16
i32 = jnp.int32

N_MIXERS = 3
POOL_WINDOWS = (2, 4, 8, 16)
POOL_HIST = 15
HALO = 16
SSM_GROUP_CH = 16
SSM_STATE = 64
SLAB_GROUPS = 8
N_HEADS = 8
HEAD_DIM = 64
ROPE_THETA = 10000.0
N_EXPERTS = 16
EXPERTS_PER_GROUP = 4
NORM_EPS = 1e-6
NEG_INF = -1e30
LOG2E = 1.4426950408889634
LANES = 128
VMEM_LIMIT = 48 * 1024 * 1024

TB_PROMPT = 512
TB_POOL = 256
T_SSM = 256
TQ_ATTN = 256
TK_ATTN = 256
HEADS_PER_STEP = 2
TM_MOE = 512
H2_ROWS_FACTOR = 3
XS_PARTS = 3
INVPERM_UNROLL = 8
PAGES_PER_STEP = 16


def _cparams(*sem):
    return pltpu.CompilerParams(dimension_semantics=sem, vmem_limit_bytes=VMEM_LIMIT)


def _mm(a, b):
    return jnp.dot(a, b, preferred_element_type=f32)


def _nt(a, b):
    return lax.dot_general(a, b, (((1,), (1,)), ((), ())), preferred_element_type=f32)


def _split2(a):
    hi = a.astype(bf16)
    lo = (a - hi.astype(f32)).astype(bf16)
    return hi, lo


def _norm_mod(x, g, sc, sh):
    ms = jnp.mean(x * x, axis=-1, keepdims=True)
    return x * lax.rsqrt(ms + NORM_EPS) * g * (1.0 + sc) + sh


def _gelu_tanh(y):
    return 0.5 * y * (1.0 + jnp.tanh(0.7978845608028654 * (y + 0.044715 * y * y * y)))


def _sigmoid(x):
    return 1.0 / (1.0 + jnp.exp(-x))


def _full_spec(a):
    nd = a.ndim
    return pl.BlockSpec(a.shape, lambda i, _n=nd: (0,) * _n)


def _rows_call(body, *, tb, nblk, off, bps, row_ins, mod_ins, seq_ins, consts, outs, prev=None,
               scratch=(), name):
    in_specs, args = [], []
    for a in row_ins:
        a, o = a if isinstance(a, tuple) else (a, off)
        in_specs.append(pl.BlockSpec((tb, a.shape[1]), lambda i, _o=o: (i + _o, 0)))
        args.append(a)
    for a, k in mod_ins:
        r, d = a.shape[2], a.shape[3]
        in_specs.append(pl.BlockSpec((None, None, r, d), lambda i, _k=k: (_k, i // bps, 0, 0)))
        args.append(a)
    for a in seq_ins:
        in_specs.append(pl.BlockSpec((tb, a.shape[1]), lambda i: (i % bps, 0)))
        args.append(a)
    for a in consts:
        in_specs.append(_full_spec(a))
        args.append(a)
    n_real = len(args)
    aliases = {}
    if prev is not None:
        for j, a in enumerate(prev):
            in_specs.append(pl.BlockSpec(memory_space=pl.ANY))
            args.append(a)
            aliases[n_real + j] = j
    out_specs, out_shapes = [], []
    for shape, dtype, kind in outs:
        if kind == 'row':
            out_specs.append(pl.BlockSpec((tb, shape[1]), lambda i: (i + off, 0)))
        elif kind == 'own':
            out_specs.append(pl.BlockSpec((shape[0] // nblk, shape[1]), lambda i: (i, 0)))
        elif kind == 'col':
            out_specs.append(pl.BlockSpec((shape[0], tb), lambda i: (0, i + off)))
        elif kind == 'blk3':
            out_specs.append(pl.BlockSpec((tb // shape[2], shape[1], shape[2]), lambda i: (i, 0, 0)))
        else:
            out_specs.append(pl.BlockSpec(shape, lambda i, _n=len(shape): (0,) * _n))
        out_shapes.append(jax.ShapeDtypeStruct(shape, dtype))
    n_prev = 0 if prev is None else len(prev)

    def wrapped(*refs):
        body(*refs[:n_real], *refs[n_real + n_prev:])

    res = pl.pallas_call(
        wrapped, grid=(nblk,), in_specs=in_specs, out_specs=out_specs, out_shape=out_shapes,
        scratch_shapes=list(scratch), input_output_aliases=aliases,
        compiler_params=_cparams("arbitrary"), name=name)(*args)
    return list(res)


def _both_groups(body, dims, *, row_ins, mod_ks, consts, outs, name, seq_ins_p=(), seq_ins_s=()):
    modp, mods = dims['modp'], dims['mods']
    n_p, n_s, seq = dims['n_p'], dims['n_s'], dims['seq']
    res = _rows_call(body, tb=TB_PROMPT, nblk=n_p // TB_PROMPT, off=0, bps=seq // TB_PROMPT,
                     row_ins=row_ins, mod_ins=[(modp, k) for k in mod_ks], seq_ins=list(seq_ins_p),
                     consts=consts, outs=outs, name=name + "_p")
    res = _rows_call(body, tb=n_s, nblk=1, off=n_p // n_s, bps=1,
                     row_ins=row_ins, mod_ins=[(mods, k) for k in mod_ks], seq_ins=list(seq_ins_s),
                     consts=consts, outs=outs, prev=res, name=name + "_s")
    return res


def _ada_body(c_ref, w_ref, b_ref, o_ref):
    c = c_ref[...]
    cond = c * _sigmoid(c)
    c_hi, c_lo = _split2(cond)
    w_hi, w_lo = _split2(w_ref[...])
    o_ref[...] = _mm(c_hi, w_hi) + _mm(c_hi, w_lo) + _mm(c_lo, w_hi) + b_ref[...]


def _ada_all(c_all, w_ada, b_ada):
    depth, d, d6 = w_ada.shape
    n = c_all.shape[0]
    nj = d6 // d
    return pl.pallas_call(
        _ada_body, grid=(depth, nj),
        in_specs=[pl.BlockSpec((n, d), lambda l, j: (0, 0)),
                  pl.BlockSpec((None, d, d), lambda l, j: (l, 0, j)),
                  pl.BlockSpec((None, 1, d), lambda l, j: (l, 0, j))],
        out_specs=pl.BlockSpec((None, n, d), lambda l, j: (l, 0, j)),
        out_shape=jax.ShapeDtypeStruct((depth, n, d6), f32),
        compiler_params=_cparams("arbitrary", "arbitrary"), name="ada")(
            c_all, w_ada, b_ada.reshape(depth, 1, d6))


def _router_body(x_ref, sh_ref, sc_ref, nw_ref, wrt_ref, br_ref, tri_ref, cnt0_ref,
                 h_ref, ids_ref, gates_ref, rank_ref, cnt_ref, carry_ref):
    @pl.when(pl.program_id(0) == 0)
    def _():
        carry_ref[...] = cnt0_ref[...]

    h = _norm_mod(x_ref[...], nw_ref[...], sc_ref[...], sh_ref[...])
    h_ref[...] = h.astype(bf16)
    h_hi, h_lo = _split2(h)
    w_hi, w_lo = _split2(wrt_ref[...])
    lg = _nt(w_hi, h_hi) + _nt(w_hi, h_lo) + _nt(w_lo, h_hi) + br_ref[...]
    e = jnp.exp(lg - jnp.max(lg, axis=0, keepdims=True))
    best = None
    for g in range(N_EXPERTS // EXPERTS_PER_GROUP):
        v = [e[EXPERTS_PER_GROUP * g + j:EXPERTS_PER_GROUP * g + j + 1, :] for j in range(EXPERTS_PER_GROUP)]
        t1 = jnp.maximum(jnp.maximum(v[0], v[1]), jnp.maximum(v[2], v[3]))
        i1 = jnp.where(v[0] == t1, 0, jnp.where(v[1] == t1, 1, jnp.where(v[2] == t1, 2, 3)))
        w = [jnp.where(i1 == j, -1.0, v[j]) for j in range(EXPERTS_PER_GROUP)]
        t2 = jnp.maximum(jnp.maximum(w[0], w[1]), jnp.maximum(w[2], w[3]))
        i2 = jnp.where(w[0] == t2, 0, jnp.where(w[1] == t2, 1, jnp.where(w[2] == t2, 2, 3)))
        cand = (t1 + t2, t1, t2, i1 + EXPERTS_PER_GROUP * g, i2 + EXPERTS_PER_GROUP * g)
        if best is None:
            best = cand
        else:
            upd = cand[0] > best[0]
            best = tuple(jnp.where(upd, c, b) for c, b in zip(cand, best))
    s, t1, t2, i1, i2 = best
    ids_ref[0:1, :] = i1.astype(i32)
    ids_ref[1:2, :] = i2.astype(i32)
    gates_ref[0:1, :] = t1 / s
    gates_ref[1:2, :] = t2 / s
    eidx = lax.broadcasted_iota(i32, lg.shape, 0)
    oh1 = (eidx == i1).astype(f32)
    oh2 = (eidx == i2).astype(f32)
    tri = tri_ref[...]
    ex1 = _mm(oh1.astype(bf16), tri)
    ex2 = _mm(oh2.astype(bf16), tri)
    carry = carry_ref[...]
    tot1 = jnp.sum(oh1, axis=1, keepdims=True)
    tot2 = jnp.sum(oh2, axis=1, keepdims=True)
    rank_ref[0:1, :] = jnp.sum(oh1 * (carry + ex1), axis=0, keepdims=True).astype(i32)
    rank_ref[1:2, :] = jnp.sum(oh2 * (carry + tot1 + ex2), axis=0, keepdims=True).astype(i32)
    carry_ref[...] = carry + tot1 + tot2
    cnt_ref[...] = carry_ref[...]


def _invperm_body(dest_ref, gap_ref, src_ref):
    n_tok = dest_ref.shape[0] // 2
    spread = (1 << (n_tok.bit_length() - 1)) - 1

    def fill(i, c):
        src_ref[i] = i & spread
        return c

    for g in range(gap_ref.shape[0] // 2):
        lax.fori_loop(gap_ref[2 * g], gap_ref[2 * g + 1], fill, 0)

    def place(t, c):
        src_ref[dest_ref[t]] = t
        src_ref[dest_ref[n_tok + t]] = t
        return c

    lax.fori_loop(0, n_tok, place, 0, unroll=INVPERM_UNROLL)


def _moe_body(t0, te_ref, nu_ref, x_ref, wgu_ref, wdn_ref, *refs):
    o_ref, wgu_b, wdn_b = refs[-3:]
    i = pl.program_id(0) + t0
    f = wdn_ref.shape[0]

    @pl.when(jnp.logical_or(i == t0, te_ref[i] != te_ref[jnp.maximum(i - 1, 0)]))
    def _():
        wgu_b[...] = wgu_ref[...].astype(bf16)
        wdn_b[...] = wdn_ref[...].astype(bf16)

    @pl.when(i < nu_ref[0])
    def _():
        gu = _mm(x_ref[...], wgu_b[...])
        g = gu[:, :f]
        act = g * _sigmoid(g) * gu[:, f:]
        o_ref[...] = _mm(act.astype(bf16), wdn_b[...]).astype(bf16)

    @pl.when(i >= nu_ref[0])
    def _():
        o_ref[...] = jnp.zeros_like(o_ref)


def _moe_grouped(xs, wgu, wdn, l, tile_expert, n_used):
    d = xs[0].shape[1]
    f2 = wgu.shape[3]
    f = wdn.shape[2]
    n_tiles = sum(x.shape[0] for x in xs) // TM_MOE
    out, t0 = None, 0
    for x in xs:
        nt = x.shape[0] // TM_MOE
        in_specs = [pl.BlockSpec((TM_MOE, d), lambda i, te, nu: (i, 0)),
                    pl.BlockSpec((None, None, d, f2), lambda i, te, nu, _t=t0: (l, te[i + _t], 0, 0)),
                    pl.BlockSpec((None, None, f, d), lambda i, te, nu, _t=t0: (l, te[i + _t], 0, 0))]
        args = [tile_expert, n_used, x, wgu, wdn]
        aliases = {}
        if out is not None:
            in_specs.append(pl.BlockSpec(memory_space=pl.ANY))
            args.append(out)
            aliases = {5: 0}
        gs = pltpu.PrefetchScalarGridSpec(
            num_scalar_prefetch=2, grid=(nt,), in_specs=in_specs,
            out_specs=pl.BlockSpec((TM_MOE, d), lambda i, te, nu, _t=t0: (i + _t, 0)),
            scratch_shapes=[pltpu.VMEM((d, f2), bf16), pltpu.VMEM((f, d), bf16)])
        out = pl.pallas_call(functools.partial(_moe_body, t0), grid_spec=gs,
                             out_shape=jax.ShapeDtypeStruct((n_tiles * TM_MOE, d), bf16),
                             input_output_aliases=aliases,
                             compiler_params=_cparams("arbitrary"), name="moe_experts")(*args)
        t0 += nt
    return out


def _combine_body(x_ref, oa_ref, ob_ref, gt_ref, g2_ref, o_ref):
    gt = gt_ref[...]
    y = gt[:, 0:1] * oa_ref[...].astype(f32) + gt[:, 1:2] * ob_ref[...].astype(f32)
    o_ref[...] = x_ref[...] + g2_ref[...] * y


def _tri(n):
    return (jnp.arange(n)[:, None] < jnp.arange(n)[None, :]).astype(bf16)


def _moe_layer(x_all, dims, l, W, last):
    n_all, d = x_all.shape
    n_p, n_s, seq = dims['n_p'], dims['n_s'], dims['seq']
    modp, mods = dims['modp'], dims['mods']
    consts = [W['norm_ff'][l].reshape(1, d), W['w_router_t'], W['b_router_c']]
    outs = [((H2_ROWS_FACTOR * n_all, d), bf16, 'row'), ((2, n_all), i32, 'col'), ((2, n_all), f32, 'col'),
            ((2, n_all), i32, 'col'), ((N_EXPERTS, 1), f32, 'const')]
    scratch = [pltpu.VMEM((N_EXPERTS, 1), f32)]
    res = _rows_call(_router_body, tb=TB_PROMPT, nblk=n_p // TB_PROMPT, off=0, bps=seq // TB_PROMPT,
                     row_ins=[x_all], mod_ins=[(modp, 3), (modp, 4)], seq_ins=[],
                     consts=consts + [_tri(TB_PROMPT), jnp.zeros((N_EXPERTS, 1), f32)],
                     outs=outs, scratch=scratch, name="router_p")
    h2, ids, gates, rank, cnt = _rows_call(
        _router_body, tb=n_s, nblk=1, off=n_p // n_s, bps=1,
        row_ins=[x_all], mod_ins=[(mods, 3), (mods, 4)], seq_ins=[],
        consts=consts + [_tri(n_s), res[4]], outs=outs, prev=res[:4], scratch=scratch, name="router_s")

    n_tiles = (2 * n_all + N_EXPERTS * (TM_MOE - 1)) // TM_MOE + 1
    counts = cnt[:, 0].astype(i32)
    padded = (counts + TM_MOE - 1) // TM_MOE * TM_MOE
    pend = jnp.cumsum(padded)
    n_used = (pend[-1:] // TM_MOE).astype(i32)
    tile_start = jnp.arange(n_tiles, dtype=i32) * TM_MOE
    tile_expert = jnp.minimum(jnp.sum((pend[None, :] <= tile_start[:, None]).astype(i32), axis=1),
                              N_EXPERTS - 1)
    pstart = (pend - padded).astype(i32)
    sel = ids[:, :, None] == jnp.arange(N_EXPERTS, dtype=i32)[None, None, :]
    dest = rank + jnp.sum(jnp.where(sel, pstart[None, None, :], 0), axis=2)
    gaps = jnp.stack([jnp.append(pstart + counts, pend[-1]),
                      jnp.append(pend, n_tiles * TM_MOE)], axis=1).reshape(-1).astype(i32)
    smem = pl.BlockSpec(memory_space=pltpu.SMEM)
    row_src = pl.pallas_call(
        _invperm_body, in_specs=[smem, smem], out_specs=smem,
        out_shape=jax.ShapeDtypeStruct((n_tiles * TM_MOE,), i32), name="moe_invperm")(dest.reshape(-1), gaps)

    def rows(a, idx):
        return a.at[idx].get(mode='promise_in_bounds')

    cuts = [k * n_tiles // XS_PARTS * TM_MOE for k in range(XS_PARTS + 1)]
    xs = [rows(h2, row_src[cuts[k]:cuts[k + 1]]) for k in range(XS_PARTS)]
    osort = _moe_grouped(xs, W['w_gate_up'], W['w_down'], l, tile_expert, n_used)
    oa = rows(osort, dest[0])
    ob = rows(osort, dest[1])
    gt = gates.T
    if not last:
        (x_new,) = _both_groups(
            _combine_body, dims, row_ins=[x_all, oa, ob, gt], mod_ks=(5,), consts=[],
            outs=[((n_all, d), f32, 'row')], name="combine")
        return x_new
    (y_p,) = _rows_call(_combine_body, tb=TB_PROMPT, nblk=n_p // TB_PROMPT, off=0, bps=seq // TB_PROMPT,
                        row_ins=[x_all, oa, ob, gt], mod_ins=[(modp, 5)], seq_ins=[], consts=[],
                        outs=[((n_p, d), f32, 'own')], name="combine_last_p")
    (y_s,) = _rows_call(_combine_body, tb=n_s, nblk=1, off=n_p // n_s, bps=1,
                        row_ins=[x_all, oa, ob, gt], mod_ins=[(mods, 5)], seq_ins=[], consts=[],
                        outs=[((n_s, d), f32, 'own')], name="combine_last_s")
    return y_p, y_s


def _pool_tail(pooled_slabs, w_ref, b_ref, ls_ref):
    ys = [_mm(p.astype(bf16), w_ref[g]) for g, p in enumerate(pooled_slabs)]
    return (jnp.concatenate(ys, axis=1) + b_ref[...]) * ls_ref[...]


def _pool_prompt_body(bps, x_ref, xh_ref, sh_ref, sc_ref, g1_ref, nw_ref, w_ref, b_ref, ls_ref,
                      o_ref, st_ref, ext_ref):
    tb = x_ref.shape[0]
    gw = w_ref.shape[1]
    blk = pl.program_id(0) % bps
    x = x_ref[...]
    nw, sc, sh = nw_ref[...], sc_ref[...], sh_ref[...]
    h = _norm_mod(x, nw, sc, sh)
    hh = _norm_mod(xh_ref[...], nw, sc, sh)
    ext_ref[0:HALO, :] = jnp.where(blk == 0, 0.0, hh)
    ext_ref[HALO:, :] = h
    pos = blk * tb + lax.broadcasted_iota(i32, (tb, 1), 0)
    slabs = []
    for g, wdw in enumerate(POOL_WINDOWS):
        lo, hi = g * gw, (g + 1) * gw
        acc = h[:, lo:hi]
        for j in range(1, wdw):
            acc = acc + ext_ref[HALO - j:HALO - j + tb, lo:hi]
        cnt = jnp.minimum(pos + 1, wdw).astype(f32)
        slabs.append(acc / cnt - h[:, lo:hi])
    y = _pool_tail(slabs, w_ref, b_ref, ls_ref)
    o_ref[...] = x + g1_ref[...] * y
    st_ref[...] = h[tb - HALO:, :]


def _pool_sample_body(n_t, pos0, x_ref, past_ref, sh_ref, sc_ref, g1_ref, nw_ref, w_ref, b_ref, ls_ref,
                      o_ref, h_ref):
    nb = past_ref.shape[1]
    gw = w_ref.shape[1]
    x = x_ref[...]
    h = _norm_mod(x, nw_ref[...], sc_ref[...], sh_ref[...])
    h_ref[...] = h

    def ext(r, lo, hi):
        if r < POOL_HIST:
            return past_ref[r][:, lo:hi]
        t = r - POOL_HIST
        return h[t * nb:(t + 1) * nb, lo:hi]

    slabs = []
    for g, wdw in enumerate(POOL_WINDOWS):
        lo, hi = g * gw, (g + 1) * gw
        rows = []
        for t in range(n_t):
            acc = ext(POOL_HIST + t, lo, hi)
            for j in range(1, wdw):
                acc = acc + ext(POOL_HIST + t - j, lo, hi)
            cnt = float(min(pos0 + t + 1, wdw))
            rows.append(acc / cnt - ext(POOL_HIST + t, lo, hi))
        slabs.append(jnp.concatenate(rows, axis=0))
    y = _pool_tail(slabs, w_ref, b_ref, ls_ref)
    o_ref[...] = x + g1_ref[...] * y


def _pool_layer(x_all, x_smp, smp_off, dims, l, ip, W, state_pool):
    d = x_all.shape[1]
    n_p, n_s, seq, nb, n_t, batch = (dims[k] for k in ('n_p', 'n_s', 'seq', 'nb', 'n_t', 'batch'))
    n_all = n_p + n_s
    modp, mods = dims['modp'], dims['mods']
    bps = seq // TB_POOL
    consts = [W['norm_mix'][l].reshape(1, d), W['w_pool'][ip].astype(bf16),
              W['b_pool'][ip].reshape(1, d), W['ls_pool'][ip].reshape(1, d)]

    def mspec(k):
        return pl.BlockSpec((None, None, 1, d), lambda i, _k=k: (_k, i // bps, 0, 0))

    ratio = TB_POOL // HALO
    x_new, st = pl.pallas_call(
        functools.partial(_pool_prompt_body, bps), grid=(n_p // TB_POOL,),
        in_specs=[pl.BlockSpec((TB_POOL, d), lambda i: (i, 0)),
                  pl.BlockSpec((HALO, d), lambda i: (jnp.maximum(i * ratio - 1, 0), 0)),
                  mspec(0), mspec(1), mspec(2)] + [_full_spec(c) for c in consts],
        out_specs=[pl.BlockSpec((TB_POOL, d), lambda i: (i, 0)),
                   pl.BlockSpec((None, HALO, d), lambda i: (i // bps, 0, 0))],
        out_shape=[jax.ShapeDtypeStruct((n_all, d), f32), jax.ShapeDtypeStruct((batch, HALO, d), f32)],
        scratch_shapes=[pltpu.VMEM((TB_POOL + HALO, d), f32)],
        compiler_params=_cparams("arbitrary"), name="pool_p")(
            x_all, x_all, modp, modp, modp, *consts)
    pool_prompt = st[:, HALO - POOL_HIST:, :]

    past = jnp.transpose(state_pool[ip], (1, 0, 2))
    off = n_p // n_s

    def sspec(k):
        return pl.BlockSpec((None, None, n_s, d), lambda i, _k=k: (_k, 0, 0, 0))

    def body(x_ref, past_ref, sh, sc, g1, nw, w, b, ls, prev_ref, o_ref, h_ref):
        _pool_sample_body(n_t, dims['past_len'], x_ref, past_ref, sh, sc, g1, nw, w, b, ls, o_ref, h_ref)

    x_new, h_s = pl.pallas_call(
        body, grid=(1,),
        in_specs=[pl.BlockSpec((n_s, d), lambda i: (smp_off, 0)), _full_spec(past),
                  sspec(0), sspec(1), sspec(2)] + [_full_spec(c) for c in consts]
                 + [pl.BlockSpec(memory_space=pl.ANY)],
        out_specs=[pl.BlockSpec((n_s, d), lambda i: (off, 0)), pl.BlockSpec((n_s, d), lambda i: (0, 0))],
        out_shape=[jax.ShapeDtypeStruct((n_all, d), f32), jax.ShapeDtypeStruct((n_s, d), f32)],
        input_output_aliases={5 + len(consts): 0},
        compiler_params=_cparams("arbitrary"), name="pool_s")(
            x_smp, past, mods, mods, mods, *consts, x_new)
    ext = jnp.concatenate([past, h_s.reshape(n_t, nb, d)], axis=0)
    pool_sample = jnp.transpose(ext[-POOL_HIST:], (1, 0, 2))
    return x_new, pool_prompt, pool_sample


def _ssm_param_body(lr_ref, li_ref, ldt_ref, apr_ref, api_ref, kr_ref, ki_ref):
    lr, li = lr_ref[...], li_ref[...]
    dt = jnp.exp(ldt_ref[...])
    mag = jnp.exp(lr * dt)
    a_r, a_i = mag * jnp.cos(li * dt), mag * jnp.sin(li * dt)
    den = lr * lr + li * li
    nr, ni = a_r - 1.0, a_i
    kr_ref[...] = (nr * lr + ni * li) / den
    ki_ref[...] = (ni * lr - nr * li) / den
    pr, pi_ = a_r, a_i
    for k in range(8):
        apr_ref[k] = pr
        api_ref[k] = pi_
        pr, pi_ = pr * a_r - pi_ * a_i, pr * a_i + pi_ * a_r


def _ssm_tables(W, iq):
    g, p = W['ssm_lambda_re'][iq].shape
    c = SSM_GROUP_CH
    ns = g // SLAB_GROUPS
    sl = SLAB_GROUPS
    shp = [jax.ShapeDtypeStruct((8, g, p), f32)] * 2 + [jax.ShapeDtypeStruct((g, p), f32)] * 2
    apr, api, k_r, k_i = pl.pallas_call(_ssm_param_body, out_shape=shp, name="ssm_params")(
        W['ssm_lambda_re'][iq], W['ssm_lambda_im'][iq], W['ssm_log_dt'][iq].reshape(g, 1))
    b_re, b_im = W['ssm_b_re'][iq], W['ssm_b_im'][iq]
    bb_re = k_r[..., None] * b_re - k_i[..., None] * b_im
    bb_im = k_r[..., None] * b_im + k_i[..., None] * b_re
    eye = jnp.eye(sl, dtype=f32)

    def b_slab(bb):
        b4 = bb.reshape(ns, sl, p, c)
        return jnp.einsum('sgpc,gh->sgchp', b4, eye).reshape(ns, sl * c, sl * p)

    def c_slab(cc):
        c4 = cc.reshape(ns, sl, c, p)
        return jnp.einsum('sgcp,gh->shpgc', c4, eye).reshape(ns, sl * p, sl * c)

    bmat = jnp.concatenate([b_slab(bb_re), b_slab(bb_im)], axis=2).astype(bf16)
    cmat = jnp.concatenate([c_slab(W['ssm_c_re'][iq]), -c_slab(W['ssm_c_im'][iq])], axis=1).astype(bf16)

    def lay(a):
        return jnp.transpose(a.reshape(8, ns, sl * p), (1, 0, 2))

    apw = jnp.stack([lay(apr), lay(api)], axis=1)
    return bmat, cmat, apw


def _ssm_out(x, h, y, g1_ref, dsk_ref, wglu_ref, bglu_ref):
    d = x.shape[1]
    z = _gelu_tanh(y + dsk_ref[...] * h)
    gu = _mm(z.astype(bf16), wglu_ref[...]) + bglu_ref[...]
    return x + g1_ref[...] * (gu[:, :d] * _sigmoid(gu[:, d:]))


def _ssm_prompt_body(x_ref, sh_ref, sc_ref, g1_ref, nw_ref, b_ref, c_ref, t2_ref, dsk_ref,
                     wglu_ref, bglu_ref, o_ref, st_ref, xs_ref, hs_ref, carry_ref, y_ref):
    t = x_ref.shape[0]
    ns = b_ref.shape[0]
    sw = b_ref.shape[1]
    half = b_ref.shape[2] // 2
    hc = half // LANES

    @pl.when(pl.program_id(1) == 0)
    def _():
        carry_ref[...] = jnp.zeros_like(carry_ref)

    x = x_ref[...]
    h = _norm_mod(x, nw_ref[...], sc_ref[...], sh_ref[...])
    hb = h.astype(bf16)
    def expand(s):
        xx = _mm(hb[:, s * sw:(s + 1) * sw], b_ref[s])
        for c in range(2 * hc):
            xs_ref[s % 3, c] = xx[:, c * LANES:(c + 1) * LANES]

    def group_scan(s):
        b = s % 3
        for c in range(hc):
            ar = jnp.broadcast_to(t2_ref[s, 0, 0:1, c * LANES:(c + 1) * LANES], (8, LANES))
            ai = jnp.broadcast_to(t2_ref[s, 1, 0:1, c * LANES:(c + 1) * LANES], (8, LANES))
            for base in range(0, t, 64):
                hr = xs_ref[b, c, pl.ds(base, 8, stride=8), :]
                hi = xs_ref[b, hc + c, pl.ds(base, 8, stride=8), :]
                for i in range(1, 8):
                    rows = pl.ds(base + i, 8, stride=8)
                    hr, hi = (ar * hr - ai * hi + xs_ref[b, c, rows, :],
                              ar * hi + ai * hr + xs_ref[b, hc + c, rows, :])
                    xs_ref[b, c, rows, :] = hr
                    xs_ref[b, hc + c, rows, :] = hi

    def carry_scan(s):
        b = s % 3

        def grp(j, car):
            cr, ci = car
            r0 = pl.multiple_of(j * 8, 8)
            xr = jnp.concatenate([xs_ref[b, c, pl.ds(r0, 8), :] for c in range(hc)], axis=1)
            xi = jnp.concatenate([xs_ref[b, hc + c, pl.ds(r0, 8), :] for c in range(hc)], axis=1)
            pr = t2_ref[s, 0]
            pi_ = t2_ref[s, 1]
            hr = xr + pr * cr - pi_ * ci
            hi = xi + pr * ci + pi_ * cr
            hs_ref[s % 2, pl.ds(r0, 8), 0:half] = hr
            hs_ref[s % 2, pl.ds(r0, 8), half:] = hi
            return (jnp.broadcast_to(hr[7:8, :], hr.shape), jnp.broadcast_to(hi[7:8, :], hi.shape))

        cr, ci = lax.fori_loop(0, t // 8, grp, (carry_ref[s, :, 0:half], carry_ref[s, :, half:]))
        carry_ref[s, :, 0:half] = cr
        carry_ref[s, :, half:] = ci
        st_ref[s:s + 1, :] = jnp.concatenate([cr[0:1, :], ci[0:1, :]], axis=1)

    expand(0)
    if ns > 1:
        expand(1)
    group_scan(0)
    for s in range(ns):
        carry_scan(s)
        y_ref[:, s * sw:(s + 1) * sw] = _mm(hs_ref[s % 2].astype(bf16), c_ref[s])
        if s + 2 < ns:
            expand(s + 2)
        if s + 1 < ns:
            group_scan(s + 1)
    y = y_ref[...]
    o_ref[...] = _ssm_out(x, h, y, g1_ref, dsk_ref, wglu_ref, bglu_ref)


def _ssm_sample_body(n_t, x_ref, s0_ref, sh_ref, sc_ref, g1_ref, nw_ref, b_ref, c_ref, t2_ref, dsk_ref,
                     wglu_ref, bglu_ref, prev_ref, o_ref, so_ref, y_ref):
    ns = b_ref.shape[0]
    sw = b_ref.shape[1]
    half = b_ref.shape[2] // 2
    nb = s0_ref.shape[1]
    x = x_ref[...]
    h = _norm_mod(x, nw_ref[...], sc_ref[...], sh_ref[...])
    hb = h.astype(bf16)
    for s in range(ns):
        s_r = s0_ref[s, :, 0:half]
        s_i = s0_ref[s, :, half:]
        ar = t2_ref[s, 0, 0:1, :]
        ai = t2_ref[s, 1, 0:1, :]
        for t in range(n_t):
            xx = _mm(hb[t * nb:(t + 1) * nb, s * sw:(s + 1) * sw], b_ref[s])
            s_r, s_i = ar * s_r - ai * s_i + xx[:, 0:half], ar * s_i + ai * s_r + xx[:, half:]
            st = jnp.concatenate([s_r, s_i], axis=1)
            y_ref[t * nb:(t + 1) * nb, s * sw:(s + 1) * sw] = _mm(st.astype(bf16), c_ref[s])
        so_ref[s] = jnp.concatenate([s_r, s_i], axis=1)
    o_ref[...] = _ssm_out(x, h, y_ref[...], g1_ref, dsk_ref, wglu_ref, bglu_ref)


def _ssm_layer(x_all, dims, l, iq, W, st_re, st_im):
    n_all, d = x_all.shape
    n_p, n_s, seq, nb, n_t, batch = (dims[k] for k in ('n_p', 'n_s', 'seq', 'nb', 'n_t', 'batch'))
    modp, mods = dims['modp'], dims['mods']
    bmat, cmat, t2 = _ssm_tables(W, iq)
    ns = bmat.shape[0]
    sp = bmat.shape[2]
    g, p = W['ssm_lambda_re'][iq].shape
    sl = SLAB_GROUPS
    tail = [W['ssm_d'][iq].reshape(1, d), W['w_glu'][iq].astype(bf16), W['b_glu'][iq].reshape(1, 2 * d)]
    nw = W['norm_mix'][l].reshape(1, d)
    nchunk = seq // T_SSM

    def mspec(k):
        return pl.BlockSpec((None, None, 1, d), lambda b, c, _k=k: (_k, b, 0, 0))

    def cspec(a):
        return pl.BlockSpec(a.shape, lambda b, c, _n=a.ndim: (0,) * _n)

    consts = [nw, bmat, cmat, t2] + tail
    x_new, st = pl.pallas_call(
        _ssm_prompt_body, grid=(batch, nchunk),
        in_specs=[pl.BlockSpec((T_SSM, d), lambda b, c: (b * nchunk + c, 0)), mspec(0), mspec(1), mspec(2)]
                 + [cspec(a) for a in consts],
        out_specs=[pl.BlockSpec((T_SSM, d), lambda b, c: (b * nchunk + c, 0)),
                   pl.BlockSpec((None, ns, sp), lambda b, c: (b, 0, 0))],
        out_shape=[jax.ShapeDtypeStruct((n_all, d), f32), jax.ShapeDtypeStruct((batch, ns, sp), f32)],
        scratch_shapes=[pltpu.VMEM((3, sp // LANES, T_SSM, LANES), f32), pltpu.VMEM((2, T_SSM, sp), f32),
                        pltpu.VMEM((ns, 8, sp), f32), pltpu.VMEM((T_SSM, d), f32)],
        compiler_params=_cparams("arbitrary", "arbitrary"), name="ssm_p")(
            x_all, modp, modp, modp, *consts)

    def unslab(a):
        a5 = a.reshape(a.shape[0], ns, 2, sl, p)
        return a5[:, :, 0].reshape(-1, g, p), a5[:, :, 1].reshape(-1, g, p)

    re_p, im_p = unslab(st)

    def slab(a):
        return a.reshape(a.shape[0], ns, sl * p)

    s0 = jnp.transpose(jnp.concatenate([slab(st_re), slab(st_im)], axis=2), (1, 0, 2))
    off = n_p // n_s

    def sspec(k):
        return pl.BlockSpec((None, None, n_s, d), lambda i, _k=k: (_k, 0, 0, 0))

    consts_s = [nw, bmat, cmat, t2] + tail
    x_new, so = pl.pallas_call(
        functools.partial(_ssm_sample_body, n_t), grid=(1,),
        in_specs=[pl.BlockSpec((n_s, d), lambda i: (off, 0)), _full_spec(s0), sspec(0), sspec(1), sspec(2)]
                 + [_full_spec(a) for a in consts_s] + [pl.BlockSpec(memory_space=pl.ANY)],
        out_specs=[pl.BlockSpec((n_s, d), lambda i: (off, 0)), _full_spec(s0)],
        out_shape=[jax.ShapeDtypeStruct((n_all, d), f32), jax.ShapeDtypeStruct(s0.shape, f32)],
        scratch_shapes=[pltpu.VMEM((n_s, d), f32)],
        input_output_aliases={5 + len(consts_s): 0},
        compiler_params=_cparams("arbitrary"), name="ssm_s")(
            x_all, s0, mods, mods, mods, *consts_s, x_new)
    re_s, im_s = unslab(jnp.transpose(so, (1, 0, 2)))
    return x_new, re_p, im_p, re_s, im_s


def _qkv_body(prompt, x_ref, sh_ref, sc_ref, cos_ref, sin_ref, nw_ref, w_ref, qn_ref, kn_ref, seg_ref, segt_ref,
              kb_ref, k_ref, v_ref, a_ref, b_ref):
    d = x_ref.shape[1]
    h = _norm_mod(x_ref[...], nw_ref[...], sc_ref[...], sh_ref[...])
    qkv = _mm(h.astype(bf16), w_ref[...])
    rep = d // LANES
    cos = jnp.concatenate([cos_ref[...]] * rep, axis=1)
    sin = jnp.concatenate([sin_ref[...]] * rep, axis=1)
    lane_lo = (lax.broadcasted_iota(i32, (1, d), 1) % HEAD_DIM) < (HEAD_DIM // 2)
    seg, segt = seg_ref[...], segt_ref[...]

    def norm_rope(t, g):
        s_hi, s_lo = _split2(t * t)
        ms = (_mm(s_hi, seg) + _mm(s_lo, seg)) * (1.0 / HEAD_DIM)
        r_hi, r_lo = _split2(lax.rsqrt(ms + NORM_EPS))
        rf = _mm(r_hi, segt) + _mm(r_lo, segt)
        tn = t * rf * g
        sw = jnp.where(lane_lo, pltpu.roll(tn, d - HEAD_DIM // 2, 1), pltpu.roll(tn, HEAD_DIM // 2, 1))
        return tn * cos + sw * sin

    q = norm_rope(qkv[:, :d], qn_ref[...])
    k = norm_rope(qkv[:, d:2 * d], kn_ref[...])
    v = qkv[:, 2 * d:]
    kb_ref[...] = k.astype(bf16)
    if prompt:
        tb = k.shape[0]
        for hh in range(N_HEADS):
            cols = slice(hh * 2 * HEAD_DIM, (hh + 1) * 2 * HEAD_DIM)
            k_ref[pl.ds(hh, tb, stride=N_HEADS), :] = k[:, cols]
            v_ref[pl.ds(hh, tb, stride=N_HEADS), :] = v[:, cols]
    else:
        k_ref[...] = k
        v_ref[...] = v
    if prompt:
        a_ref[...] = (q * (HEAD_DIM ** -0.5 * LOG2E)).T.astype(bf16)
        tk = b_ref.shape[2]
        for c in range(b_ref.shape[0]):
            b_ref[c] = v[c * tk:(c + 1) * tk, :].T.astype(bf16)
    else:
        a_ref[...] = (q * (HEAD_DIM ** -0.5)).astype(bf16)
        b_ref[...] = v.astype(bf16)


def _lam(lam_ref):
    l4 = lam_ref[...]
    a = jnp.sum(l4[0:1, :] * l4[1:2, :], axis=1, keepdims=True)
    b = jnp.sum(l4[2:3, :] * l4[3:4, :], axis=1, keepdims=True)
    return jnp.exp(a) - jnp.exp(b)


def _attn_finish(acc, l, lam, lam_init, sub_ref):
    r = acc.shape[0] // 2
    o = acc[:r] / l[:r] - lam * (acc[r:] / l[r:])
    o = o * lax.rsqrt(jnp.mean(o * o, axis=-1, keepdims=True) + NORM_EPS) * sub_ref[...]
    return o * (1.0 - lam_init)


def _flash_body(lam_init, qt_ref, k_ref, vt_ref, lam_ref, sub_ref, o_ref, acc_ref, s_ref, p_ref):
    tq = qt_ref.shape[1]
    tk = vt_ref.shape[2]
    ratio = tq // tk
    hd2 = 2 * HEAD_DIM
    nh = qt_ref.shape[0] // hd2
    qi = pl.program_id(2)
    top = lax.broadcasted_iota(i32, (hd2, 1), 0) < HEAD_DIM
    qqs = []
    for h in range(nh):
        qt = qt_ref[h * hd2:(h + 1) * hd2, :]
        zero = jnp.zeros_like(qt)
        qqs.append(jnp.concatenate([jnp.where(top, qt, zero), jnp.where(top, zero, qt)], axis=1))
    acc_ref[...] = jnp.zeros_like(acc_ref)
    p_ref[...] = jnp.zeros_like(p_ref)

    def scores(j):
        r0 = pl.multiple_of(j * tk, tk)
        return [_mm(k_ref[pl.ds(r0, tk), h * hd2:(h + 1) * hd2], qqs[h]) for h in range(nh)]

    def keep(ss, slot):
        for h in range(nh):
            s_ref[slot, h] = ss[h]

    def values(j):
        return [_mm(vt_ref[j, h * hd2:(h + 1) * hd2, :], p_ref[h]) for h in range(nh)]

    def softmax(slot, carry, key0):
        out = []
        for h in range(nh):
            m, l, _ = carry[h]
            s = s_ref[slot, h]
            if key0 is not None:
                key = lax.broadcasted_iota(i32, s.shape, 0) + key0
                qry = lax.broadcasted_iota(i32, s.shape, 1) % tq
                s = jnp.where(key <= qry, s, NEG_INF)
            m_new = jnp.maximum(m, jnp.max(s, axis=0, keepdims=True))
            alpha = jnp.exp2(m - m_new)
            p = jnp.exp2(s - m_new)
            p_ref[h] = p.astype(bf16)
            out.append((m_new, alpha * l + jnp.sum(p, axis=0, keepdims=True), alpha))
        return tuple(out)

    def accumulate(pvs, carry):
        for h in range(nh):
            acc_ref[h] = carry[h][2] * acc_ref[h] + pvs[h]

    def step(j, slot, carry):
        pvs = values(jnp.maximum(j - 1, 0))
        ss = scores(j + 1)
        new = softmax(slot, carry, None)
        keep(ss, 1 - slot)
        accumulate(pvs, carry)
        return new

    def finish(j, slot, carry):
        for u in range(ratio):
            pvs = values(jnp.maximum(j + u - 1, 0))
            if u + 1 < ratio:
                ss = scores(j + u + 1)
            new = softmax(slot, carry, u * tk)
            if u + 1 < ratio:
                keep(ss, 1 - slot)
            accumulate(pvs, carry)
            carry, slot = new, 1 - slot
        accumulate(values(j + ratio - 1), carry)
        return tuple(c[1] for c in carry)

    keep(scores(0), 0)
    init = tuple((jnp.full((1, 2 * tq), NEG_INF, f32), jnp.zeros((1, 2 * tq), f32),
                  jnp.ones((1, 2 * tq), f32)) for _ in range(nh))
    n_full = ratio * qi
    carry = lax.fori_loop(0, n_full // 2, lambda i, c: step(2 * i + 1, 1, step(2 * i, 0, c)), init)
    if ratio % 2 == 0:
        ls = finish(n_full, 0, carry)
    else:
        ls = lax.cond(n_full % 2 == 0, lambda c: finish(n_full, 0, c),
                      lambda c: finish(n_full, 1, step(n_full - 1, 0, c)), carry)
    lam = _lam(lam_ref) + lam_init
    for h in range(nh):
        acc = acc_ref[h]
        l = ls[h]
        o = acc[:, :tq] / l[:, :tq] - lam * (acc[:, tq:] / l[:, tq:])
        o = o * lax.rsqrt(jnp.mean(o * o, axis=0, keepdims=True) + NORM_EPS) * sub_ref[...]
        o_ref[:, h * hd2:(h + 1) * hd2] = (o * (1.0 - lam_init)).T.astype(o_ref.dtype)


def _attn_sample_body(lam_init, pp, pt_ref, q_ref, kn_ref, vn_ref, mp_ref, mn_ref, lam_ref, sub_ref, *rest):
    k_refs, v_refs = rest[:pp], rest[pp:2 * pp]
    o_ref = rest[2 * pp]
    m_ref, l_ref, acc_ref = rest[2 * pp + 1:]
    j = pl.program_id(1)
    q = q_ref[...]

    @pl.when(j == 0)
    def _():
        m_ref[...] = jnp.full_like(m_ref, NEG_INF)
        l_ref[...] = jnp.zeros_like(l_ref)
        acc_ref[...] = jnp.zeros_like(acc_ref)

    def lane_fold(op, a):
        parts = [a[:, c:c + LANES] for c in range(0, a.shape[1], LANES)]
        while len(parts) > 1:
            parts = [op(parts[i], parts[i + 1]) for i in range(0, len(parts) - 1, 2)] + parts[len(parts) & ~1:]
        return parts[0]

    def update(kfs, vfs, mask):
        ss = [_nt(q, kf) + mask for kf in kfs]
        mx = ss[0] if ss[0].shape[1] < LANES else functools.reduce(jnp.maximum, [lane_fold(jnp.maximum, s) for s in ss])
        m = m_ref[...]
        m_new = jnp.maximum(m, jnp.max(mx, axis=1, keepdims=True))
        alpha = jnp.exp(m - m_new)
        ps = [jnp.exp(s - m_new) for s in ss]
        sm = ps[0] if ps[0].shape[1] < LANES else functools.reduce(jnp.add, [lane_fold(jnp.add, p) for p in ps])
        l_ref[...] = alpha * l_ref[...] + jnp.sum(sm, axis=1, keepdims=True)
        pv = functools.reduce(jnp.add, [_mm(p.astype(bf16), vf) for p, vf in zip(ps, vfs)])
        acc_ref[...] = alpha * acc_ref[...] + pv
        m_ref[...] = m_new

    update([r[...].astype(bf16) for r in k_refs], [r[...].astype(bf16) for r in v_refs], mp_ref[...])

    @pl.when(j == pl.num_programs(1) - 1)
    def _():
        update([kn_ref[...]], [vn_ref[...]], mn_ref[...])
        lam = _lam(lam_ref) + lam_init
        o_ref[...] = _attn_finish(acc_ref[...], l_ref[...], lam, lam_init, sub_ref)


def _oproj_body(x_ref, o_ref_in, g1_ref, w_ref, o_ref):
    o_ref[...] = x_ref[...] + g1_ref[...] * _mm(o_ref_in[...], w_ref[...])


def _attn_layer(x_all, dims, l, ia, W, cache_k, cache_v, page_table):
    n_all, d = x_all.shape
    n_p, n_s, seq, nb, n_t, batch = (dims[k] for k in ('n_p', 'n_s', 'seq', 'nb', 'n_t', 'batch'))
    past_len = dims['past_len']
    hd2 = 2 * HEAD_DIM
    lam_init = 0.8 - 0.6 * math.exp(-0.3 * l)

    half = HEAD_DIM // 2
    inv = jnp.power(ROPE_THETA, -jnp.arange(half, dtype=f32) * (2.0 / HEAD_DIM))

    def tables(pos):
        ang = pos.astype(f32)[:, None] * inv[None, :]
        cos, sin = jnp.cos(ang), jnp.sin(ang)
        return (jnp.concatenate([cos] * 4, axis=1), jnp.concatenate([-sin, sin, -sin, sin], axis=1))

    cos_p, sin_p = tables(jnp.arange(seq))
    cos_s, sin_s = tables(past_len + jnp.repeat(jnp.arange(n_t), nb))
    seg = (jnp.arange(d)[:, None] // HEAD_DIM == jnp.arange(d // HEAD_DIM)[None, :]).astype(bf16)
    consts = [W['norm_mix'][l].reshape(1, d), W['w_qkv'][ia].astype(bf16),
              jnp.tile(W['q_norm'][ia], d // HEAD_DIM).reshape(1, d),
              jnp.tile(W['k_norm'][ia], d // HEAD_DIM).reshape(1, d), seg, seg.T]
    modp, mods = dims['modp'], dims['mods']
    tk = TK_ATTN
    kb, k_pr, v_pr, qt, vt = _rows_call(
        functools.partial(_qkv_body, True), tb=TB_PROMPT, nblk=n_p // TB_PROMPT, off=0, bps=seq // TB_PROMPT,
        row_ins=[x_all], mod_ins=[(modp, 0), (modp, 1)], seq_ins=[cos_p, sin_p], consts=consts,
        outs=[((n_all, d), bf16, 'row'), ((n_p * N_HEADS, hd2), f32, 'own'), ((n_p * N_HEADS, hd2), f32, 'own'),
              ((d, n_p), bf16, 'col'), ((n_p // tk, d, tk), bf16, 'blk3')], name="qkv_p")
    kb, k_sm, v_sm, qs, vb = _rows_call(
        functools.partial(_qkv_body, False), tb=n_s, nblk=1, off=n_p // n_s, bps=1,
        row_ins=[x_all], mod_ins=[(mods, 0), (mods, 1)], seq_ins=[cos_s, sin_s], consts=consts,
        outs=[((n_all, d), bf16, 'row'), ((n_s, d), f32, 'own'), ((n_s, d), f32, 'own'),
              ((n_s, d), bf16, 'own'), ((n_s, d), bf16, 'own')], prev=[kb], name="qkv_s")

    lam4 = jnp.stack([W['lambda_q1'][ia], W['lambda_k1'][ia], W['lambda_q2'][ia], W['lambda_k2'][ia]])
    sub = W['subln'][ia].reshape(1, hd2)
    sub_c = W['subln'][ia].reshape(hd2, 1)
    nq = seq // TQ_ATTN
    hw = HEADS_PER_STEP * hd2
    o_all = pl.pallas_call(
        functools.partial(_flash_body, lam_init), grid=(batch, N_HEADS // HEADS_PER_STEP, nq),
        in_specs=[pl.BlockSpec((hw, TQ_ATTN), lambda b, h, i: (h, b * nq + i)),
                  pl.BlockSpec((seq, hw), lambda b, h, i: (b, h)),
                  pl.BlockSpec((seq // tk, hw, tk), lambda b, h, i: (b, h, 0)),
                  pl.BlockSpec(lam4.shape, lambda b, h, i: (0, 0)),
                  pl.BlockSpec(sub_c.shape, lambda b, h, i: (0, 0))],
        out_specs=pl.BlockSpec((TQ_ATTN, hw), lambda b, h, i: (b * nq + i, h)),
        out_shape=jax.ShapeDtypeStruct((n_all, d), bf16),
        scratch_shapes=[pltpu.VMEM((HEADS_PER_STEP, hd2, 2 * TQ_ATTN), f32),
                        pltpu.VMEM((2, HEADS_PER_STEP, tk, 2 * TQ_ATTN), f32),
                        pltpu.VMEM((HEADS_PER_STEP, tk, 2 * TQ_ATTN), bf16)],
        compiler_params=_cparams("arbitrary", "arbitrary", "arbitrary"), name="flash_p")(
            qt, kb, vt, lam4, sub_c)

    na, n_phys, page, _, _ = cache_k.shape
    flat = page * N_HEADS
    ck = cache_k.reshape(na * n_phys, flat, hd2)
    cv = cache_v.reshape(na * n_phys, flat, hd2)
    n_pages = page_table.shape[1]
    pp = PAGES_PER_STEP if n_pages % PAGES_PER_STEP == 0 else 1
    rows = 2 * N_HEADS * n_t
    q5 = jnp.transpose(qs.reshape(n_t, nb, N_HEADS, 2, HEAD_DIM), (1, 3, 2, 0, 4))
    z = jnp.zeros_like(q5[:, 0])
    qm = jnp.stack([jnp.concatenate([q5[:, 0], z], -1), jnp.concatenate([z, q5[:, 1]], -1)], axis=1)
    qm = qm.reshape(nb, rows, hd2)

    def new_rows(a):
        return jnp.transpose(a.reshape(n_t, nb, N_HEADS, hd2), (1, 0, 2, 3)).reshape(nb, n_t * N_HEADS, hd2)

    kn, vn = new_rows(kb[n_p:]), new_rows(vb)
    r_h = (jnp.arange(rows) % (N_HEADS * n_t)) // n_t
    r_t = jnp.arange(rows) % n_t
    mask_p = jnp.where(r_h[:, None] == (jnp.arange(flat) % N_HEADS)[None, :], 0.0, NEG_INF).astype(f32)
    cn = jnp.arange(n_t * N_HEADS)
    mask_n = jnp.where((r_h[:, None] == (cn % N_HEADS)[None, :]) & ((cn // N_HEADS)[None, :] <= r_t[:, None]),
                       0.0, NEG_INF).astype(f32)
    base = ia * n_phys

    def seq_spec(shape):
        return pl.BlockSpec((None,) + shape, lambda s, j, pt: (s, 0, 0))

    def cst_spec(a):
        return pl.BlockSpec(a.shape, lambda s, j, pt: (0, 0))

    def page_spec(u):
        return pl.BlockSpec((None, flat, hd2), lambda s, j, pt, _u=u: (base + pt[s, j * pp + _u], 0, 0))

    gs = pltpu.PrefetchScalarGridSpec(
        num_scalar_prefetch=1, grid=(nb, n_pages // pp),
        in_specs=[seq_spec((rows, hd2)), seq_spec((n_t * N_HEADS, hd2)), seq_spec((n_t * N_HEADS, hd2)),
                  cst_spec(mask_p), cst_spec(mask_n), cst_spec(lam4), cst_spec(sub)]
                 + [page_spec(u) for u in range(pp)] + [page_spec(u) for u in range(pp)],
        out_specs=pl.BlockSpec((None, rows // 2, hd2), lambda s, j, pt: (s, 0, 0)),
        scratch_shapes=[pltpu.VMEM((rows, 1), f32), pltpu.VMEM((rows, 1), f32), pltpu.VMEM((rows, hd2), f32)])
    o_s = pl.pallas_call(
        functools.partial(_attn_sample_body, lam_init, pp), grid_spec=gs,
        out_shape=jax.ShapeDtypeStruct((nb, rows // 2, hd2), f32),
        compiler_params=_cparams("arbitrary", "arbitrary"), name="attn_s")(
            page_table, qm, kn, vn, mask_p, mask_n, lam4, sub, *([ck] * pp), *([cv] * pp))
    o_s = jnp.transpose(o_s.reshape(nb, N_HEADS, n_t, hd2), (2, 0, 1, 3)).reshape(n_s, d)
    o_all = lax.dynamic_update_slice(o_all, o_s.astype(bf16), (n_p, 0))

    (x_new,) = _both_groups(
        _oproj_body, dims, row_ins=[x_all, o_all], mod_ks=(2,), consts=[W['w_o'][ia].astype(bf16)],
        outs=[((n_all, d), f32, 'row')], name="oproj")

    def smp(a):
        return jnp.transpose(a.reshape(n_t, nb, N_HEADS, hd2), (1, 0, 2, 3))

    shp = (batch, seq, N_HEADS, hd2)
    return x_new, k_pr.reshape(shp), v_pr.reshape(shp), smp(k_sm), smp(v_sm)


def kernel(x_prompt, x_sample, cache_k, cache_v, state_pool, state_ssm_re, state_ssm_im, page_table, c_prompt, c_sample, w_ada, b_ada, norm_mix, norm_ff, w_pool, b_pool, ls_pool, ssm_lambda_re, ssm_lambda_im, ssm_log_dt, ssm_b_re, ssm_b_im, ssm_c_re, ssm_c_im, ssm_d, w_glu, b_glu, w_qkv, q_norm, k_norm, lambda_q1, lambda_k1, lambda_q2, lambda_k2, subln, w_o, w_router, b_router, w_gate_up, w_down):
    batch, seq, d = x_prompt.shape
    nb, n_t, _ = x_sample.shape
    depth = w_ada.shape[0]
    n_p, n_s = batch * seq, nb * n_t
    past_len = page_table.shape[1] * cache_k.shape[2]
    W = dict(norm_mix=norm_mix, norm_ff=norm_ff, w_pool=w_pool, b_pool=b_pool, ls_pool=ls_pool,
             ssm_lambda_re=ssm_lambda_re, ssm_lambda_im=ssm_lambda_im, ssm_log_dt=ssm_log_dt,
             ssm_b_re=ssm_b_re, ssm_b_im=ssm_b_im, ssm_c_re=ssm_c_re, ssm_c_im=ssm_c_im, ssm_d=ssm_d,
             w_glu=w_glu, b_glu=b_glu, w_qkv=w_qkv, q_norm=q_norm, k_norm=k_norm, lambda_q1=lambda_q1,
             lambda_k1=lambda_k1, lambda_q2=lambda_q2, lambda_k2=lambda_k2, subln=subln, w_o=w_o,
             w_router_t=w_router.T, b_router_c=b_router.reshape(-1, 1),
             w_gate_up=w_gate_up, w_down=w_down)

    ada = _ada_all(jnp.concatenate([c_prompt, c_sample], axis=0), w_ada, b_ada)
    x_all = x_prompt.reshape(n_p, d)
    x_smp, smp_off = jnp.transpose(x_sample, (1, 0, 2)).reshape(n_s, d), 0
    dims = dict(n_p=n_p, n_s=n_s, seq=seq, nb=nb, n_t=n_t, batch=batch, past_len=past_len)

    pools_p, pools_s, k_ps, v_ps, k_ss, v_ss = [], [], [], [], [], []
    re_ps, im_ps, re_ss, im_ss = [], [], [], []
    ip = iq = ia = 0
    for l in range(depth):
        chunks = jnp.transpose(ada[l].reshape(batch + nb, 6, d), (1, 0, 2))
        dims['modp'] = chunks[:, :batch, None, :]
        dims['mods'] = jnp.tile(chunks[:, batch:], (1, n_t, 1))[:, None]
        kind = l % N_MIXERS
        if kind == 0:
            x_all, pp_, ps_ = _pool_layer(x_all, x_smp, smp_off, dims, l, ip, W, state_pool)
            pools_p.append(pp_)
            pools_s.append(ps_)
            ip += 1
        elif kind == 1:
            x_all, rp, imp, rs, ims = _ssm_layer(x_all, dims, l, iq, W, state_ssm_re[iq], state_ssm_im[iq])
            re_ps.append(rp)
            im_ps.append(imp)
            re_ss.append(rs)
            im_ss.append(ims)
            iq += 1
        else:
            x_all, kp, vp, ks, vs = _attn_layer(x_all, dims, l, ia, W, cache_k, cache_v, page_table)
            k_ps.append(kp)
            v_ps.append(vp)
            k_ss.append(ks)
            v_ss.append(vs)
            ia += 1
        x_all = _moe_layer(x_all, dims, l, W, l == depth - 1)
        x_smp, smp_off = x_all, n_p // n_s

    y_prompt = x_all[0].reshape(batch, seq, d)
    y_sample = jnp.transpose(x_all[1].reshape(n_t, nb, d), (1, 0, 2))
    return (y_prompt, y_sample, jnp.stack(k_ps), jnp.stack(v_ps), jnp.stack(k_ss), jnp.stack(v_ss),
            jnp.stack(pools_p), jnp.stack(pools_s), jnp.stack(re_ps), jnp.stack(im_ps),
            jnp.stack(re_ss), jnp.stack(im_ss))
```

```python
import functools
import math

import jax
import jax.numpy as jnp
from jax import lax
from jax.experimental import pallas as pl
from jax.experimental.pallas import tpu as pltpu

f32 = jnp.float32
bf16 = jnp.bfloat16
i32 = jnp.int32

N_MIXERS = 3
POOL_WINDOWS = (2, 4, 8, 16)
POOL_HIST = 15
HALO = 16
SSM_GROUP_CH = 16
SSM_STATE = 64
SLAB_GROUPS = 8
N_HEADS = 8
HEAD_DIM = 64
ROPE_THETA = 10000.0
N_EXPERTS = 16
EXPERTS_PER_GROUP = 4
NORM_EPS = 1e-6
NEG_INF = -1e30
LOG2E = 1.4426950408889634
LANES = 128
VMEM_LIMIT = 48 * 1024 * 1024

TB_PROMPT = 512
TB_POOL = 512
T_SSM = 512
TQ_ATTN = 256
TK_ATTN = 256
HEADS_PER_STEP = 2
TM_MOE = 512
H2_ROWS_FACTOR = 3
XS_PARTS = 3
INVPERM_UNROLL = 8
PAGES_PER_STEP = 16


def _cparams(*sem):
    return pltpu.CompilerParams(dimension_semantics=sem, vmem_limit_bytes=VMEM_LIMIT)


def _mm(a, b):
    return jnp.dot(a, b, preferred_element_type=f32)


def _nt(a, b):
    return lax.dot_general(a, b, (((1,), (1,)), ((), ())), preferred_element_type=f32)


def _split2(a):
    hi = a.astype(bf16)
    lo = (a - hi.astype(f32)).astype(bf16)
    return hi, lo


def _norm_mod(x, g, sc, sh):
    ms = jnp.mean(x * x, axis=-1, keepdims=True)
    return x * lax.rsqrt(ms + NORM_EPS) * g * (1.0 + sc) + sh


def _gelu_tanh(y):
    return 0.5 * y * (1.0 + jnp.tanh(0.7978845608028654 * (y + 0.044715 * y * y * y)))


def _sigmoid(x):
    return 1.0 / (1.0 + jnp.exp(-x))


def _full_spec(a):
    nd = a.ndim
    return pl.BlockSpec(a.shape, lambda i, _n=nd: (0,) * _n)


def _rows_call(body, *, tb, nblk, off, bps, row_ins, mod_ins, seq_ins, consts, outs, prev=None,
               scratch=(), name):
    in_specs, args = [], []
    for a in row_ins:
        a, o = a if isinstance(a, tuple) else (a, off)
        in_specs.append(pl.BlockSpec((tb, a.shape[1]), lambda i, _o=o: (i + _o, 0)))
        args.append(a)
    for a, k in mod_ins:
        r, d = a.shape[2], a.shape[3]
        in_specs.append(pl.BlockSpec((None, None, r, d), lambda i, _k=k: (_k, i // bps, 0, 0)))
        args.append(a)
    for a in seq_ins:
        in_specs.append(pl.BlockSpec((tb, a.shape[1]), lambda i: (i % bps, 0)))
        args.append(a)
    for a in consts:
        in_specs.append(_full_spec(a))
        args.append(a)
    n_real = len(args)
    aliases = {}
    if prev is not None:
        for j, a in enumerate(prev):
            in_specs.append(pl.BlockSpec(memory_space=pl.ANY))
            args.append(a)
            aliases[n_real + j] = j
    out_specs, out_shapes = [], []
    for shape, dtype, kind in outs:
        if kind == 'row':
            out_specs.append(pl.BlockSpec((tb, shape[1]), lambda i: (i + off, 0)))
        elif kind == 'own':
            out_specs.append(pl.BlockSpec((shape[0] // nblk, shape[1]), lambda i: (i, 0)))
        elif kind == 'col':
            out_specs.append(pl.BlockSpec((shape[0], tb), lambda i: (0, i + off)))
        elif kind == 'blk3':
            out_specs.append(pl.BlockSpec((tb // shape[2], shape[1], shape[2]), lambda i: (i, 0, 0)))
        else:
            out_specs.append(pl.BlockSpec(shape, lambda i, _n=len(shape): (0,) * _n))
        out_shapes.append(jax.ShapeDtypeStruct(shape, dtype))
    n_prev = 0 if prev is None else len(prev)

    def wrapped(*refs):
        body(*refs[:n_real], *refs[n_real + n_prev:])

    res = pl.pallas_call(
        wrapped, grid=(nblk,), in_specs=in_specs, out_specs=out_specs, out_shape=out_shapes,
        scratch_shapes=list(scratch), input_output_aliases=aliases,
        compiler_params=_cparams("arbitrary"), name=name)(*args)
    return list(res)


def _both_groups(body, dims, *, row_ins, mod_ks, consts, outs, name, seq_ins_p=(), seq_ins_s=()):
    modp, mods = dims['modp'], dims['mods']
    n_p, n_s, seq = dims['n_p'], dims['n_s'], dims['seq']
    res = _rows_call(body, tb=TB_PROMPT, nblk=n_p // TB_PROMPT, off=0, bps=seq // TB_PROMPT,
                     row_ins=row_ins, mod_ins=[(modp, k) for k in mod_ks], seq_ins=list(seq_ins_p),
                     consts=consts, outs=outs, name=name + "_p")
    res = _rows_call(body, tb=n_s, nblk=1, off=n_p // n_s, bps=1,
                     row_ins=row_ins, mod_ins=[(mods, k) for k in mod_ks], seq_ins=list(seq_ins_s),
                     consts=consts, outs=outs, prev=res, name=name + "_s")
    return res


def _ada_body(c_ref, w_ref, b_ref, o_ref):
    c = c_ref[...]
    cond = c * _sigmoid(c)
    c_hi, c_lo = _split2(cond)
    w_hi, w_lo = _split2(w_ref[...])
    o_ref[...] = _mm(c_hi, w_hi) + _mm(c_hi, w_lo) + _mm(c_lo, w_hi) + b_ref[...]


def _ada_all(c_all, w_ada, b_ada):
    depth, d, d6 = w_ada.shape
    n = c_all.shape[0]
    nj = d6 // d
    return pl.pallas_call(
        _ada_body, grid=(depth, nj),
        in_specs=[pl.BlockSpec((n, d), lambda l, j: (0, 0)),
                  pl.BlockSpec((None, d, d), lambda l, j: (l, 0, j)),
                  pl.BlockSpec((None, 1, d), lambda l, j: (l, 0, j))],
        out_specs=pl.BlockSpec((None, n, d), lambda l, j: (l, 0, j)),
        out_shape=jax.ShapeDtypeStruct((depth, n, d6), f32),
        compiler_params=_cparams("arbitrary", "arbitrary"), name="ada")(
            c_all, w_ada, b_ada.reshape(depth, 1, d6))


def _router_body(x_ref, sh_ref, sc_ref, nw_ref, wrt_ref, br_ref, tri_ref, cnt0_ref,
                 h_ref, ids_ref, gates_ref, rank_ref, cnt_ref, carry_ref):
    @pl.when(pl.program_id(0) == 0)
    def _():
        carry_ref[...] = cnt0_ref[...]

    h = _norm_mod(x_ref[...], nw_ref[...], sc_ref[...], sh_ref[...])
    h_ref[...] = h.astype(bf16)
    h_hi, h_lo = _split2(h)
    w_hi, w_lo = _split2(wrt_ref[...])
    lg = _nt(w_hi, h_hi) + _nt(w_hi, h_lo) + _nt(w_lo, h_hi) + br_ref[...]
    e = jnp.exp(lg - jnp.max(lg, axis=0, keepdims=True))
    best = None
    for g in range(N_EXPERTS // EXPERTS_PER_GROUP):
        v = [e[EXPERTS_PER_GROUP * g + j:EXPERTS_PER_GROUP * g + j + 1, :] for j in range(EXPERTS_PER_GROUP)]
        t1 = jnp.maximum(jnp.maximum(v[0], v[1]), jnp.maximum(v[2], v[3]))
        i1 = jnp.where(v[0] == t1, 0, jnp.where(v[1] == t1, 1, jnp.where(v[2] == t1, 2, 3)))
        w = [jnp.where(i1 == j, -1.0, v[j]) for j in range(EXPERTS_PER_GROUP)]
        t2 = jnp.maximum(jnp.maximum(w[0], w[1]), jnp.maximum(w[2], w[3]))
        i2 = jnp.where(w[0] == t2, 0, jnp.where(w[1] == t2, 1, jnp.where(w[2] == t2, 2, 3)))
        cand = (t1 + t2, t1, t2, i1 + EXPERTS_PER_GROUP * g, i2 + EXPERTS_PER_GROUP * g)
        if best is None:
            best = cand
        else:
            upd = cand[0] > best[0]
            best = tuple(jnp.where(upd, c, b) for c, b in zip(cand, best))
    s, t1, t2, i1, i2 = best
    ids_ref[0:1, :] = i1.astype(i32)
    ids_ref[1:2, :] = i2.astype(i32)
    gates_ref[0:1, :] = t1 / s
    gates_ref[1:2, :] = t2 / s
    eidx = lax.broadcasted_iota(i32, lg.shape, 0)
    oh1 = (eidx == i1).astype(f32)
    oh2 = (eidx == i2).astype(f32)
    tri = tri_ref[...]
    ex1 = _mm(oh1.astype(bf16), tri)
    ex2 = _mm(oh2.astype(bf16), tri)
    carry = carry_ref[...]
    tot1 = jnp.sum(oh1, axis=1, keepdims=True)
    tot2 = jnp.sum(oh2, axis=1, keepdims=True)
    rank_ref[0:1, :] = jnp.sum(oh1 * (carry + ex1), axis=0, keepdims=True).astype(i32)
    rank_ref[1:2, :] = jnp.sum(oh2 * (carry + tot1 + ex2), axis=0, keepdims=True).astype(i32)
    carry_ref[...] = carry + tot1 + tot2
    cnt_ref[...] = carry_ref[...]


def _invperm_body(dest_ref, gap_ref, src_ref):
    n_tok = dest_ref.shape[0] // 2
    spread = (1 << (n_tok.bit_length() - 1)) - 1

    def fill(i, c):
        src_ref[i] = i & spread
        return c

    for g in range(gap_ref.shape[0] // 2):
        lax.fori_loop(gap_ref[2 * g], gap_ref[2 * g + 1], fill, 0)

    def place(t, c):
        src_ref[dest_ref[t]] = t
        src_ref[dest_ref[n_tok + t]] = t
        return c

    lax.fori_loop(0, n_tok, place, 0, unroll=INVPERM_UNROLL)


def _moe_body(t0, te_ref, nu_ref, x_ref, wgu_ref, wdn_ref, *refs):
    o_ref, wgu_b, wdn_b = refs[-3:]
    i = pl.program_id(0) + t0
    f = wdn_ref.shape[0]

    @pl.when(jnp.logical_or(i == t0, te_ref[i] != te_ref[jnp.maximum(i - 1, 0)]))
    def _():
        wgu_b[...] = wgu_ref[...].astype(bf16)
        wdn_b[...] = wdn_ref[...].astype(bf16)

    @pl.when(i < nu_ref[0])
    def _():
        gu = _mm(x_ref[...], wgu_b[...])
        g = gu[:, :f]
        act = g * _sigmoid(g) * gu[:, f:]
        o_ref[...] = _mm(act.astype(bf16), wdn_b[...]).astype(bf16)

    @pl.when(i >= nu_ref[0])
    def _():
        o_ref[...] = jnp.zeros_like(o_ref)


def _moe_grouped(xs, wgu, wdn, l, tile_expert, n_used):
    d = xs[0].shape[1]
    f2 = wgu.shape[3]
    f = wdn.shape[2]
    n_tiles = sum(x.shape[0] for x in xs) // TM_MOE
    out, t0 = None, 0
    for x in xs:
        nt = x.shape[0] // TM_MOE
        in_specs = [pl.BlockSpec((TM_MOE, d), lambda i, te, nu: (i, 0)),
                    pl.BlockSpec((None, None, d, f2), lambda i, te, nu, _t=t0: (l, te[i + _t], 0, 0)),
                    pl.BlockSpec((None, None, f, d), lambda i, te, nu, _t=t0: (l, te[i + _t], 0, 0))]
        args = [tile_expert, n_used, x, wgu, wdn]
        aliases = {}
        if out is not None:
            in_specs.append(pl.BlockSpec(memory_space=pl.ANY))
            args.append(out)
            aliases = {5: 0}
        gs = pltpu.PrefetchScalarGridSpec(
            num_scalar_prefetch=2, grid=(nt,), in_specs=in_specs,
            out_specs=pl.BlockSpec((TM_MOE, d), lambda i, te, nu, _t=t0: (i + _t, 0)),
            scratch_shapes=[pltpu.VMEM((d, f2), bf16), pltpu.VMEM((f, d), bf16)])
        out = pl.pallas_call(functools.partial(_moe_body, t0), grid_spec=gs,
                             out_shape=jax.ShapeDtypeStruct((n_tiles * TM_MOE, d), bf16),
                             input_output_aliases=aliases,
                             compiler_params=_cparams("arbitrary"), name="moe_experts")(*args)
        t0 += nt
    return out


def _combine_body(x_ref, oa_ref, ob_ref, gt_ref, g2_ref, o_ref):
    gt = gt_ref[...]
    y = gt[:, 0:1] * oa_ref[...].astype(f32) + gt[:, 1:2] * ob_ref[...].astype(f32)
    o_ref[...] = x_ref[...] + g2_ref[...] * y


def _tri(n):
    return (jnp.arange(n)[:, None] < jnp.arange(n)[None, :]).astype(bf16)


def _moe_layer(x_all, dims, l, W, last):
    n_all, d = x_all.shape
    n_p, n_s, seq = dims['n_p'], dims['n_s'], dims['seq']
    modp, mods = dims['modp'], dims['mods']
    consts = [W['norm_ff'][l].reshape(1, d), W['w_router_t'], W['b_router_c']]
    outs = [((H2_ROWS_FACTOR * n_all, d), bf16, 'row'), ((2, n_all), i32, 'col'), ((2, n_all), f32, 'col'),
            ((2, n_all), i32, 'col'), ((N_EXPERTS, 1), f32, 'const')]
    scratch = [pltpu.VMEM((N_EXPERTS, 1), f32)]
    res = _rows_call(_router_body, tb=TB_PROMPT, nblk=n_p // TB_PROMPT, off=0, bps=seq // TB_PROMPT,
                     row_ins=[x_all], mod_ins=[(modp, 3), (modp, 4)], seq_ins=[],
                     consts=consts + [_tri(TB_PROMPT), jnp.zeros((N_EXPERTS, 1), f32)],
                     outs=outs, scratch=scratch, name="router_p")
    h2, ids, gates, rank, cnt = _rows_call(
        _router_body, tb=n_s, nblk=1, off=n_p // n_s, bps=1,
        row_ins=[x_all], mod_ins=[(mods, 3), (mods, 4)], seq_ins=[],
        consts=consts + [_tri(n_s), res[4]], outs=outs, prev=res[:4], scratch=scratch, name="router_s")

    n_tiles = (2 * n_all + N_EXPERTS * (TM_MOE - 1)) // TM_MOE + 1
    counts = cnt[:, 0].astype(i32)
    padded = (counts + TM_MOE - 1) // TM_MOE * TM_MOE
    pend = jnp.cumsum(padded)
    n_used = (pend[-1:] // TM_MOE).astype(i32)
    tile_start = jnp.arange(n_tiles, dtype=i32) * TM_MOE
    tile_expert = jnp.minimum(jnp.sum((pend[None, :] <= tile_start[:, None]).astype(i32), axis=1),
                              N_EXPERTS - 1)
    pstart = (pend - padded).astype(i32)
    sel = ids[:, :, None] == jnp.arange(N_EXPERTS, dtype=i32)[None, None, :]
    dest = rank + jnp.sum(jnp.where(sel, pstart[None, None, :], 0), axis=2)
    gaps = jnp.stack([jnp.append(pstart + counts, pend[-1]),
                      jnp.append(pend, n_tiles * TM_MOE)], axis=1).reshape(-1).astype(i32)
    smem = pl.BlockSpec(memory_space=pltpu.SMEM)
    row_src = pl.pallas_call(
        _invperm_body, in_specs=[smem, smem], out_specs=smem,
        out_shape=jax.ShapeDtypeStruct((n_tiles * TM_MOE,), i32), name="moe_invperm")(dest.reshape(-1), gaps)

    def rows(a, idx):
        return a.at[idx].get(mode='promise_in_bounds')

    cuts = [k * n_tiles // XS_PARTS * TM_MOE for k in range(XS_PARTS + 1)]
    xs = [rows(h2, row_src[cuts[k]:cuts[k + 1]]) for k in range(XS_PARTS)]
    osort = _moe_grouped(xs, W['w_gate_up'], W['w_down'], l, tile_expert, n_used)
    oa = rows(osort, dest[0])
    ob = rows(osort, dest[1])
    gt = gates.T
    if not last:
        (x_new,) = _both_groups(
            _combine_body, dims, row_ins=[x_all, oa, ob, gt], mod_ks=(5,), consts=[],
            outs=[((n_all, d), f32, 'row')], name="combine")
        return x_new
    (y_p,) = _rows_call(_combine_body, tb=TB_PROMPT, nblk=n_p // TB_PROMPT, off=0, bps=seq // TB_PROMPT,
                        row_ins=[x_all, oa, ob, gt], mod_ins=[(modp, 5)], seq_ins=[], consts=[],
                        outs=[((n_p, d), f32, 'own')], name="combine_last_p")
    (y_s,) = _rows_call(_combine_body, tb=n_s, nblk=1, off=n_p // n_s, bps=1,
                        row_ins=[x_all, oa, ob, gt], mod_ins=[(mods, 5)], seq_ins=[], consts=[],
                        outs=[((n_s, d), f32, 'own')], name="combine_last_s")
    return y_p, y_s


def _pool_tail(pooled_slabs, w_ref, b_ref, ls_ref):
    ys = [_mm(p.astype(bf16), w_ref[g]) for g, p in enumerate(pooled_slabs)]
    return (jnp.concatenate(ys, axis=1) + b_ref[...]) * ls_ref[...]


def _pool_prompt_body(bps, x_ref, xh_ref, sh_ref, sc_ref, g1_ref, nw_ref, w_ref, b_ref, ls_ref,
                      o_ref, st_ref, ext_ref):
    tb = x_ref.shape[0]
    gw = w_ref.shape[1]
    blk = pl.program_id(0) % bps
    x = x_ref[...]
    nw, sc, sh = nw_ref[...], sc_ref[...], sh_ref[...]
    h = _norm_mod(x, nw, sc, sh)
    hh = _norm_mod(xh_ref[...], nw, sc, sh)
    ext_ref[0:HALO, :] = jnp.where(blk == 0, 0.0, hh)
    ext_ref[HALO:, :] = h
    pos = blk * tb + lax.broadcasted_iota(i32, (tb, 1), 0)
    slabs = []
    for g, wdw in enumerate(POOL_WINDOWS):
        lo, hi = g * gw, (g + 1) * gw
        acc = h[:, lo:hi]
        for j in range(1, wdw):
            acc = acc + ext_ref[HALO - j:HALO - j + tb, lo:hi]
        cnt = jnp.minimum(pos + 1, wdw).astype(f32)
        slabs.append(acc / cnt - h[:, lo:hi])
    y = _pool_tail(slabs, w_ref, b_ref, ls_ref)
    o_ref[...] = x + g1_ref[...] * y
    st_ref[...] = h[tb - HALO:, :]


def _pool_sample_body(n_t, pos0, x_ref, past_ref, sh_ref, sc_ref, g1_ref, nw_ref, w_ref, b_ref, ls_ref,
                      o_ref, h_ref):
    nb = past_ref.shape[1]
    gw = w_ref.shape[1]
    x = x_ref[...]
    h = _norm_mod(x, nw_ref[...], sc_ref[...], sh_ref[...])
    h_ref[...] = h

    def ext(r, lo, hi):
        if r < POOL_HIST:
            return past_ref[r][:, lo:hi]
        t = r - POOL_HIST
        return h[t * nb:(t + 1) * nb, lo:hi]

    slabs = []
    for g, wdw in enumerate(POOL_WINDOWS):
        lo, hi = g * gw, (g + 1) * gw
        rows = []
        for t in range(n_t):
            acc = ext(POOL_HIST + t, lo, hi)
            for j in range(1, wdw):
                acc = acc + ext(POOL_HIST + t - j, lo, hi)
            cnt = float(min(pos0 + t + 1, wdw))
            rows.append(acc / cnt - ext(POOL_HIST + t, lo, hi))
        slabs.append(jnp.concatenate(rows, axis=0))
    y = _pool_tail(slabs, w_ref, b_ref, ls_ref)
    o_ref[...] = x + g1_ref[...] * y


def _pool_layer(x_all, x_smp, smp_off, dims, l, ip, W, state_pool):
    d = x_all.shape[1]
    n_p, n_s, seq, nb, n_t, batch = (dims[k] for k in ('n_p', 'n_s', 'seq', 'nb', 'n_t', 'batch'))
    n_all = n_p + n_s
    modp, mods = dims['modp'], dims['mods']
    bps = seq // TB_POOL
    consts = [W['norm_mix'][l].reshape(1, d), W['w_pool'][ip].astype(bf16),
              W['b_pool'][ip].reshape(1, d), W['ls_pool'][ip].reshape(1, d)]

    def mspec(k):
        return pl.BlockSpec((None, None, 1, d), lambda i, _k=k: (_k, i // bps, 0, 0))

    ratio = TB_POOL // HALO
    x_new, st = pl.pallas_call(
        functools.partial(_pool_prompt_body, bps), grid=(n_p // TB_POOL,),
        in_specs=[pl.BlockSpec((TB_POOL, d), lambda i: (i, 0)),
                  pl.BlockSpec((HALO, d), lambda i: (jnp.maximum(i * ratio - 1, 0), 0)),
                  mspec(0), mspec(1), mspec(2)] + [_full_spec(c) for c in consts],
        out_specs=[pl.BlockSpec((TB_POOL, d), lambda i: (i, 0)),
                   pl.BlockSpec((None, HALO, d), lambda i: (i // bps, 0, 0))],
        out_shape=[jax.ShapeDtypeStruct((n_all, d), f32), jax.ShapeDtypeStruct((batch, HALO, d), f32)],
        scratch_shapes=[pltpu.VMEM((TB_POOL + HALO, d), f32)],
        compiler_params=_cparams("arbitrary"), name="pool_p")(
            x_all, x_all, modp, modp, modp, *consts)
    pool_prompt = st[:, HALO - POOL_HIST:, :]

    past = jnp.transpose(state_pool[ip], (1, 0, 2))
    off = n_p // n_s

    def sspec(k):
        return pl.BlockSpec((None, None, n_s, d), lambda i, _k=k: (_k, 0, 0, 0))

    def body(x_ref, past_ref, sh, sc, g1, nw, w, b, ls, prev_ref, o_ref, h_ref):
        _pool_sample_body(n_t, dims['past_len'], x_ref, past_ref, sh, sc, g1, nw, w, b, ls, o_ref, h_ref)

    x_new, h_s = pl.pallas_call(
        body, grid=(1,),
        in_specs=[pl.BlockSpec((n_s, d), lambda i: (smp_off, 0)), _full_spec(past),
                  sspec(0), sspec(1), sspec(2)] + [_full_spec(c) for c in consts]
                 + [pl.BlockSpec(memory_space=pl.ANY)],
        out_specs=[pl.BlockSpec((n_s, d), lambda i: (off, 0)), pl.BlockSpec((n_s, d), lambda i: (0, 0))],
        out_shape=[jax.ShapeDtypeStruct((n_all, d), f32), jax.ShapeDtypeStruct((n_s, d), f32)],
        input_output_aliases={5 + len(consts): 0},
        compiler_params=_cparams("arbitrary"), name="pool_s")(
            x_smp, past, mods, mods, mods, *consts, x_new)
    ext = jnp.concatenate([past, h_s.reshape(n_t, nb, d)], axis=0)
    pool_sample = jnp.transpose(ext[-POOL_HIST:], (1, 0, 2))
    return x_new, pool_prompt, pool_sample


def _ssm_param_body(lr_ref, li_ref, ldt_ref, apr_ref, api_ref, kr_ref, ki_ref):
    lr, li = lr_ref[...], li_ref[...]
    dt = jnp.exp(ldt_ref[...])
    mag = jnp.exp(lr * dt)
    a_r, a_i = mag * jnp.cos(li * dt), mag * jnp.sin(li * dt)
    den = lr * lr + li * li
    nr, ni = a_r - 1.0, a_i
    kr_ref[...] = (nr * lr + ni * li) / den
    ki_ref[...] = (ni * lr - nr * li) / den
    pr, pi_ = a_r, a_i
    for k in range(8):
        apr_ref[k] = pr
        api_ref[k] = pi_
        pr, pi_ = pr * a_r - pi_ * a_i, pr * a_i + pi_ * a_r


def _ssm_tables(W, iq):
    g, p = W['ssm_lambda_re'][iq].shape
    c = SSM_GROUP_CH
    ns = g // SLAB_GROUPS
    sl = SLAB_GROUPS
    shp = [jax.ShapeDtypeStruct((8, g, p), f32)] * 2 + [jax.ShapeDtypeStruct((g, p), f32)] * 2
    apr, api, k_r, k_i = pl.pallas_call(_ssm_param_body, out_shape=shp, name="ssm_params")(
        W['ssm_lambda_re'][iq], W['ssm_lambda_im'][iq], W['ssm_log_dt'][iq].reshape(g, 1))
    b_re, b_im = W['ssm_b_re'][iq], W['ssm_b_im'][iq]
    bb_re = k_r[..., None] * b_re - k_i[..., None] * b_im
    bb_im = k_r[..., None] * b_im + k_i[..., None] * b_re
    eye = jnp.eye(sl, dtype=f32)

    def b_slab(bb):
        b4 = bb.reshape(ns, sl, p, c)
        return jnp.einsum('sgpc,gh->sgchp', b4, eye).reshape(ns, sl * c, sl * p)

    def c_slab(cc):
        c4 = cc.reshape(ns, sl, c, p)
        return jnp.einsum('sgcp,gh->shpgc', c4, eye).reshape(ns, sl * p, sl * c)

    bmat = jnp.concatenate([b_slab(bb_re), b_slab(bb_im)], axis=2).astype(bf16)
    cmat = jnp.concatenate([c_slab(W['ssm_c_re'][iq]), -c_slab(W['ssm_c_im'][iq])], axis=1).astype(bf16)

    def lay(a):
        return jnp.transpose(a.reshape(8, ns, sl * p), (1, 0, 2))

    apw = jnp.stack([lay(apr), lay(api)], axis=1)
    return bmat, cmat, apw


def _ssm_out(x, h, y, g1_ref, dsk_ref, wglu_ref, bglu_ref):
    d = x.shape[1]
    z = _gelu_tanh(y + dsk_ref[...] * h)
    gu = _mm(z.astype(bf16), wglu_ref[...]) + bglu_ref[...]
    return x + g1_ref[...] * (gu[:, :d] * _sigmoid(gu[:, d:]))


def _ssm_prompt_body(x_ref, sh_ref, sc_ref, g1_ref, nw_ref, b_ref, c_ref, t2_ref, dsk_ref,
                     wglu_ref, bglu_ref, o_ref, st_ref, xs_ref, hs_ref, carry_ref, y_ref):
    t = x_ref.shape[0]
    ns = b_ref.shape[0]
    sw = b_ref.shape[1]
    half = b_ref.shape[2] // 2
    hc = half // LANES

    @pl.when(pl.program_id(1) == 0)
    def _():
        carry_ref[...] = jnp.zeros_like(carry_ref)

    x = x_ref[...]
    h = _norm_mod(x, nw_ref[...], sc_ref[...], sh_ref[...])
    hb = h.astype(bf16)
    def expand(s):
        xx = _mm(hb[:, s * sw:(s + 1) * sw], b_ref[s])
        for c in range(2 * hc):
            xs_ref[s % 3, c] = xx[:, c * LANES:(c + 1) * LANES]

    def group_scan(s):
        b = s % 3
        for c in range(hc):
            ar = jnp.broadcast_to(t2_ref[s, 0, 0:1, c * LANES:(c + 1) * LANES], (8, LANES))
            ai = jnp.broadcast_to(t2_ref[s, 1, 0:1, c * LANES:(c + 1) * LANES], (8, LANES))
            for base in range(0, t, 64):
                hr = xs_ref[b, c, pl.ds(base, 8, stride=8), :]
                hi = xs_ref[b, hc + c, pl.ds(base, 8, stride=8), :]
                for i in range(1, 8):
                    rows = pl.ds(base + i, 8, stride=8)
                    hr, hi = (ar * hr - ai * hi + xs_ref[b, c, rows, :],
                              ar * hi + ai * hr + xs_ref[b, hc + c, rows, :])
                    xs_ref[b, c, rows, :] = hr
                    xs_ref[b, hc + c, rows, :] = hi

    def carry_scan(s):
        b = s % 3

        def grp(j, car):
            cr, ci = car
            r0 = pl.multiple_of(j * 8, 8)
            xr = jnp.concatenate([xs_ref[b, c, pl.ds(r0, 8), :] for c in range(hc)], axis=1)
            xi = jnp.concatenate([xs_ref[b, hc + c, pl.ds(r0, 8), :] for c in range(hc)], axis=1)
            pr = t2_ref[s, 0]
            pi_ = t2_ref[s, 1]
            hr = xr + pr * cr - pi_ * ci
            hi = xi + pr * ci + pi_ * cr
            hs_ref[s % 2, pl.ds(r0, 8), 0:half] = hr
            hs_ref[s % 2, pl.ds(r0, 8), half:] = hi
            return (jnp.broadcast_to(hr[7:8, :], hr.shape), jnp.broadcast_to(hi[7:8, :], hi.shape))

        cr, ci = lax.fori_loop(0, t // 8, grp, (carry_ref[s, :, 0:half], carry_ref[s, :, half:]))
        carry_ref[s, :, 0:half] = cr
        carry_ref[s, :, half:] = ci
        st_ref[s:s + 1, :] = jnp.concatenate([cr[0:1, :], ci[0:1, :]], axis=1)

    expand(0)
    if ns > 1:
        expand(1)
    group_scan(0)
    for s in range(ns):
        carry_scan(s)
        y_ref[:, s * sw:(s + 1) * sw] = _mm(hs_ref[s % 2].astype(bf16), c_ref[s])
        if s + 2 < ns:
            expand(s + 2)
        if s + 1 < ns:
            group_scan(s + 1)
    y = y_ref[...]
    o_ref[...] = _ssm_out(x, h, y, g1_ref, dsk_ref, wglu_ref, bglu_ref)


def _ssm_sample_body(n_t, x_ref, s0_ref, sh_ref, sc_ref, g1_ref, nw_ref, b_ref, c_ref, t2_ref, dsk_ref,
                     wglu_ref, bglu_ref, prev_ref, o_ref, so_ref, y_ref):
    ns = b_ref.shape[0]
    sw = b_ref.shape[1]
    half = b_ref.shape[2] // 2
    nb = s0_ref.shape[1]
    x = x_ref[...]
    h = _norm_mod(x, nw_ref[...], sc_ref[...], sh_ref[...])
    hb = h.astype(bf16)
    for s in range(ns):
        s_r = s0_ref[s, :, 0:half]
        s_i = s0_ref[s, :, half:]
        ar = t2_ref[s, 0, 0:1, :]
        ai = t2_ref[s, 1, 0:1, :]
        for t in range(n_t):
            xx = _mm(hb[t * nb:(t + 1) * nb, s * sw:(s + 1) * sw], b_ref[s])
            s_r, s_i = ar * s_r - ai * s_i + xx[:, 0:half], ar * s_i + ai * s_r + xx[:, half:]
            st = jnp.concatenate([s_r, s_i], axis=1)
            y_ref[t * nb:(t + 1) * nb, s * sw:(s + 1) * sw] = _mm(st.astype(bf16), c_ref[s])
        so_ref[s] = jnp.concatenate([s_r, s_i], axis=1)
    o_ref[...] = _ssm_out(x, h, y_ref[...], g1_ref, dsk_ref, wglu_ref, bglu_ref)


def _ssm_layer(x_all, dims, l, iq, W, st_re, st_im):
    n_all, d = x_all.shape
    n_p, n_s, seq, nb, n_t, batch = (dims[k] for k in ('n_p', 'n_s', 'seq', 'nb', 'n_t', 'batch'))
    modp, mods = dims['modp'], dims['mods']
    bmat, cmat, t2 = _ssm_tables(W, iq)
    ns = bmat.shape[0]
    sp = bmat.shape[2]
    g, p = W['ssm_lambda_re'][iq].shape
    sl = SLAB_GROUPS
    tail = [W['ssm_d'][iq].reshape(1, d), W['w_glu'][iq].astype(bf16), W['b_glu'][iq].reshape(1, 2 * d)]
    nw = W['norm_mix'][l].reshape(1, d)
    nchunk = seq // T_SSM

    def mspec(k):
        return pl.BlockSpec((None, None, 1, d), lambda b, c, _k=k: (_k, b, 0, 0))

    def cspec(a):
        return pl.BlockSpec(a.shape, lambda b, c, _n=a.ndim: (0,) * _n)

    consts = [nw, bmat, cmat, t2] + tail
    x_new, st = pl.pallas_call(
        _ssm_prompt_body, grid=(batch, nchunk),
        in_specs=[pl.BlockSpec((T_SSM, d), lambda b, c: (b * nchunk + c, 0)), mspec(0), mspec(1), mspec(2)]
                 + [cspec(a) for a in consts],
        out_specs=[pl.BlockSpec((T_SSM, d), lambda b, c: (b * nchunk + c, 0)),
                   pl.BlockSpec((None, ns, sp), lambda b, c: (b, 0, 0))],
        out_shape=[jax.ShapeDtypeStruct((n_all, d), f32), jax.ShapeDtypeStruct((batch, ns, sp), f32)],
        scratch_shapes=[pltpu.VMEM((3, sp // LANES, T_SSM, LANES), f32), pltpu.VMEM((2, T_SSM, sp), f32),
                        pltpu.VMEM((ns, 8, sp), f32), pltpu.VMEM((T_SSM, d), f32)],
        compiler_params=_cparams("arbitrary", "arbitrary"), name="ssm_p")(
            x_all, modp, modp, modp, *consts)

    def unslab(a):
        a5 = a.reshape(a.shape[0], ns, 2, sl, p)
        return a5[:, :, 0].reshape(-1, g, p), a5[:, :, 1].reshape(-1, g, p)

    re_p, im_p = unslab(st)

    def slab(a):
        return a.reshape(a.shape[0], ns, sl * p)

    s0 = jnp.transpose(jnp.concatenate([slab(st_re), slab(st_im)], axis=2), (1, 0, 2))
    off = n_p // n_s

    def sspec(k):
        return pl.BlockSpec((None, None, n_s, d), lambda i, _k=k: (_k, 0, 0, 0))

    consts_s = [nw, bmat, cmat, t2] + tail
    x_new, so = pl.pallas_call(
        functools.partial(_ssm_sample_body, n_t), grid=(1,),
        in_specs=[pl.BlockSpec((n_s, d), lambda i: (off, 0)), _full_spec(s0), sspec(0), sspec(1), sspec(2)]
                 + [_full_spec(a) for a in consts_s] + [pl.BlockSpec(memory_space=pl.ANY)],
        out_specs=[pl.BlockSpec((n_s, d), lambda i: (off, 0)), _full_spec(s0)],
        out_shape=[jax.ShapeDtypeStruct((n_all, d), f32), jax.ShapeDtypeStruct(s0.shape, f32)],
        scratch_shapes=[pltpu.VMEM((n_s, d), f32)],
        input_output_aliases={5 + len(consts_s): 0},
        compiler_params=_cparams("arbitrary"), name="ssm_s")(
            x_all, s0, mods, mods, mods, *consts_s, x_new)
    re_s, im_s = unslab(jnp.transpose(so, (1, 0, 2)))
    return x_new, re_p, im_p, re_s, im_s


def _qkv_body(prompt, x_ref, sh_ref, sc_ref, cos_ref, sin_ref, nw_ref, w_ref, qn_ref, kn_ref, seg_ref, segt_ref,
              kb_ref, k_ref, v_ref, a_ref, b_ref):
    d = x_ref.shape[1]
    h = _norm_mod(x_ref[...], nw_ref[...], sc_ref[...], sh_ref[...])
    qkv = _mm(h.astype(bf16), w_ref[...])
    rep = d // LANES
    cos = jnp.concatenate([cos_ref[...]] * rep, axis=1)
    sin = jnp.concatenate([sin_ref[...]] * rep, axis=1)
    lane_lo = (lax.broadcasted_iota(i32, (1, d), 1) % HEAD_DIM) < (HEAD_DIM // 2)
    seg, segt = seg_ref[...], segt_ref[...]

    def norm_rope(t, g):
        s_hi, s_lo = _split2(t * t)
        ms = (_mm(s_hi, seg) + _mm(s_lo, seg)) * (1.0 / HEAD_DIM)
        r_hi, r_lo = _split2(lax.rsqrt(ms + NORM_EPS))
        rf = _mm(r_hi, segt) + _mm(r_lo, segt)
        tn = t * rf * g
        sw = jnp.where(lane_lo, pltpu.roll(tn, d - HEAD_DIM // 2, 1), pltpu.roll(tn, HEAD_DIM // 2, 1))
        return tn * cos + sw * sin

    q = norm_rope(qkv[:, :d], qn_ref[...])
    k = norm_rope(qkv[:, d:2 * d], kn_ref[...])
    v = qkv[:, 2 * d:]
    kb_ref[...] = k.astype(bf16)
    if prompt:
        tb = k.shape[0]
        for hh in range(N_HEADS):
            cols = slice(hh * 2 * HEAD_DIM, (hh + 1) * 2 * HEAD_DIM)
            k_ref[pl.ds(hh, tb, stride=N_HEADS), :] = k[:, cols]
            v_ref[pl.ds(hh, tb, stride=N_HEADS), :] = v[:, cols]
    else:
        k_ref[...] = k
        v_ref[...] = v
    if prompt:
        a_ref[...] = (q * (HEAD_DIM ** -0.5 * LOG2E)).T.astype(bf16)
        tk = b_ref.shape[2]
        for c in range(b_ref.shape[0]):
            b_ref[c] = v[c * tk:(c + 1) * tk, :].T.astype(bf16)
    else:
        a_ref[...] = (q * (HEAD_DIM ** -0.5)).astype(bf16)
        b_ref[...] = v.astype(bf16)


def _lam(lam_ref):
    l4 = lam_ref[...]
    a = jnp.sum(l4[0:1, :] * l4[1:2, :], axis=1, keepdims=True)
    b = jnp.sum(l4[2:3, :] * l4[3:4, :], axis=1, keepdims=True)
    return jnp.exp(a) - jnp.exp(b)


def _attn_finish(acc, l, lam, lam_init, sub_ref):
    r = acc.shape[0] // 2
    o = acc[:r] / l[:r] - lam * (acc[r:] / l[r:])
    o = o * lax.rsqrt(jnp.mean(o * o, axis=-1, keepdims=True) + NORM_EPS) * sub_ref[...]
    return o * (1.0 - lam_init)


def _flash_body(lam_init, qt_ref, k_ref, vt_ref, lam_ref, sub_ref, o_ref, acc_ref, s_ref, p_ref):
    tq = qt_ref.shape[1]
    tk = vt_ref.shape[2]
    ratio = tq // tk
    hd2 = 2 * HEAD_DIM
    nh = qt_ref.shape[0] // hd2
    qi = pl.program_id(2)
    top = lax.broadcasted_iota(i32, (hd2, 1), 0) < HEAD_DIM
    qqs = []
    for h in range(nh):
        qt = qt_ref[h * hd2:(h + 1) * hd2, :]
        zero = jnp.zeros_like(qt)
        qqs.append(jnp.concatenate([jnp.where(top, qt, zero), jnp.where(top, zero, qt)], axis=1))
    acc_ref[...] = jnp.zeros_like(acc_ref)
    p_ref[...] = jnp.zeros_like(p_ref)

    def scores(j):
        r0 = pl.multiple_of(j * tk, tk)
        return [_mm(k_ref[pl.ds(r0, tk), h * hd2:(h + 1) * hd2], qqs[h]) for h in range(nh)]

    def keep(ss, slot):
        for h in range(nh):
            s_ref[slot, h] = ss[h]

    def values(j):
        return [_mm(vt_ref[j, h * hd2:(h + 1) * hd2, :], p_ref[h]) for h in range(nh)]

    def softmax(slot, carry, key0):
        out = []
        for h in range(nh):
            m, l, _ = carry[h]
            s = s_ref[slot, h]
            if key0 is not None:
                key = lax.broadcasted_iota(i32, s.shape, 0) + key0
                qry = lax.broadcasted_iota(i32, s.shape, 1) % tq
                s = jnp.where(key <= qry, s, NEG_INF)
            m_new = jnp.maximum(m, jnp.max(s, axis=0, keepdims=True))
            alpha = jnp.exp2(m - m_new)
            p = jnp.exp2(s - m_new)
            p_ref[h] = p.astype(bf16)
            out.append((m_new, alpha * l + jnp.sum(p, axis=0, keepdims=True), alpha))
        return tuple(out)

    def accumulate(pvs, carry):
        for h in range(nh):
            acc_ref[h] = carry[h][2] * acc_ref[h] + pvs[h]

    def step(j, slot, carry):
        pvs = values(jnp.maximum(j - 1, 0))
        ss = scores(j + 1)
        new = softmax(slot, carry, None)
        keep(ss, 1 - slot)
        accumulate(pvs, carry)
        return new

    def finish(j, slot, carry):
        for u in range(ratio):
            pvs = values(jnp.maximum(j + u - 1, 0))
            if u + 1 < ratio:
                ss = scores(j + u + 1)
            new = softmax(slot, carry, u * tk)
            if u + 1 < ratio:
                keep(ss, 1 - slot)
            accumulate(pvs, carry)
            carry, slot = new, 1 - slot
        accumulate(values(j + ratio - 1), carry)
        return tuple(c[1] for c in carry)

    keep(scores(0), 0)
    init = tuple((jnp.full((1, 2 * tq), NEG_INF, f32), jnp.zeros((1, 2 * tq), f32),
                  jnp.ones((1, 2 * tq), f32)) for _ in range(nh))
    n_full = ratio * qi
    carry = lax.fori_loop(0, n_full // 2, lambda i, c: step(2 * i + 1, 1, step(2 * i, 0, c)), init)
    if ratio % 2 == 0:
        ls = finish(n_full, 0, carry)
    else:
        ls = lax.cond(n_full % 2 == 0, lambda c: finish(n_full, 0, c),
                      lambda c: finish(n_full, 1, step(n_full - 1, 0, c)), carry)
    lam = _lam(lam_ref) + lam_init
    for h in range(nh):
        acc = acc_ref[h]
        l = ls[h]
        o = acc[:, :tq] / l[:, :tq] - lam * (acc[:, tq:] / l[:, tq:])
        o = o * lax.rsqrt(jnp.mean(o * o, axis=0, keepdims=True) + NORM_EPS) * sub_ref[...]
        o_ref[:, h * hd2:(h + 1) * hd2] = (o * (1.0 - lam_init)).T.astype(o_ref.dtype)


def _attn_sample_body(lam_init, pp, pt_ref, q_ref, kn_ref, vn_ref, mp_ref, mn_ref, lam_ref, sub_ref, *rest):
    k_refs, v_refs = rest[:pp], rest[pp:2 * pp]
    o_ref = rest[2 * pp]
    m_ref, l_ref, acc_ref = rest[2 * pp + 1:]
    j = pl.program_id(1)
    q = q_ref[...]

    @pl.when(j == 0)
    def _():
        m_ref[...] = jnp.full_like(m_ref, NEG_INF)
        l_ref[...] = jnp.zeros_like(l_ref)
        acc_ref[...] = jnp.zeros_like(acc_ref)

    def lane_fold(op, a):
        parts = [a[:, c:c + LANES] for c in range(0, a.shape[1], LANES)]
        while len(parts) > 1:
            parts = [op(parts[i], parts[i + 1]) for i in range(0, len(parts) - 1, 2)] + parts[len(parts) & ~1:]
        return parts[0]

    def update(kfs, vfs, mask):
        ss = [_nt(q, kf) + mask for kf in kfs]
        mx = ss[0] if ss[0].shape[1] < LANES else functools.reduce(jnp.maximum, [lane_fold(jnp.maximum, s) for s in ss])
        m = m_ref[...]
        m_new = jnp.maximum(m, jnp.max(mx, axis=1, keepdims=True))
        alpha = jnp.exp(m - m_new)
        ps = [jnp.exp(s - m_new) for s in ss]
        sm = ps[0] if ps[0].shape[1] < LANES else functools.reduce(jnp.add, [lane_fold(jnp.add, p) for p in ps])
        l_ref[...] = alpha * l_ref[...] + jnp.sum(sm, axis=1, keepdims=True)
        pv = functools.reduce(jnp.add, [_mm(p.astype(bf16), vf) for p, vf in zip(ps, vfs)])
        acc_ref[...] = alpha * acc_ref[...] + pv
        m_ref[...] = m_new

    update([r[...].astype(bf16) for r in k_refs], [r[...].astype(bf16) for r in v_refs], mp_ref[...])

    @pl.when(j == pl.num_programs(1) - 1)
    def _():
        update([kn_ref[...]], [vn_ref[...]], mn_ref[...])
        lam = _lam(lam_ref) + lam_init
        o_ref[...] = _attn_finish(acc_ref[...], l_ref[...], lam, lam_init, sub_ref)


def _oproj_body(x_ref, o_ref_in, g1_ref, w_ref, o_ref):
    o_ref[...] = x_ref[...] + g1_ref[...] * _mm(o_ref_in[...], w_ref[...])


def _attn_layer(x_all, dims, l, ia, W, cache_k, cache_v, page_table):
    n_all, d = x_all.shape
    n_p, n_s, seq, nb, n_t, batch = (dims[k] for k in ('n_p', 'n_s', 'seq', 'nb', 'n_t', 'batch'))
    past_len = dims['past_len']
    hd2 = 2 * HEAD_DIM
    lam_init = 0.8 - 0.6 * math.exp(-0.3 * l)

    half = HEAD_DIM // 2
    inv = jnp.power(ROPE_THETA, -jnp.arange(half, dtype=f32) * (2.0 / HEAD_DIM))

    def tables(pos):
        ang = pos.astype(f32)[:, None] * inv[None, :]
        cos, sin = jnp.cos(ang), jnp.sin(ang)
        return (jnp.concatenate([cos] * 4, axis=1), jnp.concatenate([-sin, sin, -sin, sin], axis=1))

    cos_p, sin_p = tables(jnp.arange(seq))
    cos_s, sin_s = tables(past_len + jnp.repeat(jnp.arange(n_t), nb))
    seg = (jnp.arange(d)[:, None] // HEAD_DIM == jnp.arange(d // HEAD_DIM)[None, :]).astype(bf16)
    consts = [W['norm_mix'][l].reshape(1, d), W['w_qkv'][ia].astype(bf16),
              jnp.tile(W['q_norm'][ia], d // HEAD_DIM).reshape(1, d),
              jnp.tile(W['k_norm'][ia], d // HEAD_DIM).reshape(1, d), seg, seg.T]
    modp, mods = dims['modp'], dims['mods']
    tk = TK_ATTN
    kb, k_pr, v_pr, qt, vt = _rows_call(
        functools.partial(_qkv_body, True), tb=TB_PROMPT, nblk=n_p // TB_PROMPT, off=0, bps=seq // TB_PROMPT,
        row_ins=[x_all], mod_ins=[(modp, 0), (modp, 1)], seq_ins=[cos_p, sin_p], consts=consts,
        outs=[((n_all, d), bf16, 'row'), ((n_p * N_HEADS, hd2), f32, 'own'), ((n_p * N_HEADS, hd2), f32, 'own'),
              ((d, n_p), bf16, 'col'), ((n_p // tk, d, tk), bf16, 'blk3')], name="qkv_p")
    kb, k_sm, v_sm, qs, vb = _rows_call(
        functools.partial(_qkv_body, False), tb=n_s, nblk=1, off=n_p // n_s, bps=1,
        row_ins=[x_all], mod_ins=[(mods, 0), (mods, 1)], seq_ins=[cos_s, sin_s], consts=consts,
        outs=[((n_all, d), bf16, 'row'), ((n_s, d), f32, 'own'), ((n_s, d), f32, 'own'),
              ((n_s, d), bf16, 'own'), ((n_s, d), bf16, 'own')], prev=[kb], name="qkv_s")

    lam4 = jnp.stack([W['lambda_q1'][ia], W['lambda_k1'][ia], W['lambda_q2'][ia], W['lambda_k2'][ia]])
    sub = W['subln'][ia].reshape(1, hd2)
    sub_c = W['subln'][ia].reshape(hd2, 1)
    nq = seq // TQ_ATTN
    hw = HEADS_PER_STEP * hd2
    o_all = pl.pallas_call(
        functools.partial(_flash_body, lam_init), grid=(batch, N_HEADS // HEADS_PER_STEP, nq),
        in_specs=[pl.BlockSpec((hw, TQ_ATTN), lambda b, h, i: (h, b * nq + i)),
                  pl.BlockSpec((seq, hw), lambda b, h, i: (b, h)),
                  pl.BlockSpec((seq // tk, hw, tk), lambda b, h, i: (b, h, 0)),
                  pl.BlockSpec(lam4.shape, lambda b, h, i: (0, 0)),
                  pl.BlockSpec(sub_c.shape, lambda b, h, i: (0, 0))],
        out_specs=pl.BlockSpec((TQ_ATTN, hw), lambda b, h, i: (b * nq + i, h)),
        out_shape=jax.ShapeDtypeStruct((n_all, d), bf16),
        scratch_shapes=[pltpu.VMEM((HEADS_PER_STEP, hd2, 2 * TQ_ATTN), f32),
                        pltpu.VMEM((2, HEADS_PER_STEP, tk, 2 * TQ_ATTN), f32),
                        pltpu.VMEM((HEADS_PER_STEP, tk, 2 * TQ_ATTN), bf16)],
        compiler_params=_cparams("arbitrary", "arbitrary", "arbitrary"), name="flash_p")(
            qt, kb, vt, lam4, sub_c)

    na, n_phys, page, _, _ = cache_k.shape
    flat = page * N_HEADS
    ck = cache_k.reshape(na * n_phys, flat, hd2)
    cv = cache_v.reshape(na * n_phys, flat, hd2)
    n_pages = page_table.shape[1]
    pp = PAGES_PER_STEP if n_pages % PAGES_PER_STEP == 0 else 1
    rows = 2 * N_HEADS * n_t
    q5 = jnp.transpose(qs.reshape(n_t, nb, N_HEADS, 2, HEAD_DIM), (1, 3, 2, 0, 4))
    z = jnp.zeros_like(q5[:, 0])
    qm = jnp.stack([jnp.concatenate([q5[:, 0], z], -1), jnp.concatenate([z, q5[:, 1]], -1)], axis=1)
    qm = qm.reshape(nb, rows, hd2)

    def new_rows(a):
        return jnp.transpose(a.reshape(n_t, nb, N_HEADS, hd2), (1, 0, 2, 3)).reshape(nb, n_t * N_HEADS, hd2)

    kn, vn = new_rows(kb[n_p:]), new_rows(vb)
    r_h = (jnp.arange(rows) % (N_HEADS * n_t)) // n_t
    r_t = jnp.arange(rows) % n_t
    mask_p = jnp.where(r_h[:, None] == (jnp.arange(flat) % N_HEADS)[None, :], 0.0, NEG_INF).astype(f32)
    cn = jnp.arange(n_t * N_HEADS)
    mask_n = jnp.where((r_h[:, None] == (cn % N_HEADS)[None, :]) & ((cn // N_HEADS)[None, :] <= r_t[:, None]),
                       0.0, NEG_INF).astype(f32)
    base = ia * n_phys

    def seq_spec(shape):
        return pl.BlockSpec((None,) + shape, lambda s, j, pt: (s, 0, 0))

    def cst_spec(a):
        return pl.BlockSpec(a.shape, lambda s, j, pt: (0, 0))

    def page_spec(u):
        return pl.BlockSpec((None, flat, hd2), lambda s, j, pt, _u=u: (base + pt[s, j * pp + _u], 0, 0))

    gs = pltpu.PrefetchScalarGridSpec(
        num_scalar_prefetch=1, grid=(nb, n_pages // pp),
        in_specs=[seq_spec((rows, hd2)), seq_spec((n_t * N_HEADS, hd2)), seq_spec((n_t * N_HEADS, hd2)),
                  cst_spec(mask_p), cst_spec(mask_n), cst_spec(lam4), cst_spec(sub)]
                 + [page_spec(u) for u in range(pp)] + [page_spec(u) for u in range(pp)],
        out_specs=pl.BlockSpec((None, rows // 2, hd2), lambda s, j, pt: (s, 0, 0)),
        scratch_shapes=[pltpu.VMEM((rows, 1), f32), pltpu.VMEM((rows, 1), f32), pltpu.VMEM((rows, hd2), f32)])
    o_s = pl.pallas_call(
        functools.partial(_attn_sample_body, lam_init, pp), grid_spec=gs,
        out_shape=jax.ShapeDtypeStruct((nb, rows // 2, hd2), f32),
        compiler_params=_cparams("arbitrary", "arbitrary"), name="attn_s")(
            page_table, qm, kn, vn, mask_p, mask_n, lam4, sub, *([ck] * pp), *([cv] * pp))
    o_s = jnp.transpose(o_s.reshape(nb, N_HEADS, n_t, hd2), (2, 0, 1, 3)).reshape(n_s, d)
    o_all = lax.dynamic_update_slice(o_all, o_s.astype(bf16), (n_p, 0))

    (x_new,) = _both_groups(
        _oproj_body, dims, row_ins=[x_all, o_all], mod_ks=(2,), consts=[W['w_o'][ia].astype(bf16)],
        outs=[((n_all, d), f32, 'row')], name="oproj")

    def smp(a):
        return jnp.transpose(a.reshape(n_t, nb, N_HEADS, hd2), (1, 0, 2, 3))

    shp = (batch, seq, N_HEADS, hd2)
    return x_new, k_pr.reshape(shp), v_pr.reshape(shp), smp(k_sm), smp(v_sm)


def kernel(x_prompt, x_sample, cache_k, cache_v, state_pool, state_ssm_re, state_ssm_im, page_table, c_prompt, c_sample, w_ada, b_ada, norm_mix, norm_ff, w_pool, b_pool, ls_pool, ssm_lambda_re, ssm_lambda_im, ssm_log_dt, ssm_b_re, ssm_b_im, ssm_c_re, ssm_c_im, ssm_d, w_glu, b_glu, w_qkv, q_norm, k_norm, lambda_q1, lambda_k1, lambda_q2, lambda_k2, subln, w_o, w_router, b_router, w_gate_up, w_down):
    batch, seq, d = x_prompt.shape
    nb, n_t, _ = x_sample.shape
    depth = w_ada.shape[0]
    n_p, n_s = batch * seq, nb * n_t
    past_len = page_table.shape[1] * cache_k.shape[2]
    W = dict(norm_mix=norm_mix, norm_ff=norm_ff, w_pool=w_pool, b_pool=b_pool, ls_pool=ls_pool,
             ssm_lambda_re=ssm_lambda_re, ssm_lambda_im=ssm_lambda_im, ssm_log_dt=ssm_log_dt,
             ssm_b_re=ssm_b_re, ssm_b_im=ssm_b_im, ssm_c_re=ssm_c_re, ssm_c_im=ssm_c_im, ssm_d=ssm_d,
             w_glu=w_glu, b_glu=b_glu, w_qkv=w_qkv, q_norm=q_norm, k_norm=k_norm, lambda_q1=lambda_q1,
             lambda_k1=lambda_k1, lambda_q2=lambda_q2, lambda_k2=lambda_k2, subln=subln, w_o=w_o,
             w_router_t=w_router.T, b_router_c=b_router.reshape(-1, 1),
             w_gate_up=w_gate_up, w_down=w_down)

    ada = _ada_all(jnp.concatenate([c_prompt, c_sample], axis=0), w_ada, b_ada)
    x_all = x_prompt.reshape(n_p, d)
    x_smp, smp_off = jnp.transpose(x_sample, (1, 0, 2)).reshape(n_s, d), 0
    dims = dict(n_p=n_p, n_s=n_s, seq=seq, nb=nb, n_t=n_t, batch=batch, past_len=past_len)

    pools_p, pools_s, k_ps, v_ps, k_ss, v_ss = [], [], [], [], [], []
    re_ps, im_ps, re_ss, im_ss = [], [], [], []
    ip = iq = ia = 0
    for l in range(depth):
        chunks = jnp.transpose(ada[l].reshape(batch + nb, 6, d), (1, 0, 2))
        dims['modp'] = chunks[:, :batch, None, :]
        dims['mods'] = jnp.tile(chunks[:, batch:], (1, n_t, 1))[:, None]
        kind = l % N_MIXERS
        if kind == 0:
            x_all, pp_, ps_ = _pool_layer(x_all, x_smp, smp_off, dims, l, ip, W, state_pool)
            pools_p.append(pp_)
            pools_s.append(ps_)
            ip += 1
        elif kind == 1:
            x_all, rp, imp, rs, ims = _ssm_layer(x_all, dims, l, iq, W, state_ssm_re[iq], state_ssm_im[iq])
            re_ps.append(rp)
            im_ps.append(imp)
            re_ss.append(rs)
            im_ss.append(ims)
            iq += 1
        else:
            x_all, kp, vp, ks, vs = _attn_layer(x_all, dims, l, ia, W, cache_k, cache_v, page_table)
            k_ps.append(kp)
            v_ps.append(vp)
            k_ss.append(ks)
            v_ss.append(vs)
            ia += 1
        x_all = _moe_layer(x_all, dims, l, W, l == depth - 1)
        x_smp, smp_off = x_all, n_p // n_s

    y_prompt = x_all[0].reshape(batch, seq, d)
    y_sample = jnp.transpose(x_all[1].reshape(n_t, nb, d), (1, 0, 2))
    return (y_prompt, y_sample, jnp.stack(k_ps), jnp.stack(v_ps), jnp.stack(k_ss), jnp.stack(v_ss),
            jnp.stack(pools_p), jnp.stack(pools_s), jnp.stack(re_ps), jnp.stack(im_ps),
            jnp.stack(re_ss), jnp.stack(im_ss))
```

```python
import functools
import math

import jax
import jax.numpy as jnp
from jax import lax
from jax.experimental import pallas as pl
from jax.experimental.pallas import tpu as pltpu

f32 = jnp.float32
bf16 = jnp.bfloat16
i32 = jnp.int32

N_MIXERS = 3
POOL_WINDOWS = (2, 4, 8, 16)
POOL_HIST = 15
HALO = 16
SSM_GROUP_CH = 16
SSM_STATE = 64
SLAB_GROUPS = 8
N_HEADS = 8
HEAD_DIM = 64
ROPE_THETA = 10000.0
N_EXPERTS = 16
EXPERTS_PER_GROUP = 4
NORM_EPS = 1e-6
NEG_INF = -1e30
LOG2E = 1.4426950408889634
LANES = 128
VMEM_LIMIT = 48 * 1024 * 1024

TB_PROMPT = 1024
TB_QKV = 512
TB_POOL = 512
T_SSM = 512
TQ_ATTN = 256
TK_ATTN = 256
HEADS_PER_STEP = 2
TM_MOE = 512
H2_ROWS_FACTOR = 3
XS_PARTS = 3
INVPERM_UNROLL = 8
PAGES_PER_STEP = 16


def _cparams(*sem):
    return pltpu.CompilerParams(dimension_semantics=sem, vmem_limit_bytes=VMEM_LIMIT)


def _mm(a, b):
    return jnp.dot(a, b, preferred_element_type=f32)


def _nt(a, b):
    return lax.dot_general(a, b, (((1,), (1,)), ((), ())), preferred_element_type=f32)


def _split2(a):
    hi = a.astype(bf16)
    lo = (a - hi.astype(f32)).astype(bf16)
    return hi, lo


def _norm_mod(x, g, sc, sh):
    ms = jnp.mean(x * x, axis=-1, keepdims=True)
    return x * lax.rsqrt(ms + NORM_EPS) * g * (1.0 + sc) + sh


def _gelu_tanh(y):
    return 0.5 * y * (1.0 + jnp.tanh(0.7978845608028654 * (y + 0.044715 * y * y * y)))


def _sigmoid(x):
    return 1.0 / (1.0 + jnp.exp(-x))


def _full_spec(a):
    nd = a.ndim
    return pl.BlockSpec(a.shape, lambda i, _n=nd: (0,) * _n)


def _rows_call(body, *, tb, nblk, off, bps, row_ins, mod_ins, seq_ins, consts, outs, prev=None,
               scratch=(), name):
    in_specs, args = [], []
    for a in row_ins:
        a, o = a if isinstance(a, tuple) else (a, off)
        in_specs.append(pl.BlockSpec((tb, a.shape[1]), lambda i, _o=o: (i + _o, 0)))
        args.append(a)
    for a, k in mod_ins:
        r, d = a.shape[2], a.shape[3]
        in_specs.append(pl.BlockSpec((None, None, r, d), lambda i, _k=k: (_k, i // bps, 0, 0)))
        args.append(a)
    for a in seq_ins:
        in_specs.append(pl.BlockSpec((tb, a.shape[1]), lambda i: (i % bps, 0)))
        args.append(a)
    for a in consts:
        in_specs.append(_full_spec(a))
        args.append(a)
    n_real = len(args)
    aliases = {}
    if prev is not None:
        for j, a in enumerate(prev):
            in_specs.append(pl.BlockSpec(memory_space=pl.ANY))
            args.append(a)
            aliases[n_real + j] = j
    out_specs, out_shapes = [], []
    for shape, dtype, kind in outs:
        if kind == 'row':
            out_specs.append(pl.BlockSpec((tb, shape[1]), lambda i: (i + off, 0)))
        elif kind == 'own':
            out_specs.append(pl.BlockSpec((shape[0] // nblk, shape[1]), lambda i: (i, 0)))
        elif kind == 'col':
            out_specs.append(pl.BlockSpec((shape[0], tb), lambda i: (0, i + off)))
        elif kind == 'blk3':
            out_specs.append(pl.BlockSpec((tb // shape[2], shape[1], shape[2]), lambda i: (i, 0, 0)))
        else:
            out_specs.append(pl.BlockSpec(shape, lambda i, _n=len(shape): (0,) * _n))
        out_shapes.append(jax.ShapeDtypeStruct(shape, dtype))
    n_prev = 0 if prev is None else len(prev)

    def wrapped(*refs):
        body(*refs[:n_real], *refs[n_real + n_prev:])

    res = pl.pallas_call(
        wrapped, grid=(nblk,), in_specs=in_specs, out_specs=out_specs, out_shape=out_shapes,
        scratch_shapes=list(scratch), input_output_aliases=aliases,
        compiler_params=_cparams("arbitrary"), name=name)(*args)
    return list(res)


def _both_groups(body, dims, *, row_ins, mod_ks, consts, outs, name, seq_ins_p=(), seq_ins_s=()):
    modp, mods = dims['modp'], dims['mods']
    n_p, n_s, seq = dims['n_p'], dims['n_s'], dims['seq']
    res = _rows_call(body, tb=TB_PROMPT, nblk=n_p // TB_PROMPT, off=0, bps=seq // TB_PROMPT,
                     row_ins=row_ins, mod_ins=[(modp, k) for k in mod_ks], seq_ins=list(seq_ins_p),
                     consts=consts, outs=outs, name=name + "_p")
    res = _rows_call(body, tb=n_s, nblk=1, off=n_p // n_s, bps=1,
                     row_ins=row_ins, mod_ins=[(mods, k) for k in mod_ks], seq_ins=list(seq_ins_s),
                     consts=consts, outs=outs, prev=res, name=name + "_s")
    return res


def _ada_body(c_ref, w_ref, b_ref, o_ref):
    c = c_ref[...]
    cond = c * _sigmoid(c)
    c_hi, c_lo = _split2(cond)
    w_hi, w_lo = _split2(w_ref[...])
    o_ref[...] = _mm(c_hi, w_hi) + _mm(c_hi, w_lo) + _mm(c_lo, w_hi) + b_ref[...]


def _ada_all(c_all, w_ada, b_ada):
    depth, d, d6 = w_ada.shape
    n = c_all.shape[0]
    nj = d6 // d
    return pl.pallas_call(
        _ada_body, grid=(depth, nj),
        in_specs=[pl.BlockSpec((n, d), lambda l, j: (0, 0)),
                  pl.BlockSpec((None, d, d), lambda l, j: (l, 0, j)),
                  pl.BlockSpec((None, 1, d), lambda l, j: (l, 0, j))],
        out_specs=pl.BlockSpec((None, n, d), lambda l, j: (l, 0, j)),
        out_shape=jax.ShapeDtypeStruct((depth, n, d6), f32),
        compiler_params=_cparams("arbitrary", "arbitrary"), name="ada")(
            c_all, w_ada, b_ada.reshape(depth, 1, d6))


def _router_body(x_ref, sh_ref, sc_ref, nw_ref, wrt_ref, br_ref, tri_ref, cnt0_ref,
                 h_ref, ids_ref, gates_ref, rank_ref, cnt_ref, carry_ref):
    @pl.when(pl.program_id(0) == 0)
    def _():
        carry_ref[...] = cnt0_ref[...]

    h = _norm_mod(x_ref[...], nw_ref[...], sc_ref[...], sh_ref[...])
    h_ref[...] = h.astype(bf16)
    h_hi, h_lo = _split2(h)
    w_hi, w_lo = _split2(wrt_ref[...])
    lg = _nt(w_hi, h_hi) + _nt(w_hi, h_lo) + _nt(w_lo, h_hi) + br_ref[...]
    e = jnp.exp(lg - jnp.max(lg, axis=0, keepdims=True))
    best = None
    for g in range(N_EXPERTS // EXPERTS_PER_GROUP):
        v = [e[EXPERTS_PER_GROUP * g + j:EXPERTS_PER_GROUP * g + j + 1, :] for j in range(EXPERTS_PER_GROUP)]
        t1 = jnp.maximum(jnp.maximum(v[0], v[1]), jnp.maximum(v[2], v[3]))
        i1 = jnp.where(v[0] == t1, 0, jnp.where(v[1] == t1, 1, jnp.where(v[2] == t1, 2, 3)))
        w = [jnp.where(i1 == j, -1.0, v[j]) for j in range(EXPERTS_PER_GROUP)]
        t2 = jnp.maximum(jnp.maximum(w[0], w[1]), jnp.maximum(w[2], w[3]))
        i2 = jnp.where(w[0] == t2, 0, jnp.where(w[1] == t2, 1, jnp.where(w[2] == t2, 2, 3)))
        cand = (t1 + t2, t1, t2, i1 + EXPERTS_PER_GROUP * g, i2 + EXPERTS_PER_GROUP * g)
        if best is None:
            best = cand
        else:
            upd = cand[0] > best[0]
            best = tuple(jnp.where(upd, c, b) for c, b in zip(cand, best))
    s, t1, t2, i1, i2 = best
    ids_ref[0:1, :] = i1.astype(i32)
    ids_ref[1:2, :] = i2.astype(i32)
    gates_ref[0:1, :] = t1 / s
    gates_ref[1:2, :] = t2 / s
    eidx = lax.broadcasted_iota(i32, lg.shape, 0)
    oh1 = (eidx == i1).astype(f32)
    oh2 = (eidx == i2).astype(f32)
    tri = tri_ref[...]
    ex1 = _mm(oh1.astype(bf16), tri)
    ex2 = _mm(oh2.astype(bf16), tri)
    carry = carry_ref[...]
    tot1 = jnp.sum(oh1, axis=1, keepdims=True)
    tot2 = jnp.sum(oh2, axis=1, keepdims=True)
    rank_ref[0:1, :] = jnp.sum(oh1 * (carry + ex1), axis=0, keepdims=True).astype(i32)
    rank_ref[1:2, :] = jnp.sum(oh2 * (carry + tot1 + ex2), axis=0, keepdims=True).astype(i32)
    carry_ref[...] = carry + tot1 + tot2
    cnt_ref[...] = carry_ref[...]


def _invperm_body(dest_ref, gap_ref, src_ref):
    n_tok = dest_ref.shape[0] // 2
    spread = (1 << (n_tok.bit_length() - 1)) - 1

    def fill(i, c):
        src_ref[i] = i & spread
        return c

    for g in range(gap_ref.shape[0] // 2):
        lax.fori_loop(gap_ref[2 * g], gap_ref[2 * g + 1], fill, 0)

    def place(t, c):
        src_ref[dest_ref[t]] = t
        src_ref[dest_ref[n_tok + t]] = t
        return c

    lax.fori_loop(0, n_tok, place, 0, unroll=INVPERM_UNROLL)


def _moe_body(t0, te_ref, nu_ref, x_ref, wgu_ref, wdn_ref, *refs):
    o_ref, wgu_b, wdn_b = refs[-3:]
    i = pl.program_id(0) + t0
    f = wdn_ref.shape[0]

    @pl.when(jnp.logical_or(i == t0, te_ref[i] != te_ref[jnp.maximum(i - 1, 0)]))
    def _():
        wgu_b[...] = wgu_ref[...].astype(bf16)
        wdn_b[...] = wdn_ref[...].astype(bf16)

    @pl.when(i < nu_ref[0])
    def _():
        gu = _mm(x_ref[...], wgu_b[...])
        g = gu[:, :f]
        act = g * _sigmoid(g) * gu[:, f:]
        o_ref[...] = _mm(act.astype(bf16), wdn_b[...]).astype(bf16)

    @pl.when(i >= nu_ref[0])
    def _():
        o_ref[...] = jnp.zeros_like(o_ref)


def _moe_grouped(xs, wgu, wdn, l, tile_expert, n_used):
    d = xs[0].shape[1]
    f2 = wgu.shape[3]
    f = wdn.shape[2]
    n_tiles = sum(x.shape[0] for x in xs) // TM_MOE
    out, t0 = None, 0
    for x in xs:
        nt = x.shape[0] // TM_MOE
        in_specs = [pl.BlockSpec((TM_MOE, d), lambda i, te, nu: (i, 0)),
                    pl.BlockSpec((None, None, d, f2), lambda i, te, nu, _t=t0: (l, te[i + _t], 0, 0)),
                    pl.BlockSpec((None, None, f, d), lambda i, te, nu, _t=t0: (l, te[i + _t], 0, 0))]
        args = [tile_expert, n_used, x, wgu, wdn]
        aliases = {}
        if out is not None:
            in_specs.append(pl.BlockSpec(memory_space=pl.ANY))
            args.append(out)
            aliases = {5: 0}
        gs = pltpu.PrefetchScalarGridSpec(
            num_scalar_prefetch=2, grid=(nt,), in_specs=in_specs,
            out_specs=pl.BlockSpec((TM_MOE, d), lambda i, te, nu, _t=t0: (i + _t, 0)),
            scratch_shapes=[pltpu.VMEM((d, f2), bf16), pltpu.VMEM((f, d), bf16)])
        out = pl.pallas_call(functools.partial(_moe_body, t0), grid_spec=gs,
                             out_shape=jax.ShapeDtypeStruct((n_tiles * TM_MOE, d), bf16),
                             input_output_aliases=aliases,
                             compiler_params=_cparams("arbitrary"), name="moe_experts")(*args)
        t0 += nt
    return out


def _combine_body(x_ref, oa_ref, ob_ref, gt_ref, g2_ref, o_ref):
    gt = gt_ref[...]
    y = gt[:, 0:1] * oa_ref[...].astype(f32) + gt[:, 1:2] * ob_ref[...].astype(f32)
    o_ref[...] = x_ref[...] + g2_ref[...] * y


def _tri(n):
    return (jnp.arange(n)[:, None] < jnp.arange(n)[None, :]).astype(bf16)


def _moe_layer(x_all, dims, l, W, last):
    n_all, d = x_all.shape
    n_p, n_s, seq = dims['n_p'], dims['n_s'], dims['seq']
    modp, mods = dims['modp'], dims['mods']
    consts = [W['norm_ff'][l].reshape(1, d), W['w_router_t'], W['b_router_c']]
    outs = [((H2_ROWS_FACTOR * n_all, d), bf16, 'row'), ((2, n_all), i32, 'col'), ((2, n_all), f32, 'col'),
            ((2, n_all), i32, 'col'), ((N_EXPERTS, 1), f32, 'const')]
    scratch = [pltpu.VMEM((N_EXPERTS, 1), f32)]
    res = _rows_call(_router_body, tb=TB_PROMPT, nblk=n_p // TB_PROMPT, off=0, bps=seq // TB_PROMPT,
                     row_ins=[x_all], mod_ins=[(modp, 3), (modp, 4)], seq_ins=[],
                     consts=consts + [_tri(TB_PROMPT), jnp.zeros((N_EXPERTS, 1), f32)],
                     outs=outs, scratch=scratch, name="router_p")
    h2, ids, gates, rank, cnt = _rows_call(
        _router_body, tb=n_s, nblk=1, off=n_p // n_s, bps=1,
        row_ins=[x_all], mod_ins=[(mods, 3), (mods, 4)], seq_ins=[],
        consts=consts + [_tri(n_s), res[4]], outs=outs, prev=res[:4], scratch=scratch, name="router_s")

    n_tiles = (2 * n_all + N_EXPERTS * (TM_MOE - 1)) // TM_MOE + 1
    counts = cnt[:, 0].astype(i32)
    padded = (counts + TM_MOE - 1) // TM_MOE * TM_MOE
    pend = jnp.cumsum(padded)
    n_used = (pend[-1:] // TM_MOE).astype(i32)
    tile_start = jnp.arange(n_tiles, dtype=i32) * TM_MOE
    tile_expert = jnp.minimum(jnp.sum((pend[None, :] <= tile_start[:, None]).astype(i32), axis=1),
                              N_EXPERTS - 1)
    pstart = (pend - padded).astype(i32)
    sel = ids[:, :, None] == jnp.arange(N_EXPERTS, dtype=i32)[None, None, :]
    dest = rank + jnp.sum(jnp.where(sel, pstart[None, None, :], 0), axis=2)
    gaps = jnp.stack([jnp.append(pstart + counts, pend[-1]),
                      jnp.append(pend, n_tiles * TM_MOE)], axis=1).reshape(-1).astype(i32)
    smem = pl.BlockSpec(memory_space=pltpu.SMEM)
    row_src = pl.pallas_call(
        _invperm_body, in_specs=[smem, smem], out_specs=smem,
        out_shape=jax.ShapeDtypeStruct((n_tiles * TM_MOE,), i32), name="moe_invperm")(dest.reshape(-1), gaps)

    def rows(a, idx):
        return a.at[idx].get(mode='promise_in_bounds')

    cuts = [k * n_tiles // XS_PARTS * TM_MOE for k in range(XS_PARTS + 1)]
    xs = [rows(h2, row_src[cuts[k]:cuts[k + 1]]) for k in range(XS_PARTS)]
    osort = _moe_grouped(xs, W['w_gate_up'], W['w_down'], l, tile_expert, n_used)
    oa = rows(osort, dest[0])
    ob = rows(osort, dest[1])
    gt = gates.T
    if not last:
        (x_new,) = _both_groups(
            _combine_body, dims, row_ins=[x_all, oa, ob, gt], mod_ks=(5,), consts=[],
            outs=[((n_all, d), f32, 'row')], name="combine")
        return x_new
    (y_p,) = _rows_call(_combine_body, tb=TB_PROMPT, nblk=n_p // TB_PROMPT, off=0, bps=seq // TB_PROMPT,
                        row_ins=[x_all, oa, ob, gt], mod_ins=[(modp, 5)], seq_ins=[], consts=[],
                        outs=[((n_p, d), f32, 'own')], name="combine_last_p")
    (y_s,) = _rows_call(_combine_body, tb=n_s, nblk=1, off=n_p // n_s, bps=1,
                        row_ins=[x_all, oa, ob, gt], mod_ins=[(mods, 5)], seq_ins=[], consts=[],
                        outs=[((n_s, d), f32, 'own')], name="combine_last_s")
    return y_p, y_s


def _pool_tail(pooled_slabs, w_ref, b_ref, ls_ref):
    ys = [_mm(p.astype(bf16), w_ref[g]) for g, p in enumerate(pooled_slabs)]
    return (jnp.concatenate(ys, axis=1) + b_ref[...]) * ls_ref[...]


def _pool_prompt_body(bps, x_ref, xh_ref, sh_ref, sc_ref, g1_ref, nw_ref, w_ref, b_ref, ls_ref,
                      o_ref, st_ref, ext_ref):
    tb = x_ref.shape[0]
    gw = w_ref.shape[1]
    blk = pl.program_id(0) % bps
    x = x_ref[...]
    nw, sc, sh = nw_ref[...], sc_ref[...], sh_ref[...]
    h = _norm_mod(x, nw, sc, sh)
    hh = _norm_mod(xh_ref[...], nw, sc, sh)
    ext_ref[0:HALO, :] = jnp.where(blk == 0, 0.0, hh)
    ext_ref[HALO:, :] = h
    pos = blk * tb + lax.broadcasted_iota(i32, (tb, 1), 0)
    slabs = []
    for g, wdw in enumerate(POOL_WINDOWS):
        lo, hi = g * gw, (g + 1) * gw
        acc = h[:, lo:hi]
        for j in range(1, wdw):
            acc = acc + ext_ref[HALO - j:HALO - j + tb, lo:hi]
        cnt = jnp.minimum(pos + 1, wdw).astype(f32)
        slabs.append(acc / cnt - h[:, lo:hi])
    y = _pool_tail(slabs, w_ref, b_ref, ls_ref)
    o_ref[...] = x + g1_ref[...] * y
    st_ref[...] = h[tb - HALO:, :]


def _pool_sample_body(n_t, pos0, x_ref, past_ref, sh_ref, sc_ref, g1_ref, nw_ref, w_ref, b_ref, ls_ref,
                      o_ref, h_ref):
    nb = past_ref.shape[1]
    gw = w_ref.shape[1]
    x = x_ref[...]
    h = _norm_mod(x, nw_ref[...], sc_ref[...], sh_ref[...])
    h_ref[...] = h

    def ext(r, lo, hi):
        if r < POOL_HIST:
            return past_ref[r][:, lo:hi]
        t = r - POOL_HIST
        return h[t * nb:(t + 1) * nb, lo:hi]

    slabs = []
    for g, wdw in enumerate(POOL_WINDOWS):
        lo, hi = g * gw, (g + 1) * gw
        rows = []
        for t in range(n_t):
            acc = ext(POOL_HIST + t, lo, hi)
            for j in range(1, wdw):
                acc = acc + ext(POOL_HIST + t - j, lo, hi)
            cnt = float(min(pos0 + t + 1, wdw))
            rows.append(acc / cnt - ext(POOL_HIST + t, lo, hi))
        slabs.append(jnp.concatenate(rows, axis=0))
    y = _pool_tail(slabs, w_ref, b_ref, ls_ref)
    o_ref[...] = x + g1_ref[...] * y


def _pool_layer(x_all, x_smp, smp_off, dims, l, ip, W, state_pool):
    d = x_all.shape[1]
    n_p, n_s, seq, nb, n_t, batch = (dims[k] for k in ('n_p', 'n_s', 'seq', 'nb', 'n_t', 'batch'))
    n_all = n_p + n_s
    modp, mods = dims['modp'], dims['mods']
    bps = seq // TB_POOL
    consts = [W['norm_mix'][l].reshape(1, d), W['w_pool'][ip].astype(bf16),
              W['b_pool'][ip].reshape(1, d), W['ls_pool'][ip].reshape(1, d)]

    def mspec(k):
        return pl.BlockSpec((None, None, 1, d), lambda i, _k=k: (_k, i // bps, 0, 0))

    ratio = TB_POOL // HALO
    x_new, st = pl.pallas_call(
        functools.partial(_pool_prompt_body, bps), grid=(n_p // TB_POOL,),
        in_specs=[pl.BlockSpec((TB_POOL, d), lambda i: (i, 0)),
                  pl.BlockSpec((HALO, d), lambda i: (jnp.maximum(i * ratio - 1, 0), 0)),
                  mspec(0), mspec(1), mspec(2)] + [_full_spec(c) for c in consts],
        out_specs=[pl.BlockSpec((TB_POOL, d), lambda i: (i, 0)),
                   pl.BlockSpec((None, HALO, d), lambda i: (i // bps, 0, 0))],
        out_shape=[jax.ShapeDtypeStruct((n_all, d), f32), jax.ShapeDtypeStruct((batch, HALO, d), f32)],
        scratch_shapes=[pltpu.VMEM((TB_POOL + HALO, d), f32)],
        compiler_params=_cparams("arbitrary"), name="pool_p")(
            x_all, x_all, modp, modp, modp, *consts)
    pool_prompt = st[:, HALO - POOL_HIST:, :]

    past = jnp.transpose(state_pool[ip], (1, 0, 2))
    off = n_p // n_s

    def sspec(k):
        return pl.BlockSpec((None, None, n_s, d), lambda i, _k=k: (_k, 0, 0, 0))

    def body(x_ref, past_ref, sh, sc, g1, nw, w, b, ls, prev_ref, o_ref, h_ref):
        _pool_sample_body(n_t, dims['past_len'], x_ref, past_ref, sh, sc, g1, nw, w, b, ls, o_ref, h_ref)

    x_new, h_s = pl.pallas_call(
        body, grid=(1,),
        in_specs=[pl.BlockSpec((n_s, d), lambda i: (smp_off, 0)), _full_spec(past),
                  sspec(0), sspec(1), sspec(2)] + [_full_spec(c) for c in consts]
                 + [pl.BlockSpec(memory_space=pl.ANY)],
        out_specs=[pl.BlockSpec((n_s, d), lambda i: (off, 0)), pl.BlockSpec((n_s, d), lambda i: (0, 0))],
        out_shape=[jax.ShapeDtypeStruct((n_all, d), f32), jax.ShapeDtypeStruct((n_s, d), f32)],
        input_output_aliases={5 + len(consts): 0},
        compiler_params=_cparams("arbitrary"), name="pool_s")(
            x_smp, past, mods, mods, mods, *consts, x_new)
    ext = jnp.concatenate([past, h_s.reshape(n_t, nb, d)], axis=0)
    pool_sample = jnp.transpose(ext[-POOL_HIST:], (1, 0, 2))
    return x_new, pool_prompt, pool_sample


def _ssm_param_body(lr_ref, li_ref, ldt_ref, apr_ref, api_ref, kr_ref, ki_ref):
    lr, li = lr_ref[...], li_ref[...]
    dt = jnp.exp(ldt_ref[...])
    mag = jnp.exp(lr * dt)
    a_r, a_i = mag * jnp.cos(li * dt), mag * jnp.sin(li * dt)
    den = lr * lr + li * li
    nr, ni = a_r - 1.0, a_i
    kr_ref[...] = (nr * lr + ni * li) / den
    ki_ref[...] = (ni * lr - nr * li) / den
    pr, pi_ = a_r, a_i
    for k in range(8):
        apr_ref[k] = pr
        api_ref[k] = pi_
        pr, pi_ = pr * a_r - pi_ * a_i, pr * a_i + pi_ * a_r


def _ssm_tables(W, iq):
    g, p = W['ssm_lambda_re'][iq].shape
    c = SSM_GROUP_CH
    ns = g // SLAB_GROUPS
    sl = SLAB_GROUPS
    shp = [jax.ShapeDtypeStruct((8, g, p), f32)] * 2 + [jax.ShapeDtypeStruct((g, p), f32)] * 2
    apr, api, k_r, k_i = pl.pallas_call(_ssm_param_body, out_shape=shp, name="ssm_params")(
        W['ssm_lambda_re'][iq], W['ssm_lambda_im'][iq], W['ssm_log_dt'][iq].reshape(g, 1))
    b_re, b_im = W['ssm_b_re'][iq], W['ssm_b_im'][iq]
    bb_re = k_r[..., None] * b_re - k_i[..., None] * b_im
    bb_im = k_r[..., None] * b_im + k_i[..., None] * b_re
    eye = jnp.eye(sl, dtype=f32)

    def b_slab(bb):
        b4 = bb.reshape(ns, sl, p, c)
        return jnp.einsum('sgpc,gh->sgchp', b4, eye).reshape(ns, sl * c, sl * p)

    def c_slab(cc):
        c4 = cc.reshape(ns, sl, c, p)
        return jnp.einsum('sgcp,gh->shpgc', c4, eye).reshape(ns, sl * p, sl * c)

    bmat = jnp.concatenate([b_slab(bb_re), b_slab(bb_im)], axis=2).astype(bf16)
    cmat = jnp.concatenate([c_slab(W['ssm_c_re'][iq]), -c_slab(W['ssm_c_im'][iq])], axis=1).astype(bf16)

    def lay(a):
        return jnp.transpose(a.reshape(8, ns, sl * p), (1, 0, 2))

    apw = jnp.stack([lay(apr), lay(api)], axis=1)
    return bmat, cmat, apw


def _ssm_out(x, h, y, g1_ref, dsk_ref, wglu_ref, bglu_ref):
    d = x.shape[1]
    z = _gelu_tanh(y + dsk_ref[...] * h)
    gu = _mm(z.astype(bf16), wglu_ref[...]) + bglu_ref[...]
    return x + g1_ref[...] * (gu[:, :d] * _sigmoid(gu[:, d:]))


def _ssm_prompt_body(x_ref, sh_ref, sc_ref, g1_ref, nw_ref, b_ref, c_ref, t2_ref, dsk_ref,
                     wglu_ref, bglu_ref, o_ref, st_ref, xs_ref, hs_ref, carry_ref, y_ref):
    t = x_ref.shape[0]
    ns = b_ref.shape[0]
    sw = b_ref.shape[1]
    half = b_ref.shape[2] // 2
    hc = half // LANES

    @pl.when(pl.program_id(1) == 0)
    def _():
        carry_ref[...] = jnp.zeros_like(carry_ref)

    x = x_ref[...]
    h = _norm_mod(x, nw_ref[...], sc_ref[...], sh_ref[...])
    hb = h.astype(bf16)
    def expand(s):
        xx = _mm(hb[:, s * sw:(s + 1) * sw], b_ref[s])
        for c in range(2 * hc):
            xs_ref[s % 3, c] = xx[:, c * LANES:(c + 1) * LANES]

    def group_scan(s):
        b = s % 3
        for c in range(hc):
            ar = jnp.broadcast_to(t2_ref[s, 0, 0:1, c * LANES:(c + 1) * LANES], (8, LANES))
            ai = jnp.broadcast_to(t2_ref[s, 1, 0:1, c * LANES:(c + 1) * LANES], (8, LANES))
            for base in range(0, t, 64):
                hr = xs_ref[b, c, pl.ds(base, 8, stride=8), :]
                hi = xs_ref[b, hc + c, pl.ds(base, 8, stride=8), :]
                for i in range(1, 8):
                    rows = pl.ds(base + i, 8, stride=8)
                    hr, hi = (ar * hr - ai * hi + xs_ref[b, c, rows, :],
                              ar * hi + ai * hr + xs_ref[b, hc + c, rows, :])
                    xs_ref[b, c, rows, :] = hr
                    xs_ref[b, hc + c, rows, :] = hi

    def carry_scan(s):
        b = s % 3

        def grp(j, car):
            cr, ci = car
            r0 = pl.multiple_of(j * 8, 8)
            xr = jnp.concatenate([xs_ref[b, c, pl.ds(r0, 8), :] for c in range(hc)], axis=1)
            xi = jnp.concatenate([xs_ref[b, hc + c, pl.ds(r0, 8), :] for c in range(hc)], axis=1)
            pr = t2_ref[s, 0]
            pi_ = t2_ref[s, 1]
            hr = xr + pr * cr - pi_ * ci
            hi = xi + pr * ci + pi_ * cr
            hs_ref[s % 2, pl.ds(r0, 8), 0:half] = hr
            hs_ref[s % 2, pl.ds(r0, 8), half:] = hi
            return (jnp.broadcast_to(hr[7:8, :], hr.shape), jnp.broadcast_to(hi[7:8, :], hi.shape))

        cr, ci = lax.fori_loop(0, t // 8, grp, (carry_ref[s, :, 0:half], carry_ref[s, :, half:]))
        carry_ref[s, :, 0:half] = cr
        carry_ref[s, :, half:] = ci
        st_ref[s:s + 1, :] = jnp.concatenate([cr[0:1, :], ci[0:1, :]], axis=1)

    expand(0)
    if ns > 1:
        expand(1)
    group_scan(0)
    for s in range(ns):
        carry_scan(s)
        y_ref[:, s * sw:(s + 1) * sw] = _mm(hs_ref[s % 2].astype(bf16), c_ref[s])
        if s + 2 < ns:
            expand(s + 2)
        if s + 1 < ns:
            group_scan(s + 1)
    y = y_ref[...]
    o_ref[...] = _ssm_out(x, h, y, g1_ref, dsk_ref, wglu_ref, bglu_ref)


def _ssm_sample_body(n_t, x_ref, s0_ref, sh_ref, sc_ref, g1_ref, nw_ref, b_ref, c_ref, t2_ref, dsk_ref,
                     wglu_ref, bglu_ref, prev_ref, o_ref, so_ref, y_ref):
    ns = b_ref.shape[0]
    sw = b_ref.shape[1]
    half = b_ref.shape[2] // 2
    nb = s0_ref.shape[1]
    x = x_ref[...]
    h = _norm_mod(x, nw_ref[...], sc_ref[...], sh_ref[...])
    hb = h.astype(bf16)
    for s in range(ns):
        s_r = s0_ref[s, :, 0:half]
        s_i = s0_ref[s, :, half:]
        ar = t2_ref[s, 0, 0:1, :]
        ai = t2_ref[s, 1, 0:1, :]
        for t in range(n_t):
            xx = _mm(hb[t * nb:(t + 1) * nb, s * sw:(s + 1) * sw], b_ref[s])
            s_r, s_i = ar * s_r - ai * s_i + xx[:, 0:half], ar * s_i + ai * s_r + xx[:, half:]
            st = jnp.concatenate([s_r, s_i], axis=1)
            y_ref[t * nb:(t + 1) * nb, s * sw:(s + 1) * sw] = _mm(st.astype(bf16), c_ref[s])
        so_ref[s] = jnp.concatenate([s_r, s_i], axis=1)
    o_ref[...] = _ssm_out(x, h, y_ref[...], g1_ref, dsk_ref, wglu_ref, bglu_ref)


def _ssm_layer(x_all, dims, l, iq, W, st_re, st_im):
    n_all, d = x_all.shape
    n_p, n_s, seq, nb, n_t, batch = (dims[k] for k in ('n_p', 'n_s', 'seq', 'nb', 'n_t', 'batch'))
    modp, mods = dims['modp'], dims['mods']
    bmat, cmat, t2 = _ssm_tables(W, iq)
    ns = bmat.shape[0]
    sp = bmat.shape[2]
    g, p = W['ssm_lambda_re'][iq].shape
    sl = SLAB_GROUPS
    tail = [W['ssm_d'][iq].reshape(1, d), W['w_glu'][iq].astype(bf16), W['b_glu'][iq].reshape(1, 2 * d)]
    nw = W['norm_mix'][l].reshape(1, d)
    nchunk = seq // T_SSM

    def mspec(k):
        return pl.BlockSpec((None, None, 1, d), lambda b, c, _k=k: (_k, b, 0, 0))

    def cspec(a):
        return pl.BlockSpec(a.shape, lambda b, c, _n=a.ndim: (0,) * _n)

    consts = [nw, bmat, cmat, t2] + tail
    x_new, st = pl.pallas_call(
        _ssm_prompt_body, grid=(batch, nchunk),
        in_specs=[pl.BlockSpec((T_SSM, d), lambda b, c: (b * nchunk + c, 0)), mspec(0), mspec(1), mspec(2)]
                 + [cspec(a) for a in consts],
        out_specs=[pl.BlockSpec((T_SSM, d), lambda b, c: (b * nchunk + c, 0)),
                   pl.BlockSpec((None, ns, sp), lambda b, c: (b, 0, 0))],
        out_shape=[jax.ShapeDtypeStruct((n_all, d), f32), jax.ShapeDtypeStruct((batch, ns, sp), f32)],
        scratch_shapes=[pltpu.VMEM((3, sp // LANES, T_SSM, LANES), f32), pltpu.VMEM((2, T_SSM, sp), f32),
                        pltpu.VMEM((ns, 8, sp), f32), pltpu.VMEM((T_SSM, d), f32)],
        compiler_params=_cparams("arbitrary", "arbitrary"), name="ssm_p")(
            x_all, modp, modp, modp, *consts)

    def unslab(a):
        a5 = a.reshape(a.shape[0], ns, 2, sl, p)
        return a5[:, :, 0].reshape(-1, g, p), a5[:, :, 1].reshape(-1, g, p)

    re_p, im_p = unslab(st)

    def slab(a):
        return a.reshape(a.shape[0], ns, sl * p)

    s0 = jnp.transpose(jnp.concatenate([slab(st_re), slab(st_im)], axis=2), (1, 0, 2))
    off = n_p // n_s

    def sspec(k):
        return pl.BlockSpec((None, None, n_s, d), lambda i, _k=k: (_k, 0, 0, 0))

    consts_s = [nw, bmat, cmat, t2] + tail
    x_new, so = pl.pallas_call(
        functools.partial(_ssm_sample_body, n_t), grid=(1,),
        in_specs=[pl.BlockSpec((n_s, d), lambda i: (off, 0)), _full_spec(s0), sspec(0), sspec(1), sspec(2)]
                 + [_full_spec(a) for a in consts_s] + [pl.BlockSpec(memory_space=pl.ANY)],
        out_specs=[pl.BlockSpec((n_s, d), lambda i: (off, 0)), _full_spec(s0)],
        out_shape=[jax.ShapeDtypeStruct((n_all, d), f32), jax.ShapeDtypeStruct(s0.shape, f32)],
        scratch_shapes=[pltpu.VMEM((n_s, d), f32)],
        input_output_aliases={5 + len(consts_s): 0},
        compiler_params=_cparams("arbitrary"), name="ssm_s")(
            x_all, s0, mods, mods, mods, *consts_s, x_new)
    re_s, im_s = unslab(jnp.transpose(so, (1, 0, 2)))
    return x_new, re_p, im_p, re_s, im_s


def _qkv_body(prompt, x_ref, sh_ref, sc_ref, cos_ref, sin_ref, nw_ref, w_ref, qn_ref, kn_ref, seg_ref, segt_ref,
              kb_ref, k_ref, v_ref, a_ref, b_ref):
    d = x_ref.shape[1]
    h = _norm_mod(x_ref[...], nw_ref[...], sc_ref[...], sh_ref[...])
    qkv = _mm(h.astype(bf16), w_ref[...])
    rep = d // LANES
    cos = jnp.concatenate([cos_ref[...]] * rep, axis=1)
    sin = jnp.concatenate([sin_ref[...]] * rep, axis=1)
    lane_lo = (lax.broadcasted_iota(i32, (1, d), 1) % HEAD_DIM) < (HEAD_DIM // 2)
    seg, segt = seg_ref[...], segt_ref[...]

    def norm_rope(t, g):
        s_hi, s_lo = _split2(t * t)
        ms = (_mm(s_hi, seg) + _mm(s_lo, seg)) * (1.0 / HEAD_DIM)
        r_hi, r_lo = _split2(lax.rsqrt(ms + NORM_EPS))
        rf = _mm(r_hi, segt) + _mm(r_lo, segt)
        tn = t * rf * g
        sw = jnp.where(lane_lo, pltpu.roll(tn, d - HEAD_DIM // 2, 1), pltpu.roll(tn, HEAD_DIM // 2, 1))
        return tn * cos + sw * sin

    q = norm_rope(qkv[:, :d], qn_ref[...])
    k = norm_rope(qkv[:, d:2 * d], kn_ref[...])
    v = qkv[:, 2 * d:]
    kb_ref[...] = k.astype(bf16)
    if prompt:
        tb = k.shape[0]
        for hh in range(N_HEADS):
            cols = slice(hh * 2 * HEAD_DIM, (hh + 1) * 2 * HEAD_DIM)
            k_ref[pl.ds(hh, tb, stride=N_HEADS), :] = k[:, cols]
            v_ref[pl.ds(hh, tb, stride=N_HEADS), :] = v[:, cols]
    else:
        k_ref[...] = k
        v_ref[...] = v
    if prompt:
        a_ref[...] = (q * (HEAD_DIM ** -0.5 * LOG2E)).T.astype(bf16)
        tk = b_ref.shape[2]
        for c in range(b_ref.shape[0]):
            b_ref[c] = v[c * tk:(c + 1) * tk, :].T.astype(bf16)
    else:
        a_ref[...] = (q * (HEAD_DIM ** -0.5)).astype(bf16)
        b_ref[...] = v.astype(bf16)


def _lam(lam_ref):
    l4 = lam_ref[...]
    a = jnp.sum(l4[0:1, :] * l4[1:2, :], axis=1, keepdims=True)
    b = jnp.sum(l4[2:3, :] * l4[3:4, :], axis=1, keepdims=True)
    return jnp.exp(a) - jnp.exp(b)


def _attn_finish(acc, l, lam, lam_init, sub_ref):
    r = acc.shape[0] // 2
    o = acc[:r] / l[:r] - lam * (acc[r:] / l[r:])
    o = o * lax.rsqrt(jnp.mean(o * o, axis=-1, keepdims=True) + NORM_EPS) * sub_ref[...]
    return o * (1.0 - lam_init)


def _flash_body(lam_init, qt_ref, k_ref, vt_ref, lam_ref, sub_ref, o_ref, acc_ref, s_ref, p_ref):
    tq = qt_ref.shape[1]
    tk = vt_ref.shape[2]
    ratio = tq // tk
    hd2 = 2 * HEAD_DIM
    nh = qt_ref.shape[0] // hd2
    qi = pl.program_id(2)
    top = lax.broadcasted_iota(i32, (hd2, 1), 0) < HEAD_DIM
    qqs = []
    for h in range(nh):
        qt = qt_ref[h * hd2:(h + 1) * hd2, :]
        zero = jnp.zeros_like(qt)
        qqs.append(jnp.concatenate([jnp.where(top, qt, zero), jnp.where(top, zero, qt)], axis=1))
    acc_ref[...] = jnp.zeros_like(acc_ref)
    p_ref[...] = jnp.zeros_like(p_ref)

    def scores(j):
        r0 = pl.multiple_of(j * tk, tk)
        return [_mm(k_ref[pl.ds(r0, tk), h * hd2:(h + 1) * hd2], qqs[h]) for h in range(nh)]

    def keep(ss, slot):
        for h in range(nh):
            s_ref[slot, h] = ss[h]

    def values(j):
        return [_mm(vt_ref[j, h * hd2:(h + 1) * hd2, :], p_ref[h]) for h in range(nh)]

    def softmax(slot, carry, key0):
        out = []
        for h in range(nh):
            m, l, _ = carry[h]
            s = s_ref[slot, h]
            if key0 is not None:
                key = lax.broadcasted_iota(i32, s.shape, 0) + key0
                qry = lax.broadcasted_iota(i32, s.shape, 1) % tq
                s = jnp.where(key <= qry, s, NEG_INF)
            m_new = jnp.maximum(m, jnp.max(s, axis=0, keepdims=True))
            alpha = jnp.exp2(m - m_new)
            p = jnp.exp2(s - m_new)
            p_ref[h] = p.astype(bf16)
            out.append((m_new, alpha * l + jnp.sum(p, axis=0, keepdims=True), alpha))
        return tuple(out)

    def accumulate(pvs, carry):
        for h in range(nh):
            acc_ref[h] = carry[h][2] * acc_ref[h] + pvs[h]

    def step(j, slot, carry):
        pvs = values(jnp.maximum(j - 1, 0))
        ss = scores(j + 1)
        new = softmax(slot, carry, None)
        keep(ss, 1 - slot)
        accumulate(pvs, carry)
        return new

    def finish(j, slot, carry):
        for u in range(ratio):
            pvs = values(jnp.maximum(j + u - 1, 0))
            if u + 1 < ratio:
                ss = scores(j + u + 1)
            new = softmax(slot, carry, u * tk)
            if u + 1 < ratio:
                keep(ss, 1 - slot)
            accumulate(pvs, carry)
            carry, slot = new, 1 - slot
        accumulate(values(j + ratio - 1), carry)
        return tuple(c[1] for c in carry)

    keep(scores(0), 0)
    init = tuple((jnp.full((1, 2 * tq), NEG_INF, f32), jnp.zeros((1, 2 * tq), f32),
                  jnp.ones((1, 2 * tq), f32)) for _ in range(nh))
    n_full = ratio * qi
    carry = lax.fori_loop(0, n_full // 2, lambda i, c: step(2 * i + 1, 1, step(2 * i, 0, c)), init)
    if ratio % 2 == 0:
        ls = finish(n_full, 0, carry)
    else:
        ls = lax.cond(n_full % 2 == 0, lambda c: finish(n_full, 0, c),
                      lambda c: finish(n_full, 1, step(n_full - 1, 0, c)), carry)
    lam = _lam(lam_ref) + lam_init
    for h in range(nh):
        acc = acc_ref[h]
        l = ls[h]
        o = acc[:, :tq] / l[:, :tq] - lam * (acc[:, tq:] / l[:, tq:])
        o = o * lax.rsqrt(jnp.mean(o * o, axis=0, keepdims=True) + NORM_EPS) * sub_ref[...]
        o_ref[:, h * hd2:(h + 1) * hd2] = (o * (1.0 - lam_init)).T.astype(o_ref.dtype)


def _attn_sample_body(lam_init, pp, pt_ref, q_ref, kn_ref, vn_ref, mp_ref, mn_ref, lam_ref, sub_ref, *rest):
    k_refs, v_refs = rest[:pp], rest[pp:2 * pp]
    o_ref = rest[2 * pp]
    m_ref, l_ref, acc_ref = rest[2 * pp + 1:]
    j = pl.program_id(1)
    q = q_ref[...]

    @pl.when(j == 0)
    def _():
        m_ref[...] = jnp.full_like(m_ref, NEG_INF)
        l_ref[...] = jnp.zeros_like(l_ref)
        acc_ref[...] = jnp.zeros_like(acc_ref)

    def lane_fold(op, a):
        parts = [a[:, c:c + LANES] for c in range(0, a.shape[1], LANES)]
        while len(parts) > 1:
            parts = [op(parts[i], parts[i + 1]) for i in range(0, len(parts) - 1, 2)] + parts[len(parts) & ~1:]
        return parts[0]

    def update(kfs, vfs, mask):
        ss = [_nt(q, kf) + mask for kf in kfs]
        mx = ss[0] if ss[0].shape[1] < LANES else functools.reduce(jnp.maximum, [lane_fold(jnp.maximum, s) for s in ss])
        m = m_ref[...]
        m_new = jnp.maximum(m, jnp.max(mx, axis=1, keepdims=True))
        alpha = jnp.exp(m - m_new)
        ps = [jnp.exp(s - m_new) for s in ss]
        sm = ps[0] if ps[0].shape[1] < LANES else functools.reduce(jnp.add, [lane_fold(jnp.add, p) for p in ps])
        l_ref[...] = alpha * l_ref[...] + jnp.sum(sm, axis=1, keepdims=True)
        pv = functools.reduce(jnp.add, [_mm(p.astype(bf16), vf) for p, vf in zip(ps, vfs)])
        acc_ref[...] = alpha * acc_ref[...] + pv
        m_ref[...] = m_new

    update([r[...].astype(bf16) for r in k_refs], [r[...].astype(bf16) for r in v_refs], mp_ref[...])

    @pl.when(j == pl.num_programs(1) - 1)
    def _():
        update([kn_ref[...]], [vn_ref[...]], mn_ref[...])
        lam = _lam(lam_ref) + lam_init
        o_ref[...] = _attn_finish(acc_ref[...], l_ref[...], lam, lam_init, sub_ref)


def _oproj_body(x_ref, o_ref_in, g1_ref, w_ref, o_ref):
    o_ref[...] = x_ref[...] + g1_ref[...] * _mm(o_ref_in[...], w_ref[...])


def _attn_layer(x_all, dims, l, ia, W, cache_k, cache_v, page_table):
    n_all, d = x_all.shape
    n_p, n_s, seq, nb, n_t, batch = (dims[k] for k in ('n_p', 'n_s', 'seq', 'nb', 'n_t', 'batch'))
    past_len = dims['past_len']
    hd2 = 2 * HEAD_DIM
    lam_init = 0.8 - 0.6 * math.exp(-0.3 * l)

    half = HEAD_DIM // 2
    inv = jnp.power(ROPE_THETA, -jnp.arange(half, dtype=f32) * (2.0 / HEAD_DIM))

    def tables(pos):
        ang = pos.astype(f32)[:, None] * inv[None, :]
        cos, sin = jnp.cos(ang), jnp.sin(ang)
        return (jnp.concatenate([cos] * 4, axis=1), jnp.concatenate([-sin, sin, -sin, sin], axis=1))

    cos_p, sin_p = tables(jnp.arange(seq))
    cos_s, sin_s = tables(past_len + jnp.repeat(jnp.arange(n_t), nb))
    seg = (jnp.arange(d)[:, None] // HEAD_DIM == jnp.arange(d // HEAD_DIM)[None, :]).astype(bf16)
    consts = [W['norm_mix'][l].reshape(1, d), W['w_qkv'][ia].astype(bf16),
              jnp.tile(W['q_norm'][ia], d // HEAD_DIM).reshape(1, d),
              jnp.tile(W['k_norm'][ia], d // HEAD_DIM).reshape(1, d), seg, seg.T]
    modp, mods = dims['modp'], dims['mods']
    tk = TK_ATTN
    kb, k_pr, v_pr, qt, vt = _rows_call(
        functools.partial(_qkv_body, True), tb=TB_QKV, nblk=n_p // TB_QKV, off=0, bps=seq // TB_QKV,
        row_ins=[x_all], mod_ins=[(modp, 0), (modp, 1)], seq_ins=[cos_p, sin_p], consts=consts,
        outs=[((n_all, d), bf16, 'row'), ((n_p * N_HEADS, hd2), f32, 'own'), ((n_p * N_HEADS, hd2), f32, 'own'),
              ((d, n_p), bf16, 'col'), ((n_p // tk, d, tk), bf16, 'blk3')], name="qkv_p")
    kb, k_sm, v_sm, qs, vb = _rows_call(
        functools.partial(_qkv_body, False), tb=n_s, nblk=1, off=n_p // n_s, bps=1,
        row_ins=[x_all], mod_ins=[(mods, 0), (mods, 1)], seq_ins=[cos_s, sin_s], consts=consts,
        outs=[((n_all, d), bf16, 'row'), ((n_s, d), f32, 'own'), ((n_s, d), f32, 'own'),
              ((n_s, d), bf16, 'own'), ((n_s, d), bf16, 'own')], prev=[kb], name="qkv_s")

    lam4 = jnp.stack([W['lambda_q1'][ia], W['lambda_k1'][ia], W['lambda_q2'][ia], W['lambda_k2'][ia]])
    sub = W['subln'][ia].reshape(1, hd2)
    sub_c = W['subln'][ia].reshape(hd2, 1)
    nq = seq // TQ_ATTN
    hw = HEADS_PER_STEP * hd2
    o_all = pl.pallas_call(
        functools.partial(_flash_body, lam_init), grid=(batch, N_HEADS // HEADS_PER_STEP, nq),
        in_specs=[pl.BlockSpec((hw, TQ_ATTN), lambda b, h, i: (h, b * nq + i)),
                  pl.BlockSpec((seq, hw), lambda b, h, i: (b, h)),
                  pl.BlockSpec((seq // tk, hw, tk), lambda b, h, i: (b, h, 0)),
                  pl.BlockSpec(lam4.shape, lambda b, h, i: (0, 0)),
                  pl.BlockSpec(sub_c.shape, lambda b, h, i: (0, 0))],
        out_specs=pl.BlockSpec((TQ_ATTN, hw), lambda b, h, i: (b * nq + i, h)),
        out_shape=jax.ShapeDtypeStruct((n_all, d), bf16),
        scratch_shapes=[pltpu.VMEM((HEADS_PER_STEP, hd2, 2 * TQ_ATTN), f32),
                        pltpu.VMEM((2, HEADS_PER_STEP, tk, 2 * TQ_ATTN), f32),
                        pltpu.VMEM((HEADS_PER_STEP, tk, 2 * TQ_ATTN), bf16)],
        compiler_params=_cparams("arbitrary", "arbitrary", "arbitrary"), name="flash_p")(
            qt, kb, vt, lam4, sub_c)

    na, n_phys, page, _, _ = cache_k.shape
    flat = page * N_HEADS
    ck = cache_k.reshape(na * n_phys, flat, hd2)
    cv = cache_v.reshape(na * n_phys, flat, hd2)
    n_pages = page_table.shape[1]
    pp = PAGES_PER_STEP if n_pages % PAGES_PER_STEP == 0 else 1
    rows = 2 * N_HEADS * n_t
    q5 = jnp.transpose(qs.reshape(n_t, nb, N_HEADS, 2, HEAD_DIM), (1, 3, 2, 0, 4))
    z = jnp.zeros_like(q5[:, 0])
    qm = jnp.stack([jnp.concatenate([q5[:, 0], z], -1), jnp.concatenate([z, q5[:, 1]], -1)], axis=1)
    qm = qm.reshape(nb, rows, hd2)

    def new_rows(a):
        return jnp.transpose(a.reshape(n_t, nb, N_HEADS, hd2), (1, 0, 2, 3)).reshape(nb, n_t * N_HEADS, hd2)

    kn, vn = new_rows(kb[n_p:]), new_rows(vb)
    r_h = (jnp.arange(rows) % (N_HEADS * n_t)) // n_t
    r_t = jnp.arange(rows) % n_t
    mask_p = jnp.where(r_h[:, None] == (jnp.arange(flat) % N_HEADS)[None, :], 0.0, NEG_INF).astype(f32)
    cn = jnp.arange(n_t * N_HEADS)
    mask_n = jnp.where((r_h[:, None] == (cn % N_HEADS)[None, :]) & ((cn // N_HEADS)[None, :] <= r_t[:, None]),
                       0.0, NEG_INF).astype(f32)
    base = ia * n_phys

    def seq_spec(shape):
        return pl.BlockSpec((None,) + shape, lambda s, j, pt: (s, 0, 0))

    def cst_spec(a):
        return pl.BlockSpec(a.shape, lambda s, j, pt: (0, 0))

    def page_spec(u):
        return pl.BlockSpec((None, flat, hd2), lambda s, j, pt, _u=u: (base + pt[s, j * pp + _u], 0, 0))

    gs = pltpu.PrefetchScalarGridSpec(
        num_scalar_prefetch=1, grid=(nb, n_pages // pp),
        in_specs=[seq_spec((rows, hd2)), seq_spec((n_t * N_HEADS, hd2)), seq_spec((n_t * N_HEADS, hd2)),
                  cst_spec(mask_p), cst_spec(mask_n), cst_spec(lam4), cst_spec(sub)]
                 + [page_spec(u) for u in range(pp)] + [page_spec(u) for u in range(pp)],
        out_specs=pl.BlockSpec((None, rows // 2, hd2), lambda s, j, pt: (s, 0, 0)),
        scratch_shapes=[pltpu.VMEM((rows, 1), f32), pltpu.VMEM((rows, 1), f32), pltpu.VMEM((rows, hd2), f32)])
    o_s = pl.pallas_call(
        functools.partial(_attn_sample_body, lam_init, pp), grid_spec=gs,
        out_shape=jax.ShapeDtypeStruct((nb, rows // 2, hd2), f32),
        compiler_params=_cparams("arbitrary", "arbitrary"), name="attn_s")(
            page_table, qm, kn, vn, mask_p, mask_n, lam4, sub, *([ck] * pp), *([cv] * pp))
    o_s = jnp.transpose(o_s.reshape(nb, N_HEADS, n_t, hd2), (2, 0, 1, 3)).reshape(n_s, d)
    o_all = lax.dynamic_update_slice(o_all, o_s.astype(bf16), (n_p, 0))

    (x_new,) = _both_groups(
        _oproj_body, dims, row_ins=[x_all, o_all], mod_ks=(2,), consts=[W['w_o'][ia].astype(bf16)],
        outs=[((n_all, d), f32, 'row')], name="oproj")

    def smp(a):
        return jnp.transpose(a.reshape(n_t, nb, N_HEADS, hd2), (1, 0, 2, 3))

    shp = (batch, seq, N_HEADS, hd2)
    return x_new, k_pr.reshape(shp), v_pr.reshape(shp), smp(k_sm), smp(v_sm)


def kernel(x_prompt, x_sample, cache_k, cache_v, state_pool, state_ssm_re, state_ssm_im, page_table, c_prompt, c_sample, w_ada, b_ada, norm_mix, norm_ff, w_pool, b_pool, ls_pool, ssm_lambda_re, ssm_lambda_im, ssm_log_dt, ssm_b_re, ssm_b_im, ssm_c_re, ssm_c_im, ssm_d, w_glu, b_glu, w_qkv, q_norm, k_norm, lambda_q1, lambda_k1, lambda_q2, lambda_k2, subln, w_o, w_router, b_router, w_gate_up, w_down):
    batch, seq, d = x_prompt.shape
    nb, n_t, _ = x_sample.shape
    depth = w_ada.shape[0]
    n_p, n_s = batch * seq, nb * n_t
    past_len = page_table.shape[1] * cache_k.shape[2]
    W = dict(norm_mix=norm_mix, norm_ff=norm_ff, w_pool=w_pool, b_pool=b_pool, ls_pool=ls_pool,
             ssm_lambda_re=ssm_lambda_re, ssm_lambda_im=ssm_lambda_im, ssm_log_dt=ssm_log_dt,
             ssm_b_re=ssm_b_re, ssm_b_im=ssm_b_im, ssm_c_re=ssm_c_re, ssm_c_im=ssm_c_im, ssm_d=ssm_d,
             w_glu=w_glu, b_glu=b_glu, w_qkv=w_qkv, q_norm=q_norm, k_norm=k_norm, lambda_q1=lambda_q1,
             lambda_k1=lambda_k1, lambda_q2=lambda_q2, lambda_k2=lambda_k2, subln=subln, w_o=w_o,
             w_router_t=w_router.T, b_router_c=b_router.reshape(-1, 1),
             w_gate_up=w_gate_up, w_down=w_down)

    ada = _ada_all(jnp.concatenate([c_prompt, c_sample], axis=0), w_ada, b_ada)
    x_all = x_prompt.reshape(n_p, d)
    x_smp, smp_off = jnp.transpose(x_sample, (1, 0, 2)).reshape(n_s, d), 0
    dims = dict(n_p=n_p, n_s=n_s, seq=seq, nb=nb, n_t=n_t, batch=batch, past_len=past_len)

    pools_p, pools_s, k_ps, v_ps, k_ss, v_ss = [], [], [], [], [], []
    re_ps, im_ps, re_ss, im_ss = [], [], [], []
    ip = iq = ia = 0
    for l in range(depth):
        chunks = jnp.transpose(ada[l].reshape(batch + nb, 6, d), (1, 0, 2))
        dims['modp'] = chunks[:, :batch, None, :]
        dims['mods'] = jnp.tile(chunks[:, batch:], (1, n_t, 1))[:, None]
        kind = l % N_MIXERS
        if kind == 0:
            x_all, pp_, ps_ = _pool_layer(x_all, x_smp, smp_off, dims, l, ip, W, state_pool)
            pools_p.append(pp_)
            pools_s.append(ps_)
            ip += 1
        elif kind == 1:
            x_all, rp, imp, rs, ims = _ssm_layer(x_all, dims, l, iq, W, state_ssm_re[iq], state_ssm_im[iq])
            re_ps.append(rp)
            im_ps.append(imp)
            re_ss.append(rs)
            im_ss.append(ims)
            iq += 1
        else:
            x_all, kp, vp, ks, vs = _attn_layer(x_all, dims, l, ia, W, cache_k, cache_v, page_table)
            k_ps.append(kp)
            v_ps.append(vp)
            k_ss.append(ks)
            v_ss.append(vs)
            ia += 1
        x_all = _moe_layer(x_all, dims, l, W, l == depth - 1)
        x_smp, smp_off = x_all, n_p // n_s

    y_prompt = x_all[0].reshape(batch, seq, d)
    y_sample = jnp.transpose(x_all[1].reshape(n_t, nb, d), (1, 0, 2))
    return (y_prompt, y_sample, jnp.stack(k_ps), jnp.stack(v_ps), jnp.stack(k_ss), jnp.stack(v_ss),
            jnp.stack(pools_p), jnp.stack(pools_s), jnp.stack(re_ps), jnp.stack(im_ps),
            jnp.stack(re_ss), jnp.stack(im_ss))
```
